```python
import jax, jax.numpy as jnp
from jax import lax
import numpy as np

D_MODEL = 2048
BATCH = 4
SEQ = 4096
DEPTH = 1

CHUNK = 64
HEAD_DIM = 64
D_MIX = D_MODEL
N_HEADS_A = (D_MIX // 2) // HEAD_DIM
N_KV_A = N_HEADS_A // 4
WINDOW = 128
N_PREV_A = WINDOW // CHUNK
N_HEADS_B = (D_MIX // 2) // HEAD_DIM
N_PREV_B = 8
REL_CLIP = 128
N_REL = 2 * REL_CLIP + 1
DA_Q = N_HEADS_A * HEAD_DIM
DA_KV = N_KV_A * HEAD_DIM
DB = N_HEADS_B * HEAD_DIM
D_IN = DA_Q + 2 * DA_KV + 3 * DB
N_GROUPS = 4
EXPERTS_PER_GROUP = 8
N_EXPERTS = N_GROUPS * EXPERTS_PER_GROUP
TOP_K_INNER = 2
D_EXPERT = D_MODEL // 4
EPS = 1e-6
NEG_INF = -1e30

kernel_name = "hybrid_chunk_swa_relbias_hiermoe"


def rms_norm(x, g):
    xf = x.astype(jnp.float32)
    y = xf * lax.rsqrt(jnp.mean(xf * xf, axis=-1, keepdims=True) + EPS)
    return (y * g.astype(jnp.float32)).astype(x.dtype)


def modulate(h, shift, scale):
    return h * (1 + scale[:, None, :]) + shift[:, None, :]


def alibi_slopes(n_heads):
    return jnp.exp2(-8.0 * jnp.arange(1, n_heads + 1, dtype=jnp.float32) / n_heads)


def chunk_band_attention(q, k, v, n_prev, bias_fn, sinks):
    b, s, hq, dh = q.shape
    hkv = k.shape[2]
    rep = hq // hkv
    n_chunks = s // CHUNK
    band = (n_prev + 1) * CHUNK
    pad = n_prev * CHUNK
    kp = jnp.pad(k, ((0, 0), (pad, 0), (0, 0), (0, 0)))
    vp = jnp.pad(v, ((0, 0), (pad, 0), (0, 0), (0, 0)))
    qc = q.reshape(b, n_chunks, CHUNK, hkv, rep, dh).transpose(1, 0, 2, 3, 4, 5)
    scale = dh ** -0.5

    def one_chunk(args):
        q_i, ci = args
        kb = lax.dynamic_slice_in_dim(kp, ci * CHUNK, band, axis=1)
        vb = lax.dynamic_slice_in_dim(vp, ci * CHUNK, band, axis=1)
        logits = jnp.einsum('bqgrd,bkgd->bgrqk', q_i, kb).astype(jnp.float32) * scale
        q_pos = ci * CHUNK + jnp.arange(CHUNK)
        k_pos = (ci - n_prev) * CHUNK + jnp.arange(band)
        rel = q_pos[:, None] - k_pos[None, :]
        logits = logits + bias_fn(rel).reshape(hkv, rep, CHUNK, band)[None]
        logits = jnp.where((k_pos >= 0)[None, None, None, None, :], logits, NEG_INF)
        m = jnp.max(logits, axis=-1, keepdims=True)
        if sinks is not None:
            sink = sinks.astype(jnp.float32).reshape(1, hkv, rep, 1, 1)
            m = jnp.maximum(m, sink)
        p = jnp.exp(logits - m)
        denom = jnp.sum(p, axis=-1, keepdims=True)
        if sinks is not None:
            denom = denom + jnp.exp(sink - m)
        probs = (p / denom).astype(v.dtype)
        return jnp.einsum('bgrqk,bkgd->bqgrd', probs, vb)

    out = lax.map(one_chunk, (qc, jnp.arange(n_chunks)))
    return out.transpose(1, 0, 2, 3, 4, 5).reshape(b, s, hq * dh)


def hier_moe(h, w_rg, b_rg, w_re, b_re, w_gate, w_up, w_down):
    t = h.reshape(-1, D_MODEL)
    n_tok = t.shape[0]
    g_logits = (t @ w_rg + b_rg).astype(jnp.float32)
    g_prob = jax.nn.softmax(g_logits, axis=-1)
    g_idx = jnp.argmax(g_logits, axis=-1)
    p_g = jnp.take_along_axis(g_prob, g_idx[:, None], axis=-1)
    e_logits = (t @ w_re + b_re).astype(jnp.float32).reshape(n_tok, N_GROUPS, EXPERTS_PER_GROUP)
    sel = jnp.take_along_axis(e_logits, g_idx[:, None, None], axis=1)[:, 0]
    top_v, top_i = lax.top_k(sel, TOP_K_INNER)
    w_sel = jax.nn.softmax(top_v, axis=-1) * p_g
    ids = g_idx[:, None] * EXPERTS_PER_GROUP + top_i
    combine = jnp.sum(jax.nn.one_hot(ids, N_EXPERTS, dtype=jnp.float32) * w_sel[..., None], axis=1)
    y = jnp.zeros((n_tok, D_MODEL), jnp.float32)
    for e in range(N_EXPERTS):
        a = jax.nn.silu(t @ w_gate[e]) * (t @ w_up[e])
        y = y + combine[:, e:e + 1] * (a @ w_down[e]).astype(jnp.float32)
    return y.astype(h.dtype).reshape(h.shape)


def setup_inputs(seed: int = 0) -> dict:
    key = jax.random.key(seed)
    ks = jax.random.split(key, 24)
    f32 = jnp.float32
    L = DEPTH

    def nrm(k, shape, s):
        return jax.random.normal(k, shape, f32) * s

    return {
        "x": nrm(ks[0], (BATCH, SEQ, D_MODEL), 1.0),
        "c": nrm(ks[1], (BATCH, D_MODEL), 1.0),
        "w_ada": nrm(ks[2], (L, D_MODEL, 6 * D_MODEL), 0.5 * D_MODEL ** -0.5),
        "b_ada": nrm(ks[3], (L, 6 * D_MODEL), 0.02),
        "g_mix": 1.0 + nrm(ks[4], (L, D_MODEL), 0.02),
        "w_in": nrm(ks[5], (L, D_MODEL, D_IN), D_MODEL ** -0.5),
        "sinks_a": nrm(ks[6], (L, N_HEADS_A), 0.5),
        "rel_bias_b": nrm(ks[7], (L, N_HEADS_B, N_REL), 0.2),
        "g_out_a": 1.0 + nrm(ks[8], (L, DA_Q), 0.02),
        "g_out_b": 1.0 + nrm(ks[9], (L, DB), 0.02),
        "w_out": nrm(ks[10], (L, D_MIX, D_MODEL), D_MIX ** -0.5),
        "g_ffn": 1.0 + nrm(ks[11], (L, D_MODEL), 0.02),
        "w_router_group": nrm(ks[12], (L, D_MODEL, N_GROUPS), D_MODEL ** -0.5),
        "b_router_group": nrm(ks[13], (L, N_GROUPS), 0.01),
        "w_router_expert": nrm(ks[14], (L, D_MODEL, N_EXPERTS), D_MODEL ** -0.5),
        "b_router_expert": nrm(ks[15], (L, N_EXPERTS), 0.01),
        "w_gate": nrm(ks[16], (L, N_EXPERTS, D_MODEL, D_EXPERT), D_MODEL ** -0.5),
        "w_up": nrm(ks[17], (L, N_EXPERTS, D_MODEL, D_EXPERT), D_MODEL ** -0.5),
        "w_down": nrm(ks[18], (L, N_EXPERTS, D_EXPERT, D_MODEL), D_EXPERT ** -0.5),
        "w_ada_final": nrm(ks[19], (D_MODEL, 2 * D_MODEL), 0.5 * D_MODEL ** -0.5),
        "b_ada_final": nrm(ks[20], (2 * D_MODEL,), 0.02),
        "g_final": 1.0 + nrm(ks[21], (D_MODEL,), 0.02),
    }


def reference(x, c, w_ada, b_ada, g_mix, w_in, sinks_a, rel_bias_b, g_out_a, g_out_b, w_out,
              g_ffn, w_router_group, b_router_group, w_router_expert, b_router_expert,
              w_gate, w_up, w_down, w_ada_final, b_ada_final, g_final):
    b, s, _ = x.shape
    c_act = jax.nn.silu(c)
    slopes = alibi_slopes(N_HEADS_A)
    for l in range(DEPTH):
        mod = c_act @ w_ada[l] + b_ada[l]
        sh1, sc1, gt1, sh2, sc2, gt2 = jnp.split(mod, 6, axis=-1)

        h = modulate(rms_norm(x, g_mix[l]), sh1, sc1)
        proj = h @ w_in[l]
        qa, ka, va, qb, kb, vb = jnp.split(
            proj, np.cumsum([DA_Q, DA_KV, DA_KV, DB, DB]).tolist(), axis=-1)
        qa = qa.reshape(b, s, N_HEADS_A, HEAD_DIM)
        ka = ka.reshape(b, s, N_KV_A, HEAD_DIM)
        va = va.reshape(b, s, N_KV_A, HEAD_DIM)
        qb = qb.reshape(b, s, N_HEADS_B, HEAD_DIM)
        kb = kb.reshape(b, s, N_HEADS_B, HEAD_DIM)
        vb = vb.reshape(b, s, N_HEADS_B, HEAD_DIM)

        def alibi_bias(rel):
            return -slopes[:, None, None] * jnp.abs(rel).astype(jnp.float32)[None]

        rb = rel_bias_b[l]

        def rel_bias(rel):
            idx = jnp.clip(rel, -REL_CLIP, REL_CLIP) + REL_CLIP
            return rb[:, idx].astype(jnp.float32)

        o_a = chunk_band_attention(qa, ka, va, N_PREV_A, alibi_bias, sinks_a[l])
        o_b = chunk_band_attention(qb, kb, vb, N_PREV_B, rel_bias, None)
        o = jnp.concatenate([rms_norm(o_a, g_out_a[l]), rms_norm(o_b, g_out_b[l])], axis=-1)
        x = x + gt1[:, None, :] * (o @ w_out[l])

        h = modulate(rms_norm(x, g_ffn[l]), sh2, sc2)
        y = hier_moe(h, w_router_group[l], b_router_group[l], w_router_expert[l],
                     b_router_expert[l], w_gate[l], w_up[l], w_down[l])
        x = x + gt2[:, None, :] * y

    modf = c_act @ w_ada_final + b_ada_final
    shf, scf = jnp.split(modf, 2, axis=-1)
    return modulate(rms_norm(x, g_final), shf, scf)
```

```python
import functools

import jax
import jax.numpy as jnp
from jax import lax
from jax.experimental import pallas as pl
from jax.experimental.pallas import tpu as pltpu

D_MODEL = 2048
CHUNK = 64
HEAD_DIM = 64
N_HEADS_A = 16
N_KV_A = 4
N_PREV_A = 2
N_HEADS_B = 16
N_PREV_B = 8
REL_CLIP = 128
DA_Q = N_HEADS_A * HEAD_DIM
DA_KV = N_KV_A * HEAD_DIM
DB = N_HEADS_B * HEAD_DIM
D_IN = DA_Q + 2 * DA_KV + 3 * DB
N_GROUPS = 4
EXPERTS_PER_GROUP = 8
N_EXPERTS = N_GROUPS * EXPERTS_PER_GROUP
D_EXPERT = D_MODEL // 4
EPS = 1e-6
NEG_INF = -1e30

LANES = 128
QBLK = 2 * CHUNK
VMEM_LIMIT = 56 * 1024 * 1024

F32 = jnp.float32
BF16 = jnp.bfloat16


def _rms(x, g):
    return x * lax.rsqrt(jnp.mean(x * x, axis=-1, keepdims=True) + EPS) * g


def _ada_kernel(a_ref, w_ref, b_ref, o_ref):
    n_b = a_ref.shape[0]
    tn = w_ref.shape[1]
    for j in range(tn // LANES):
        cols = slice(j * LANES, (j + 1) * LANES)
        w = w_ref[:, cols]
        for b in range(n_b):
            o_ref[b:b + 1, cols] = jnp.sum(a_ref[b] * w, axis=0, keepdims=True) + b_ref[:, cols]


def _ada(a_rep, w, bias):
    n_b, k, _ = a_rep.shape
    n = w.shape[1]
    tn = 1024
    return pl.pallas_call(
        _ada_kernel,
        out_shape=jax.ShapeDtypeStruct((n_b, n), F32),
        grid=(n // tn,),
        in_specs=[
            pl.BlockSpec((n_b, k, LANES), lambda j: (0, 0, 0)),
            pl.BlockSpec((k, tn), lambda j: (0, j)),
            pl.BlockSpec((1, tn), lambda j: (0, j)),
        ],
        out_specs=pl.BlockSpec((n_b, tn), lambda j: (0, j)),
        compiler_params=pltpu.CompilerParams(
            dimension_semantics=("arbitrary",), vmem_limit_bytes=VMEM_LIMIT),
        name="ada",
    )(a_rep, w, bias.reshape(1, n))


def _proj_kernel(x_ref, mod_ref, g_ref, w_ref, o_ref, *, n_chunk):
    h = _rms(x_ref[0], g_ref[...])
    h = h * (1.0 + mod_ref[0, 1:2, :]) + mod_ref[0, 0:1, :]
    hb = h.astype(BF16)
    for n0 in range(0, o_ref.shape[2], n_chunk):
        cols = slice(n0, n0 + n_chunk)
        o_ref[0, :, cols] = jnp.dot(hb, w_ref[:, cols], preferred_element_type=F32).astype(BF16)


def _proj(x, mod, g_mix, w_in_bf16):
    b, s, d = x.shape
    n = w_in_bf16.shape[1]
    tm = 512
    return pl.pallas_call(
        functools.partial(_proj_kernel, n_chunk=512),
        out_shape=jax.ShapeDtypeStruct((b, s, n), BF16),
        grid=(b, s // tm),
        in_specs=[
            pl.BlockSpec((1, tm, d), lambda bi, i: (bi, i, 0)),
            pl.BlockSpec((1, 6, d), lambda bi, i: (bi, 0, 0)),
            pl.BlockSpec((1, d), lambda bi, i: (0, 0)),
            pl.BlockSpec((d, n), lambda bi, i: (0, 0), pipeline_mode=pl.Buffered(1)),
        ],
        out_specs=pl.BlockSpec((1, tm, n), lambda bi, i: (bi, i, 0)),
        compiler_params=pltpu.CompilerParams(
            dimension_semantics=("arbitrary", "arbitrary"), vmem_limit_bytes=VMEM_LIMIT),
        name="proj",
    )(x, mod, g_mix.reshape(1, d), w_in_bf16)


def _attn_kernel(*refs, ncol, nw, dup, has_sink):
    if has_sink:
        sink_ref, q_ref, k_ref, v_ref, bias_ref, o_ref = refs[:6]
        scratch = refs[6:]
    else:
        q_ref, k_ref, v_ref, bias_ref, o_ref = refs[:5]
        scratch = refs[5:]
        sink_ref = None
    s_len = q_ref.shape[1]
    n_blk = s_len // QBLK
    n_stack = 2 * ncol
    grp = pl.program_id(1)

    if dup:
        kd_ref, vd_ref = scratch
        half = grp % 2
        rows = 512

        def dup_body(c, carry):
            r0 = pl.multiple_of(c * rows, rows)
            lane_half = lax.broadcasted_iota(jnp.int32, (rows, LANES), 1) // HEAD_DIM
            keep = lane_half == half
            for src, dst in ((k_ref, kd_ref), (v_ref, vd_ref)):
                t = src[0, pl.ds(r0, rows), :].astype(F32)
                dst[pl.ds(r0, rows), :] = jnp.where(keep, t, pltpu.roll(t, HEAD_DIM, 1)).astype(BF16)
            return carry

        lax.fori_loop(0, s_len // rows, dup_body, 0)
        k_src, v_src = kd_ref, vd_ref
    else:
        k_src, v_src = k_ref.at[0], v_ref.at[0]

    lane = lax.broadcasted_iota(jnp.int32, (QBLK, LANES), 1)
    low = lane < HEAD_DIM

    if has_sink:
        sink = jnp.concatenate(
            [jnp.full((QBLK, 1), sink_ref[grp * n_stack + h], F32) for h in range(n_stack)], axis=0)

    def attend(j, nvb):
        r0 = j * QBLK if isinstance(j, int) else pl.multiple_of(j * QBLK, QBLK)
        k0 = (j - (nvb - 1)) * QBLK
        if not isinstance(j, int):
            k0 = pl.multiple_of(k0, QBLK)
        qf = q_ref[0, pl.ds(r0, QBLK), :].astype(F32) * (HEAD_DIM ** -0.5)
        parts = []
        for c in range(ncol):
            qc = qf[:, c * LANES:(c + 1) * LANES]
            parts.append(jnp.where(low, qc, 0.0))
            parts.append(jnp.where(low, 0.0, qc))
        lhs = jnp.concatenate(parts, axis=0).astype(BF16)
        kw = k_src[pl.ds(k0, nvb * QBLK), :]
        vw = v_src[pl.ds(k0, nvb * QBLK), :]
        s = lax.dot_general(lhs, kw, (((1,), (1,)), ((), ())), preferred_element_type=F32)
        s = s + bias_ref[0, :, (nw - nvb) * QBLK:]
        m = jnp.max(s, axis=-1, keepdims=True)
        if has_sink:
            m = jnp.maximum(m, sink)
        p = jnp.exp(s - m)
        denom = jnp.sum(p, axis=-1, keepdims=True)
        if has_sink:
            denom = denom + jnp.exp(sink - m)
        o = jnp.dot(p.astype(BF16), vw, preferred_element_type=F32) / denom
        for c in range(ncol):
            o0 = o[(2 * c) * QBLK:(2 * c + 1) * QBLK]
            o1 = o[(2 * c + 1) * QBLK:(2 * c + 2) * QBLK]
            o_ref[0, pl.ds(r0, QBLK), c * LANES:(c + 1) * LANES] = jnp.where(low, o0, o1).astype(BF16)

    for j in range(nw - 1):
        attend(j, j + 1)

    def body(j, carry):
        attend(j, nw)
        return carry

    lax.fori_loop(nw - 1, n_blk, body, 0)


def _attention(proj, bias, sinks, *, n_groups, ncol, nw, q_col0, k_col0, v_col0, kv_share, dup):
    b, s, _ = proj.shape
    qw = ncol * LANES
    has_sink = sinks is not None
    kernel = functools.partial(_attn_kernel, ncol=ncol, nw=nw, dup=dup, has_sink=has_sink)
    n_rows = bias.shape[1]
    in_specs = [
        pl.BlockSpec((1, s, qw), lambda bi, g, *_: (bi, 0, q_col0 // ncol + g)),
        pl.BlockSpec((1, s, LANES), lambda bi, g, *_: (bi, 0, k_col0 + g // kv_share)),
        pl.BlockSpec((1, s, LANES), lambda bi, g, *_: (bi, 0, v_col0 + g // kv_share)),
        pl.BlockSpec((1, n_rows, nw * QBLK), lambda bi, g, *_: (g, 0, 0)),
    ]
    out_spec = pl.BlockSpec((1, s, qw), lambda bi, g, *_: (bi, 0, g))
    scratch = [pltpu.VMEM((s, LANES), BF16), pltpu.VMEM((s, LANES), BF16)] if dup else []
    grid_spec = pltpu.PrefetchScalarGridSpec(
        num_scalar_prefetch=1 if has_sink else 0,
        grid=(b, n_groups),
        in_specs=in_specs,
        out_specs=out_spec,
        scratch_shapes=scratch,
    )
    args = ((sinks,) if has_sink else ()) + (proj, proj, proj, bias)
    return pl.pallas_call(
        kernel,
        out_shape=jax.ShapeDtypeStruct((b, s, n_groups * qw), BF16),
        grid_spec=grid_spec,
        compiler_params=pltpu.CompilerParams(
            dimension_semantics=("arbitrary", "arbitrary"), vmem_limit_bytes=VMEM_LIMIT),
        name="attn_a" if dup else "attn_b",
    )(*args)


def _band_tables(rel_bias_b):
    qi = jnp.arange(QBLK)[:, None]
    r = qi // CHUNK

    def table(nw, n_prev, fn):
        kj = jnp.arange(nw * QBLK)[None, :]
        rel = (nw - 1) * QBLK + qi - kj
        inband = (kj >= r * CHUNK) & (kj < (r + n_prev + 1) * CHUNK)
        return jnp.where(inband[None], fn(rel), NEG_INF)

    slopes = jnp.exp2(-8.0 * jnp.arange(1, N_HEADS_A + 1, dtype=F32) / N_HEADS_A)
    nw_a = (N_PREV_A * CHUNK) // QBLK + 1
    nw_b = (N_PREV_B * CHUNK) // QBLK + 1
    bias_a = table(nw_a, N_PREV_A, lambda rel: -slopes[:, None, None] * jnp.abs(rel).astype(F32)[None])
    bias_b = table(nw_b, N_PREV_B,
                   lambda rel: rel_bias_b[:, jnp.clip(rel, -REL_CLIP, REL_CLIP) + REL_CLIP].astype(F32))
    rep = N_HEADS_A // N_KV_A
    bias_a = bias_a.reshape(N_KV_A, rep * QBLK, nw_a * QBLK)
    bias_b = bias_b.reshape(N_HEADS_B // 2, 2 * QBLK, nw_b * QBLK)
    return bias_a, nw_a, bias_b, nw_b


def _mix_kernel(oa_ref, ob_ref, x_ref, mod_ref, ga_ref, gb_ref, gf_ref, wo_ref, wr_ref, br_ref,
                x1_ref, h2_ref, rt_ref):
    half = oa_ref.shape[2]
    na = _rms(oa_ref[0].astype(F32), ga_ref[...]).astype(BF16)
    nb = _rms(ob_ref[0].astype(F32), gb_ref[...]).astype(BF16)
    acc = jnp.dot(na, wo_ref[:half, :], preferred_element_type=F32)
    acc = acc + jnp.dot(nb, wo_ref[half:, :], preferred_element_type=F32)
    x1 = x_ref[0] + mod_ref[0, 2:3, :] * acc
    x1_ref[0] = x1
    h2 = _rms(x1, gf_ref[...]) * (1.0 + mod_ref[0, 4:5, :]) + mod_ref[0, 3:4, :]
    h2_ref[0] = h2

    tm = h2.shape[0]
    h_hi = h2.astype(BF16)
    h_lo = (h2 - h_hi.astype(F32)).astype(BF16)
    w = wr_ref[...]
    w_hi = w.astype(BF16)
    w_lo = (w - w_hi.astype(F32)).astype(BF16)
    logits = jnp.dot(h_hi, w_hi, preferred_element_type=F32)
    logits = logits + jnp.dot(h_lo, w_hi, preferred_element_type=F32)
    logits = logits + jnp.dot(h_hi, w_lo, preferred_element_type=F32)
    logits = logits + br_ref[...]

    lane = lax.broadcasted_iota(jnp.int32, (tm, LANES), 1)
    lane_f = lane.astype(F32)
    big = float(LANES)
    ninf = -jnp.inf

    def first_max(vals):
        top = jnp.max(vals, axis=-1, keepdims=True)
        idx = jnp.min(jnp.where(vals == top, lane_f, big), axis=-1, keepdims=True)
        return top, idx

    is_g = lane < N_GROUPS
    g_top, g_idx = first_max(jnp.where(is_g, logits, ninf))
    p_g = 1.0 / jnp.sum(jnp.where(is_g, jnp.exp(logits - g_top), 0.0), axis=-1, keepdims=True)
    lo = N_GROUPS + g_idx * EXPERTS_PER_GROUP
    e_vals = jnp.where((lane_f >= lo) & (lane_f < lo + EXPERTS_PER_GROUP), logits, ninf)
    v1, i1 = first_max(e_vals)
    v2, i2 = first_max(jnp.where(lane_f == i1, ninf, e_vals))
    e2 = jnp.exp(v2 - v1)
    w1 = p_g / (1.0 + e2)
    w2 = p_g * e2 / (1.0 + e2)
    rt = jnp.where(lane == 0, i1 - N_GROUPS,
                   jnp.where(lane == 1, i2 - N_GROUPS,
                             jnp.where(lane == 2, w1, jnp.where(lane == 3, w2, 0.0))))
    rt_ref[...] = rt


def _mix(o_a, o_b, x, mod, g_out_a, g_out_b, g_ffn, w_out_bf16, w_router, b_router):
    b, s, d = x.shape
    half = o_a.shape[2]
    tm = 256
    nt = s // tm
    vec = lambda n: pl.BlockSpec((1, n), lambda bi, i: (0, 0))
    return pl.pallas_call(
        _mix_kernel,
        out_shape=(jax.ShapeDtypeStruct((b, s, d), F32),
                   jax.ShapeDtypeStruct((b, s, d), F32),
                   jax.ShapeDtypeStruct((b * s, LANES), F32)),
        grid=(b, nt),
        in_specs=[
            pl.BlockSpec((1, tm, half), lambda bi, i: (bi, i, 0)),
            pl.BlockSpec((1, tm, half), lambda bi, i: (bi, i, 0)),
            pl.BlockSpec((1, tm, d), lambda bi, i: (bi, i, 0)),
            pl.BlockSpec((1, 6, d), lambda bi, i: (bi, 0, 0)),
            vec(half), vec(half), vec(d),
            pl.BlockSpec((d, d), lambda bi, i: (0, 0), pipeline_mode=pl.Buffered(1)),
            pl.BlockSpec((d, LANES), lambda bi, i: (0, 0)),
            vec(LANES),
        ],
        out_specs=(pl.BlockSpec((1, tm, d), lambda bi, i: (bi, i, 0)),
                   pl.BlockSpec((1, tm, d), lambda bi, i: (bi, i, 0)),
                   pl.BlockSpec((tm, LANES), lambda bi, i: (bi * nt + i, 0))),
        compiler_params=pltpu.CompilerParams(
            dimension_semantics=("arbitrary", "arbitrary"), vmem_limit_bytes=VMEM_LIMIT),
        name="mix_out",
    )(o_a, o_b, x, mod, g_out_a.reshape(1, half), g_out_b.reshape(1, half), g_ffn.reshape(1, d),
      w_out_bf16, w_router, b_router.reshape(1, LANES))


MOE_TM = 256


def _moe_kernel(te_ref, nv_ref, nt_ref, src_ref, dst_ref, h_hbm, wg_ref, wu_ref, wd_ref, y_hbm,
                xbuf, ybuf, wgb, wub, wdb, gsem, ssem):
    i = pl.program_id(0)
    n_tiles = nt_ref[0]
    tm = xbuf.shape[1]

    def gather(tile, slot):
        def row(r, carry):
            tok = src_ref[tile * tm + r]
            pltpu.make_async_copy(h_hbm.at[pl.ds(tok, 1)], xbuf.at[slot, pl.ds(r, 1)],
                                  gsem.at[slot]).start()
            return carry
        lax.fori_loop(0, tm, row, 0, unroll=8)

    def scatter(tile, slot):
        def row(r, carry):
            dst = dst_ref[tile * tm + r]
            pltpu.make_async_copy(ybuf.at[slot, pl.ds(r, 1)], y_hbm.at[pl.ds(dst, 1)],
                                  ssem.at[slot]).start()
            return carry
        lax.fori_loop(0, nv_ref[tile], row, 0)

    def wait_gather(slot):
        pltpu.make_async_copy(h_hbm.at[pl.ds(0, tm)], xbuf.at[slot], gsem.at[slot]).wait()

    def wait_scatter(tile, slot):
        n = nv_ref[tile]
        n8 = pl.multiple_of((n // 8) * 8, 8)

        @pl.when(n8 > 0)
        def _():
            pltpu.make_async_copy(ybuf.at[slot, pl.ds(0, n8)], y_hbm.at[pl.ds(0, n8)],
                                  ssem.at[slot]).wait()

        def one(r, carry):
            pltpu.make_async_copy(ybuf.at[slot, pl.ds(0, 1)], y_hbm.at[pl.ds(0, 1)],
                                  ssem.at[slot]).wait()
            return carry
        lax.fori_loop(0, n - n8, one, 0)

    @pl.when(i < n_tiles)
    def _():
        slot = i % 2

        @pl.when(i == 0)
        def _():
            gather(0, 0)

        wait_gather(slot)

        @pl.when(i + 1 < n_tiles)
        def _():
            gather(i + 1, 1 - slot)

        @pl.when(jnp.logical_or(i == 0, te_ref[i] != te_ref[jnp.maximum(i - 1, 0)]))
        def _():
            wgb[...] = wg_ref[0].astype(BF16)
            wub[...] = wu_ref[0].astype(BF16)
            wdb[...] = wd_ref[0].astype(BF16)

        xb = xbuf[slot].astype(BF16)
        g = jnp.dot(xb, wgb[...], preferred_element_type=F32)
        u = jnp.dot(xb, wub[...], preferred_element_type=F32)
        a = (g * jax.nn.sigmoid(g) * u).astype(BF16)
        y = jnp.dot(a, wdb[...], preferred_element_type=F32)

        @pl.when(i >= 2)
        def _():
            wait_scatter(i - 2, slot)

        ybuf[slot] = y
        scatter(i, slot)

        @pl.when(i == n_tiles - 1)
        def _():
            @pl.when(i >= 1)
            def _():
                wait_scatter(i - 1, 1 - slot)
            wait_scatter(i, slot)


def _moe(h2, w_gate, w_up, w_down, tile_expert, tile_rows, n_tiles, src_tok, dst_row):
    t, d = h2.shape
    n_pad = src_tok.shape[0]
    tm = MOE_TM
    max_tiles = n_pad // tm
    de = w_gate.shape[2]
    grid_spec = pltpu.PrefetchScalarGridSpec(
        num_scalar_prefetch=5,
        grid=(max_tiles,),
        in_specs=[
            pl.BlockSpec(memory_space=pl.ANY),
            pl.BlockSpec((1, d, de), lambda i, te, *_: (te[i], 0, 0)),
            pl.BlockSpec((1, d, de), lambda i, te, *_: (te[i], 0, 0)),
            pl.BlockSpec((1, de, d), lambda i, te, *_: (te[i], 0, 0)),
        ],
        out_specs=pl.BlockSpec(memory_space=pl.ANY),
        scratch_shapes=[
            pltpu.VMEM((2, tm, d), F32),
            pltpu.VMEM((2, tm, d), F32),
            pltpu.VMEM((d, de), BF16),
            pltpu.VMEM((d, de), BF16),
            pltpu.VMEM((de, d), BF16),
            pltpu.SemaphoreType.DMA((2,)),
            pltpu.SemaphoreType.DMA((2,)),
        ],
    )
    return pl.pallas_call(
        _moe_kernel,
        out_shape=jax.ShapeDtypeStruct((2 * t, d), F32),
        grid_spec=grid_spec,
        compiler_params=pltpu.CompilerParams(
            dimension_semantics=("arbitrary",), vmem_limit_bytes=VMEM_LIMIT),
        name="moe",
    )(tile_expert, tile_rows, n_tiles, src_tok, dst_row, h2, w_gate, w_up, w_down)


def _route_plan(rt, n_tok):
    tm = MOE_TM
    n_asg = 2 * n_tok
    n_pad = n_asg + N_EXPERTS * tm
    max_tiles = n_pad // tm
    e_flat = rt[:, :2].astype(jnp.int32).reshape(-1)
    onehot = (e_flat[:, None] == jnp.arange(N_EXPERTS)[None, :]).astype(jnp.int32)
    csum = jnp.cumsum(onehot, axis=0)
    counts = csum[-1]
    rank = jnp.sum(csum * onehot, axis=1) - 1
    tiles_e = (counts + tm - 1) // tm
    tile_end = jnp.cumsum(tiles_e)
    base = (tile_end - tiles_e) * tm
    slot = jnp.sum(onehot * base[None, :], axis=1) + rank
    n_tiles = tile_end[-1]
    tile_id = jnp.arange(max_tiles)
    te = jnp.sum((tile_id[:, None] >= tile_end[None, :]).astype(jnp.int32), axis=1)
    te_last = jnp.sum(((n_tiles - 1) >= tile_end).astype(jnp.int32))
    tile_expert = jnp.where(tile_id < n_tiles, te, te_last).astype(jnp.int32)
    sel = (tile_expert[:, None] == jnp.arange(N_EXPERTS)[None, :]).astype(jnp.int32)
    first_tile = jnp.sum(sel * (tile_end - tiles_e)[None, :], axis=1)
    left = jnp.sum(sel * counts[None, :], axis=1) - (tile_id - first_tile) * tm
    tile_rows = jnp.where(tile_id < n_tiles, jnp.clip(left, 0, tm), 0).astype(jnp.int32)
    n = jnp.arange(n_asg)
    tok = n // 2
    src_tok = jnp.zeros((n_pad,), jnp.int32).at[slot].set(tok.astype(jnp.int32), unique_indices=True)
    dst_row = jnp.zeros((n_pad,), jnp.int32).at[slot].set(
        ((n % 2) * n_tok + tok).astype(jnp.int32), unique_indices=True)
    return tile_expert, tile_rows, n_tiles.reshape(1).astype(jnp.int32), src_tok, dst_row


def _final_kernel(x1_ref, y0_ref, y1_ref, rt_ref, mod_ref, modf_ref, g_ref, o_ref):
    rt = rt_ref[...]
    y = rt[:, 2:3] * y0_ref[...] + rt[:, 3:4] * y1_ref[...]
    x2 = x1_ref[0] + mod_ref[0, 5:6, :] * y
    o_ref[0] = _rms(x2, g_ref[...]) * (1.0 + modf_ref[0, 1:2, :]) + modf_ref[0, 0:1, :]


def _final(x1, y, rt, mod, modf, g_final):
    b, s, d = x1.shape
    tm = 256
    nt = s // tm
    n_tok_tiles = (b * s) // tm
    return pl.pallas_call(
        _final_kernel,
        out_shape=jax.ShapeDtypeStruct((b, s, d), F32),
        grid=(b, nt),
        in_specs=[
            pl.BlockSpec((1, tm, d), lambda bi, i: (bi, i, 0)),
            pl.BlockSpec((tm, d), lambda bi, i: (bi * nt + i, 0)),
            pl.BlockSpec((tm, d), lambda bi, i: (n_tok_tiles + bi * nt + i, 0)),
            pl.BlockSpec((tm, LANES), lambda bi, i: (bi * nt + i, 0)),
            pl.BlockSpec((1, 6, d), lambda bi, i: (bi, 0, 0)),
            pl.BlockSpec((1, 2, d), lambda bi, i: (bi, 0, 0)),
            pl.BlockSpec((1, d), lambda bi, i: (0, 0)),
        ],
        out_specs=pl.BlockSpec((1, tm, d), lambda bi, i: (bi, i, 0)),
        compiler_params=pltpu.CompilerParams(
            dimension_semantics=("arbitrary", "arbitrary"), vmem_limit_bytes=VMEM_LIMIT),
        name="final",
    )(x1, y, y, rt, mod, modf, g_final.reshape(1, d))


def kernel(x, c, w_ada, b_ada, g_mix, w_in, sinks_a, rel_bias_b, g_out_a, g_out_b, w_out, g_ffn,
           w_router_group, b_router_group, w_router_expert, b_router_expert, w_gate, w_up, w_down,
           w_ada_final, b_ada_final, g_final):
    b, s, d = x.shape
    assert w_ada.shape[0] == 1, "one layer"
    n_tok = b * s

    c_act = jax.nn.silu(c)
    a_rep = jnp.broadcast_to(c_act[:, :, None], (b, d, LANES))
    mod = _ada(a_rep, w_ada[0], b_ada[0]).reshape(b, 6, d)
    modf = _ada(a_rep, w_ada_final, b_ada_final).reshape(b, 2, d)

    proj = _proj(x, mod, g_mix[0], w_in[0].astype(BF16))

    bias_a, nw_a, bias_b, nw_b = _band_tables(rel_bias_b[0])
    kv_a0 = DA_Q // LANES
    o_a = _attention(proj, bias_a, sinks_a[0].astype(F32), n_groups=N_KV_A, ncol=2, nw=nw_a,
                     q_col0=0, k_col0=kv_a0, v_col0=kv_a0 + DA_KV // LANES, kv_share=2, dup=True)
    qb0 = (DA_Q + 2 * DA_KV) // LANES
    o_b = _attention(proj, bias_b, None, n_groups=N_HEADS_B // 2, ncol=1, nw=nw_b,
                     q_col0=qb0, k_col0=qb0 + DB // LANES, v_col0=qb0 + 2 * DB // LANES,
                     kv_share=1, dup=False)

    n_r = N_GROUPS + N_EXPERTS
    w_router = jnp.zeros((d, LANES), F32).at[:, :n_r].set(
        jnp.concatenate([w_router_group[0], w_router_expert[0]], axis=1))
    b_router = jnp.zeros((LANES,), F32).at[:n_r].set(
        jnp.concatenate([b_router_group[0], b_router_expert[0]]))
    x1, h2, rt = _mix(o_a, o_b, x, mod, g_out_a[0], g_out_b[0], g_ffn[0], w_out[0].astype(BF16),
                      w_router, b_router)

    tile_expert, tile_rows, n_tiles, src_tok, dst_row = _route_plan(rt, n_tok)
    y = _moe(h2.reshape(n_tok, d), w_gate[0], w_up[0], w_down[0], tile_expert, tile_rows, n_tiles,
             src_tok, dst_row)

    return _final(x1, y, rt, mod, modf, g_final)
```

```python
import functools

import jax
import jax.numpy as jnp
from jax import lax
from jax.experimental import pallas as pl
from jax.experimental.pallas import tpu as pltpu

D_MODEL = 2048
CHUNK = 64
HEAD_DIM = 64
N_HEADS_A = 16
N_KV_A = 4
N_PREV_A = 2
N_HEADS_B = 16
N_PREV_B = 8
REL_CLIP = 128
DA_Q = N_HEADS_A * HEAD_DIM
DA_KV = N_KV_A * HEAD_DIM
DB = N_HEADS_B * HEAD_DIM
D_IN = DA_Q + 2 * DA_KV + 3 * DB
N_GROUPS = 4
EXPERTS_PER_GROUP = 8
N_EXPERTS = N_GROUPS * EXPERTS_PER_GROUP
D_EXPERT = D_MODEL // 4
EPS = 1e-6
NEG_INF = -1e30

LANES = 128
QBLK = 2 * CHUNK
VMEM_LIMIT = 56 * 1024 * 1024

F32 = jnp.float32
BF16 = jnp.bfloat16


def _rms(x, g):
    return x * lax.rsqrt(jnp.mean(x * x, axis=-1, keepdims=True) + EPS) * g


def _ada_kernel(a_ref, w_ref, b_ref, o_ref):
    n_b = a_ref.shape[0]
    tn = w_ref.shape[1]
    for j in range(tn // LANES):
        cols = slice(j * LANES, (j + 1) * LANES)
        w = w_ref[:, cols]
        for b in range(n_b):
            o_ref[b:b + 1, cols] = jnp.sum(a_ref[b] * w, axis=0, keepdims=True) + b_ref[:, cols]


def _ada(a_rep, w, bias):
    n_b, k, _ = a_rep.shape
    n = w.shape[1]
    tn = 1024
    return pl.pallas_call(
        _ada_kernel,
        out_shape=jax.ShapeDtypeStruct((n_b, n), F32),
        grid=(n // tn,),
        in_specs=[
            pl.BlockSpec((n_b, k, LANES), lambda j: (0, 0, 0)),
            pl.BlockSpec((k, tn), lambda j: (0, j)),
            pl.BlockSpec((1, tn), lambda j: (0, j)),
        ],
        out_specs=pl.BlockSpec((n_b, tn), lambda j: (0, j)),
        compiler_params=pltpu.CompilerParams(
            dimension_semantics=("arbitrary",), vmem_limit_bytes=VMEM_LIMIT),
        name="ada",
    )(a_rep, w, bias.reshape(1, n))


def _proj_kernel(x_ref, mod_ref, g_ref, w_ref, o_ref, *, n_chunk):
    h = _rms(x_ref[0], g_ref[...])
    h = h * (1.0 + mod_ref[0, 1:2, :]) + mod_ref[0, 0:1, :]
    hb = h.astype(BF16)
    for n0 in range(0, o_ref.shape[2], n_chunk):
        cols = slice(n0, n0 + n_chunk)
        o_ref[0, :, cols] = jnp.dot(hb, w_ref[:, cols], preferred_element_type=F32).astype(BF16)


def _proj(x, mod, g_mix, w_in_bf16):
    b, s, d = x.shape
    n = w_in_bf16.shape[1]
    tm = 512
    return pl.pallas_call(
        functools.partial(_proj_kernel, n_chunk=512),
        out_shape=jax.ShapeDtypeStruct((b, s, n), BF16),
        grid=(b, s // tm),
        in_specs=[
            pl.BlockSpec((1, tm, d), lambda bi, i: (bi, i, 0)),
            pl.BlockSpec((1, 6, d), lambda bi, i: (bi, 0, 0)),
            pl.BlockSpec((1, d), lambda bi, i: (0, 0)),
            pl.BlockSpec((d, n), lambda bi, i: (0, 0), pipeline_mode=pl.Buffered(1)),
        ],
        out_specs=pl.BlockSpec((1, tm, n), lambda bi, i: (bi, i, 0)),
        compiler_params=pltpu.CompilerParams(
            dimension_semantics=("arbitrary", "arbitrary"), vmem_limit_bytes=VMEM_LIMIT),
        name="proj",
    )(x, mod, g_mix.reshape(1, d), w_in_bf16)


def _attn_kernel(*refs, ncol, nw, dup, has_sink):
    if has_sink:
        sink_ref, q_ref, k_ref, v_ref, bias_ref, o_ref = refs[:6]
        scratch = refs[6:]
    else:
        q_ref, k_ref, v_ref, bias_ref, o_ref = refs[:5]
        scratch = refs[5:]
        sink_ref = None
    s_len = q_ref.shape[1]
    n_blk = s_len // QBLK
    n_stack = 2 * ncol
    grp = pl.program_id(1)

    if dup:
        kd_ref, vd_ref = scratch
        half = grp % 2
        rows = 512

        def dup_body(c, carry):
            r0 = pl.multiple_of(c * rows, rows)
            lane_half = lax.broadcasted_iota(jnp.int32, (rows, LANES), 1) // HEAD_DIM
            keep = lane_half == half
            for src, dst in ((k_ref, kd_ref), (v_ref, vd_ref)):
                t = src[0, pl.ds(r0, rows), :].astype(F32)
                dst[pl.ds(r0, rows), :] = jnp.where(keep, t, pltpu.roll(t, HEAD_DIM, 1)).astype(BF16)
            return carry

        lax.fori_loop(0, s_len // rows, dup_body, 0)
        k_src, v_src = kd_ref, vd_ref
    else:
        k_src, v_src = k_ref.at[0], v_ref.at[0]

    lane = lax.broadcasted_iota(jnp.int32, (QBLK, LANES), 1)
    low = lane < HEAD_DIM

    if has_sink:
        sink = jnp.max(jnp.concatenate(
            [jnp.full((QBLK, LANES), sink_ref[grp * n_stack + h], F32) for h in range(n_stack)], axis=0),
            axis=-1, keepdims=True)

    def attend(j, nvb):
        r0 = j * QBLK if isinstance(j, int) else pl.multiple_of(j * QBLK, QBLK)
        k0 = (j - (nvb - 1)) * QBLK
        if not isinstance(j, int):
            k0 = pl.multiple_of(k0, QBLK)
        qf = q_ref[0, pl.ds(r0, QBLK), :].astype(F32) * (HEAD_DIM ** -0.5)
        parts = []
        for c in range(ncol):
            qc = qf[:, c * LANES:(c + 1) * LANES]
            parts.append(jnp.where(low, qc, 0.0))
            parts.append(jnp.where(low, 0.0, qc))
        lhs = jnp.concatenate(parts, axis=0).astype(BF16)
        kw = k_src[pl.ds(k0, nvb * QBLK), :]
        vw = v_src[pl.ds(k0, nvb * QBLK), :]
        s = lax.dot_general(lhs, kw, (((1,), (1,)), ((), ())), preferred_element_type=F32)
        s = s + bias_ref[0, :, (nw - nvb) * QBLK:]
        m = jnp.max(s, axis=-1, keepdims=True)
        if has_sink:
            m = jnp.maximum(m, sink)
        p = jnp.exp(s - m)
        denom = jnp.sum(p, axis=-1, keepdims=True)
        if has_sink:
            denom = denom + jnp.exp(sink - m)
        o = jnp.dot(p.astype(BF16), vw, preferred_element_type=F32) / denom
        for c in range(ncol):
            o0 = o[(2 * c) * QBLK:(2 * c + 1) * QBLK]
            o1 = o[(2 * c + 1) * QBLK:(2 * c + 2) * QBLK]
            o_ref[0, pl.ds(r0, QBLK), c * LANES:(c + 1) * LANES] = jnp.where(low, o0, o1).astype(BF16)

    for j in range(nw - 1):
        attend(j, j + 1)

    first = nw - 1
    if (n_blk - first) % 2:
        attend(first, nw)
        first += 1

    def body(t, carry):
        j = first + 2 * t
        attend(j, nw)
        attend(j + 1, nw)
        return carry

    lax.fori_loop(0, (n_blk - first) // 2, body, 0)


def _attention(proj, bias, sinks, *, n_groups, ncol, nw, q_col0, k_col0, v_col0, kv_share, dup):
    b, s, _ = proj.shape
    qw = ncol * LANES
    has_sink = sinks is not None
    kernel = functools.partial(_attn_kernel, ncol=ncol, nw=nw, dup=dup, has_sink=has_sink)
    n_rows = bias.shape[1]
    in_specs = [
        pl.BlockSpec((1, s, qw), lambda bi, g, *_: (bi, 0, q_col0 // ncol + g)),
        pl.BlockSpec((1, s, LANES), lambda bi, g, *_: (bi, 0, k_col0 + g // kv_share)),
        pl.BlockSpec((1, s, LANES), lambda bi, g, *_: (bi, 0, v_col0 + g // kv_share)),
        pl.BlockSpec((1, n_rows, nw * QBLK), lambda bi, g, *_: (g, 0, 0)),
    ]
    out_spec = pl.BlockSpec((1, s, qw), lambda bi, g, *_: (bi, 0, g))
    scratch = [pltpu.VMEM((s, LANES), BF16), pltpu.VMEM((s, LANES), BF16)] if dup else []
    grid_spec = pltpu.PrefetchScalarGridSpec(
        num_scalar_prefetch=1 if has_sink else 0,
        grid=(b, n_groups),
        in_specs=in_specs,
        out_specs=out_spec,
        scratch_shapes=scratch,
    )
    args = ((sinks,) if has_sink else ()) + (proj, proj, proj, bias)
    return pl.pallas_call(
        kernel,
        out_shape=jax.ShapeDtypeStruct((b, s, n_groups * qw), BF16),
        grid_spec=grid_spec,
        compiler_params=pltpu.CompilerParams(
            dimension_semantics=("arbitrary", "arbitrary"), vmem_limit_bytes=VMEM_LIMIT),
        name="attn_a" if dup else "attn_b",
    )(*args)


def _band_tables(rel_bias_b):
    qi = jnp.arange(QBLK)[:, None]
    r = qi // CHUNK

    def table(nw, n_prev, fn):
        kj = jnp.arange(nw * QBLK)[None, :]
        rel = (nw - 1) * QBLK + qi - kj
        inband = (kj >= r * CHUNK) & (kj < (r + n_prev + 1) * CHUNK)
        return jnp.where(inband[None], fn(rel), NEG_INF)

    slopes = jnp.exp2(-8.0 * jnp.arange(1, N_HEADS_A + 1, dtype=F32) / N_HEADS_A)
    nw_a = (N_PREV_A * CHUNK) // QBLK + 1
    nw_b = (N_PREV_B * CHUNK) // QBLK + 1
    bias_a = table(nw_a, N_PREV_A, lambda rel: -slopes[:, None, None] * jnp.abs(rel).astype(F32)[None])
    w_b = nw_b * QBLK
    p = QBLK + w_b - 1
    dist = jnp.clip(jnp.arange(p) - (QBLK - 1), -REL_CLIP, REL_CLIP) + REL_CLIP
    vec = rel_bias_b[:, dist].astype(F32)
    reps = -(-(QBLK * (p + 1)) // p)
    hank = jnp.tile(vec, (1, reps))[:, :QBLK * (p + 1)].reshape(N_HEADS_B, QBLK, p + 1)
    rel_b = jnp.flip(hank[:, :, :w_b], axis=2)
    bias_b = table(nw_b, N_PREV_B, lambda rel: rel_b)
    rep = N_HEADS_A // N_KV_A
    bias_a = bias_a.reshape(N_KV_A, rep * QBLK, nw_a * QBLK)
    bias_b = bias_b.reshape(N_HEADS_B // 2, 2 * QBLK, nw_b * QBLK)
    return bias_a, nw_a, bias_b, nw_b


def _mix_kernel(oa_ref, ob_ref, x_ref, mod_ref, ga_ref, gb_ref, gf_ref, wo_ref, wr_ref, br_ref,
                x1_ref, h2_ref, rt_ref):
    half = oa_ref.shape[2]
    na = _rms(oa_ref[0].astype(F32), ga_ref[...]).astype(BF16)
    nb = _rms(ob_ref[0].astype(F32), gb_ref[...]).astype(BF16)
    acc = jnp.dot(na, wo_ref[:half, :], preferred_element_type=F32)
    acc = acc + jnp.dot(nb, wo_ref[half:, :], preferred_element_type=F32)
    x1 = x_ref[0] + mod_ref[0, 2:3, :] * acc
    x1_ref[0] = x1
    h2 = _rms(x1, gf_ref[...]) * (1.0 + mod_ref[0, 4:5, :]) + mod_ref[0, 3:4, :]
    h2_ref[0] = h2

    tm = h2.shape[0]
    h_hi = h2.astype(BF16)
    h_lo = (h2 - h_hi.astype(F32)).astype(BF16)
    w = wr_ref[...]
    w_hi = w.astype(BF16)
    w_lo = (w - w_hi.astype(F32)).astype(BF16)
    logits = jnp.dot(h_hi, w_hi, preferred_element_type=F32)
    logits = logits + jnp.dot(h_lo, w_hi, preferred_element_type=F32)
    logits = logits + jnp.dot(h_hi, w_lo, preferred_element_type=F32)
    logits = logits + br_ref[...]

    lane = lax.broadcasted_iota(jnp.int32, (tm, LANES), 1)
    lane_f = lane.astype(F32)
    big = float(LANES)
    ninf = -jnp.inf

    def first_max(vals):
        top = jnp.max(vals, axis=-1, keepdims=True)
        idx = jnp.min(jnp.where(vals == top, lane_f, big), axis=-1, keepdims=True)
        return top, idx

    is_g = lane < N_GROUPS
    g_top, g_idx = first_max(jnp.where(is_g, logits, ninf))
    p_g = 1.0 / jnp.sum(jnp.where(is_g, jnp.exp(logits - g_top), 0.0), axis=-1, keepdims=True)
    lo = N_GROUPS + g_idx * EXPERTS_PER_GROUP
    e_vals = jnp.where((lane_f >= lo) & (lane_f < lo + EXPERTS_PER_GROUP), logits, ninf)
    v1, i1 = first_max(e_vals)
    v2, i2 = first_max(jnp.where(lane_f == i1, ninf, e_vals))
    e2 = jnp.exp(v2 - v1)
    w1 = p_g / (1.0 + e2)
    w2 = p_g * e2 / (1.0 + e2)
    rt = jnp.where(lane == 0, i1 - N_GROUPS,
                   jnp.where(lane == 1, i2 - N_GROUPS,
                             jnp.where(lane == 2, w1, jnp.where(lane == 3, w2, 0.0))))
    rt_ref[...] = rt


def _mix(o_a, o_b, x, mod, g_out_a, g_out_b, g_ffn, w_out_bf16, w_router, b_router):
    b, s, d = x.shape
    half = o_a.shape[2]
    tm = 256
    nt = s // tm
    vec = lambda n: pl.BlockSpec((1, n), lambda bi, i: (0, 0))
    return pl.pallas_call(
        _mix_kernel,
        out_shape=(jax.ShapeDtypeStruct((b, s, d), F32),
                   jax.ShapeDtypeStruct((b, s, d), F32),
                   jax.ShapeDtypeStruct((b * s, LANES), F32)),
        grid=(b, nt),
        in_specs=[
            pl.BlockSpec((1, tm, half), lambda bi, i: (bi, i, 0)),
            pl.BlockSpec((1, tm, half), lambda bi, i: (bi, i, 0)),
            pl.BlockSpec((1, tm, d), lambda bi, i: (bi, i, 0)),
            pl.BlockSpec((1, 6, d), lambda bi, i: (bi, 0, 0)),
            vec(half), vec(half), vec(d),
            pl.BlockSpec((d, d), lambda bi, i: (0, 0), pipeline_mode=pl.Buffered(1)),
            pl.BlockSpec((d, LANES), lambda bi, i: (0, 0)),
            vec(LANES),
        ],
        out_specs=(pl.BlockSpec((1, tm, d), lambda bi, i: (bi, i, 0)),
                   pl.BlockSpec((1, tm, d), lambda bi, i: (bi, i, 0)),
                   pl.BlockSpec((tm, LANES), lambda bi, i: (bi * nt + i, 0))),
        compiler_params=pltpu.CompilerParams(
            dimension_semantics=("arbitrary", "arbitrary"), vmem_limit_bytes=VMEM_LIMIT),
        name="mix_out",
    )(o_a, o_b, x, mod, g_out_a.reshape(1, half), g_out_b.reshape(1, half), g_ffn.reshape(1, d),
      w_out_bf16, w_router, b_router.reshape(1, LANES))


MOE_TM = 256


def _moe_kernel(te_ref, nv_ref, nt_ref, src_ref, dst_ref, h_hbm, wg_ref, wu_ref, wd_ref, y_hbm,
                xbuf, ybuf, wgb, wub, wdb, gsem, ssem):
    i = pl.program_id(0)
    n_tiles = nt_ref[0]
    tm = xbuf.shape[1]

    def gather(tile, slot):
        def row(r, carry):
            tok = src_ref[tile * tm + r]
            pltpu.make_async_copy(h_hbm.at[pl.ds(tok, 1)], xbuf.at[slot, pl.ds(r, 1)],
                                  gsem.at[slot]).start()
            return carry
        lax.fori_loop(0, tm, row, 0, unroll=8)

    def scatter(tile, slot):
        def row(r, carry):
            dst = dst_ref[tile * tm + r]
            pltpu.make_async_copy(ybuf.at[slot, pl.ds(r, 1)], y_hbm.at[pl.ds(dst, 1)],
                                  ssem.at[slot]).start()
            return carry
        lax.fori_loop(0, nv_ref[tile], row, 0)

    def wait_gather(slot):
        pltpu.make_async_copy(h_hbm.at[pl.ds(0, tm)], xbuf.at[slot], gsem.at[slot]).wait()

    def wait_scatter(tile, slot):
        n = nv_ref[tile]
        n8 = pl.multiple_of((n // 8) * 8, 8)

        @pl.when(n8 > 0)
        def _():
            pltpu.make_async_copy(ybuf.at[slot, pl.ds(0, n8)], y_hbm.at[pl.ds(0, n8)],
                                  ssem.at[slot]).wait()

        def one(r, carry):
            pltpu.make_async_copy(ybuf.at[slot, pl.ds(0, 1)], y_hbm.at[pl.ds(0, 1)],
                                  ssem.at[slot]).wait()
            return carry
        lax.fori_loop(0, n - n8, one, 0)

    @pl.when(i < n_tiles)
    def _():
        slot = i % 2

        @pl.when(i == 0)
        def _():
            gather(0, 0)

        wait_gather(slot)

        @pl.when(i + 1 < n_tiles)
        def _():
            gather(i + 1, 1 - slot)

        @pl.when(jnp.logical_or(i == 0, te_ref[i] != te_ref[jnp.maximum(i - 1, 0)]))
        def _():
            wgb[...] = wg_ref[0].astype(BF16)
            wub[...] = wu_ref[0].astype(BF16)
            wdb[...] = wd_ref[0].astype(BF16)

        xb = xbuf[slot].astype(BF16)
        g = jnp.dot(xb, wgb[...], preferred_element_type=F32)
        u = jnp.dot(xb, wub[...], preferred_element_type=F32)
        a = (g * jax.nn.sigmoid(g) * u).astype(BF16)
        y = jnp.dot(a, wdb[...], preferred_element_type=F32)

        @pl.when(i >= 2)
        def _():
            wait_scatter(i - 2, slot)

        ybuf[slot] = y
        scatter(i, slot)

        @pl.when(i == n_tiles - 1)
        def _():
            @pl.when(i >= 1)
            def _():
                wait_scatter(i - 1, 1 - slot)
            wait_scatter(i, slot)


def _moe(h2, w_gate, w_up, w_down, tile_expert, tile_rows, n_tiles, src_tok, dst_row):
    t, d = h2.shape
    n_pad = src_tok.shape[0]
    tm = MOE_TM
    max_tiles = n_pad // tm
    de = w_gate.shape[2]
    grid_spec = pltpu.PrefetchScalarGridSpec(
        num_scalar_prefetch=5,
        grid=(max_tiles,),
        in_specs=[
            pl.BlockSpec(memory_space=pl.ANY),
            pl.BlockSpec((1, d, de), lambda i, te, *_: (te[i], 0, 0)),
            pl.BlockSpec((1, d, de), lambda i, te, *_: (te[i], 0, 0)),
            pl.BlockSpec((1, de, d), lambda i, te, *_: (te[i], 0, 0)),
        ],
        out_specs=pl.BlockSpec(memory_space=pl.ANY),
        scratch_shapes=[
            pltpu.VMEM((2, tm, d), F32),
            pltpu.VMEM((2, tm, d), F32),
            pltpu.VMEM((d, de), BF16),
            pltpu.VMEM((d, de), BF16),
            pltpu.VMEM((de, d), BF16),
            pltpu.SemaphoreType.DMA((2,)),
            pltpu.SemaphoreType.DMA((2,)),
        ],
    )
    return pl.pallas_call(
        _moe_kernel,
        out_shape=jax.ShapeDtypeStruct((2 * t, d), F32),
        grid_spec=grid_spec,
        compiler_params=pltpu.CompilerParams(
            dimension_semantics=("arbitrary",), vmem_limit_bytes=VMEM_LIMIT),
        name="moe",
    )(tile_expert, tile_rows, n_tiles, src_tok, dst_row, h2, w_gate, w_up, w_down)


def _route_plan(rt, n_tok):
    tm = MOE_TM
    n_asg = 2 * n_tok
    n_pad = n_asg + N_EXPERTS * tm
    max_tiles = n_pad // tm
    e_flat = rt[:, :2].astype(jnp.int32).reshape(-1)
    onehot = (e_flat[:, None] == jnp.arange(N_EXPERTS)[None, :]).astype(jnp.int32)
    csum = jnp.cumsum(onehot, axis=0)
    counts = csum[-1]
    rank = jnp.sum(csum * onehot, axis=1) - 1
    tiles_e = (counts + tm - 1) // tm
    tile_end = jnp.cumsum(tiles_e)
    base = (tile_end - tiles_e) * tm
    slot = jnp.sum(onehot * base[None, :], axis=1) + rank
    n_tiles = tile_end[-1]
    tile_id = jnp.arange(max_tiles)
    te = jnp.sum((tile_id[:, None] >= tile_end[None, :]).astype(jnp.int32), axis=1)
    te_last = jnp.sum(((n_tiles - 1) >= tile_end).astype(jnp.int32))
    tile_expert = jnp.where(tile_id < n_tiles, te, te_last).astype(jnp.int32)
    sel = (tile_expert[:, None] == jnp.arange(N_EXPERTS)[None, :]).astype(jnp.int32)
    first_tile = jnp.sum(sel * (tile_end - tiles_e)[None, :], axis=1)
    left = jnp.sum(sel * counts[None, :], axis=1) - (tile_id - first_tile) * tm
    tile_rows = jnp.where(tile_id < n_tiles, jnp.clip(left, 0, tm), 0).astype(jnp.int32)
    asg = jnp.full((n_pad,), -1, jnp.int32).at[slot].set(jnp.arange(n_asg, dtype=jnp.int32),
                                                          unique_indices=True)
    tok = jnp.maximum(asg, 0) // 2
    src_tok = tok.astype(jnp.int32)
    dst_row = jnp.where(asg >= 0, (asg % 2) * n_tok + tok, 0).astype(jnp.int32)
    return tile_expert, tile_rows, n_tiles.reshape(1).astype(jnp.int32), src_tok, dst_row


def _final_kernel(x1_ref, y0_ref, y1_ref, rt_ref, mod_ref, modf_ref, g_ref, o_ref):
    rt = rt_ref[...]
    y = rt[:, 2:3] * y0_ref[...] + rt[:, 3:4] * y1_ref[...]
    x2 = x1_ref[0] + mod_ref[0, 5:6, :] * y
    o_ref[0] = _rms(x2, g_ref[...]) * (1.0 + modf_ref[0, 1:2, :]) + modf_ref[0, 0:1, :]


def _final(x1, y, rt, mod, modf, g_final):
    b, s, d = x1.shape
    tm = 256
    nt = s // tm
    n_tok_tiles = (b * s) // tm
    return pl.pallas_call(
        _final_kernel,
        out_shape=jax.ShapeDtypeStruct((b, s, d), F32),
        grid=(b, nt),
        in_specs=[
            pl.BlockSpec((1, tm, d), lambda bi, i: (bi, i, 0)),
            pl.BlockSpec((tm, d), lambda bi, i: (bi * nt + i, 0)),
            pl.BlockSpec((tm, d), lambda bi, i: (n_tok_tiles + bi * nt + i, 0)),
            pl.BlockSpec((tm, LANES), lambda bi, i: (bi * nt + i, 0)),
            pl.BlockSpec((1, 6, d), lambda bi, i: (bi, 0, 0)),
            pl.BlockSpec((1, 2, d), lambda bi, i: (bi, 0, 0)),
            pl.BlockSpec((1, d), lambda bi, i: (0, 0)),
        ],
        out_specs=pl.BlockSpec((1, tm, d), lambda bi, i: (bi, i, 0)),
        compiler_params=pltpu.CompilerParams(
            dimension_semantics=("arbitrary", "arbitrary"), vmem_limit_bytes=VMEM_LIMIT),
        name="final",
    )(x1, y, y, rt, mod, modf, g_final.reshape(1, d))


def kernel(x, c, w_ada, b_ada, g_mix, w_in, sinks_a, rel_bias_b, g_out_a, g_out_b, w_out, g_ffn,
           w_router_group, b_router_group, w_router_expert, b_router_expert, w_gate, w_up, w_down,
           w_ada_final, b_ada_final, g_final):
    b, s, d = x.shape
    assert w_ada.shape[0] == 1, "one layer"
    n_tok = b * s

    c_act = jax.nn.silu(c)
    a_rep = jnp.broadcast_to(c_act[:, :, None], (b, d, LANES))
    mod = _ada(a_rep, w_ada[0], b_ada[0]).reshape(b, 6, d)
    modf = _ada(a_rep, w_ada_final, b_ada_final).reshape(b, 2, d)

    proj = _proj(x, mod, g_mix[0], w_in[0].astype(BF16))

    bias_a, nw_a, bias_b, nw_b = _band_tables(rel_bias_b[0])
    kv_a0 = DA_Q // LANES
    o_a = _attention(proj, bias_a, sinks_a[0].astype(F32), n_groups=N_KV_A, ncol=2, nw=nw_a,
                     q_col0=0, k_col0=kv_a0, v_col0=kv_a0 + DA_KV // LANES, kv_share=2, dup=True)
    qb0 = (DA_Q + 2 * DA_KV) // LANES
    o_b = _attention(proj, bias_b, None, n_groups=N_HEADS_B // 2, ncol=1, nw=nw_b,
                     q_col0=qb0, k_col0=qb0 + DB // LANES, v_col0=qb0 + 2 * DB // LANES,
                     kv_share=1, dup=False)

    n_r = N_GROUPS + N_EXPERTS
    w_router = jnp.zeros((d, LANES), F32).at[:, :n_r].set(
        jnp.concatenate([w_router_group[0], w_router_expert[0]], axis=1))
    b_router = jnp.zeros((LANES,), F32).at[:n_r].set(
        jnp.concatenate([b_router_group[0], b_router_expert[0]]))
    x1, h2, rt = _mix(o_a, o_b, x, mod, g_out_a[0], g_out_b[0], g_ffn[0], w_out[0].astype(BF16),
                      w_router, b_router)

    tile_expert, tile_rows, n_tiles, src_tok, dst_row = _route_plan(rt, n_tok)
    y = _moe(h2.reshape(n_tok, d), w_gate[0], w_up[0], w_down[0], tile_expert, tile_rows, n_tiles,
             src_tok, dst_row)

    return _final(x1, y, rt, mod, modf, g_final)
```

```python
import functools

import jax
import jax.numpy as jnp
import numpy as np
from jax import lax
from jax.experimental import pallas as pl
from jax.experimental.pallas import tpu as pltpu

D_MODEL = 2048
CHUNK = 64
HEAD_DIM = 64
N_HEADS_A = 16
N_KV_A = 4
N_PREV_A = 2
N_HEADS_B = 16
N_PREV_B = 8
REL_CLIP = 128
DA_Q = N_HEADS_A * HEAD_DIM
DA_KV = N_KV_A * HEAD_DIM
DB = N_HEADS_B * HEAD_DIM
D_IN = DA_Q + 2 * DA_KV + 3 * DB
N_GROUPS = 4
EXPERTS_PER_GROUP = 8
N_EXPERTS = N_GROUPS * EXPERTS_PER_GROUP
D_EXPERT = D_MODEL // 4
EPS = 1e-6
NEG_INF = -1e30

LANES = 128
QBLK = 2 * CHUNK
VMEM_LIMIT = 56 * 1024 * 1024

F32 = jnp.float32
BF16 = jnp.bfloat16


def _rms(x, g):
    return x * lax.rsqrt(jnp.mean(x * x, axis=-1, keepdims=True) + EPS) * g


def _ada_kernel(a_ref, w_ref, b_ref, o_ref):
    n_b = a_ref.shape[0]
    tn = w_ref.shape[1]
    for j in range(tn // LANES):
        cols = slice(j * LANES, (j + 1) * LANES)
        w = w_ref[:, cols]
        for b in range(n_b):
            o_ref[b:b + 1, cols] = jnp.sum(a_ref[b] * w, axis=0, keepdims=True) + b_ref[:, cols]


def _ada(a_rep, w, bias):
    n_b, k, _ = a_rep.shape
    n = w.shape[1]
    tn = 1024
    return pl.pallas_call(
        _ada_kernel,
        out_shape=jax.ShapeDtypeStruct((n_b, n), F32),
        grid=(n // tn,),
        in_specs=[
            pl.BlockSpec((n_b, k, LANES), lambda j: (0, 0, 0)),
            pl.BlockSpec((k, tn), lambda j: (0, j)),
            pl.BlockSpec((1, tn), lambda j: (0, j)),
        ],
        out_specs=pl.BlockSpec((n_b, tn), lambda j: (0, j)),
        compiler_params=pltpu.CompilerParams(
            dimension_semantics=("arbitrary",), vmem_limit_bytes=VMEM_LIMIT),
        name="ada",
    )(a_rep, w, bias.reshape(1, n))


def _proj_kernel(x_ref, mod_ref, g_ref, w_ref, o_ref, *, n_chunk):
    h = _rms(x_ref[0], g_ref[...])
    h = h * (1.0 + mod_ref[0, 1:2, :]) + mod_ref[0, 0:1, :]
    hb = h.astype(BF16)
    for n0 in range(0, o_ref.shape[2], n_chunk):
        cols = slice(n0, n0 + n_chunk)
        o_ref[0, :, cols] = jnp.dot(hb, w_ref[:, cols], preferred_element_type=F32).astype(BF16)


def _proj(x, mod, g_mix, w_in_bf16):
    b, s, d = x.shape
    n = w_in_bf16.shape[1]
    tm = 512
    return pl.pallas_call(
        functools.partial(_proj_kernel, n_chunk=512),
        out_shape=jax.ShapeDtypeStruct((b, s, n), BF16),
        grid=(b, s // tm),
        in_specs=[
            pl.BlockSpec((1, tm, d), lambda bi, i: (bi, i, 0)),
            pl.BlockSpec((1, 6, d), lambda bi, i: (bi, 0, 0)),
            pl.BlockSpec((1, d), lambda bi, i: (0, 0)),
            pl.BlockSpec((d, n), lambda bi, i: (0, 0), pipeline_mode=pl.Buffered(1)),
        ],
        out_specs=pl.BlockSpec((1, tm, n), lambda bi, i: (bi, i, 0)),
        compiler_params=pltpu.CompilerParams(
            dimension_semantics=("arbitrary", "arbitrary"), vmem_limit_bytes=VMEM_LIMIT),
        name="proj",
    )(x, mod, g_mix.reshape(1, d), w_in_bf16)


def _attn_kernel(*refs, ncol, nw, dup, has_sink):
    if has_sink:
        sink_ref, q_ref, k_ref, v_ref, bias_ref, o_ref = refs[:6]
        scratch = refs[6:]
    else:
        q_ref, k_ref, v_ref, bias_ref, o_ref = refs[:5]
        scratch = refs[5:]
        sink_ref = None
    s_len = q_ref.shape[1]
    n_blk = s_len // QBLK
    n_stack = 2 * ncol
    grp = pl.program_id(1)

    if dup:
        kd_ref, vd_ref = scratch
        half = grp % 2
        rows = 512

        def dup_body(c, carry):
            r0 = pl.multiple_of(c * rows, rows)
            lane_half = lax.broadcasted_iota(jnp.int32, (rows, LANES), 1) // HEAD_DIM
            keep = lane_half == half
            for src, dst in ((k_ref, kd_ref), (v_ref, vd_ref)):
                t = src[0, pl.ds(r0, rows), :].astype(F32)
                dst[pl.ds(r0, rows), :] = jnp.where(keep, t, pltpu.roll(t, HEAD_DIM, 1)).astype(BF16)
            return carry

        lax.fori_loop(0, s_len // rows, dup_body, 0)
        k_src, v_src = kd_ref, vd_ref
    else:
        k_src, v_src = k_ref.at[0], v_ref.at[0]

    lane = lax.broadcasted_iota(jnp.int32, (QBLK, LANES), 1)
    low = lane < HEAD_DIM

    if has_sink:
        sink = jnp.max(jnp.concatenate(
            [jnp.full((QBLK, LANES), sink_ref[grp * n_stack + h], F32) for h in range(n_stack)], axis=0),
            axis=-1, keepdims=True)

    def attend(j, nvb):
        r0 = j * QBLK if isinstance(j, int) else pl.multiple_of(j * QBLK, QBLK)
        k0 = (j - (nvb - 1)) * QBLK
        if not isinstance(j, int):
            k0 = pl.multiple_of(k0, QBLK)
        qf = q_ref[0, pl.ds(r0, QBLK), :].astype(F32) * (HEAD_DIM ** -0.5)
        parts = []
        for c in range(ncol):
            qc = qf[:, c * LANES:(c + 1) * LANES]
            parts.append(jnp.where(low, qc, 0.0))
            parts.append(jnp.where(low, 0.0, qc))
        lhs = jnp.concatenate(parts, axis=0).astype(BF16)
        kw = k_src[pl.ds(k0, nvb * QBLK), :]
        vw = v_src[pl.ds(k0, nvb * QBLK), :]
        s = lax.dot_general(lhs, kw, (((1,), (1,)), ((), ())), preferred_element_type=F32)
        s = s + bias_ref[0, :, (nw - nvb) * QBLK:]
        m = jnp.max(s, axis=-1, keepdims=True)
        if has_sink:
            m = jnp.maximum(m, sink)
        p = jnp.exp(s - m)
        denom = jnp.sum(p, axis=-1, keepdims=True)
        if has_sink:
            denom = denom + jnp.exp(sink - m)
        o = jnp.dot(p.astype(BF16), vw, preferred_element_type=F32) / denom
        for c in range(ncol):
            o0 = o[(2 * c) * QBLK:(2 * c + 1) * QBLK]
            o1 = o[(2 * c + 1) * QBLK:(2 * c + 2) * QBLK]
            o_ref[0, pl.ds(r0, QBLK), c * LANES:(c + 1) * LANES] = jnp.where(low, o0, o1).astype(BF16)

    for j in range(nw - 1):
        attend(j, j + 1)

    first = nw - 1
    if (n_blk - first) % 2:
        attend(first, nw)
        first += 1

    def body(t, carry):
        j = first + 2 * t
        attend(j, nw)
        attend(j + 1, nw)
        return carry

    lax.fori_loop(0, (n_blk - first) // 2, body, 0)


def _attention(proj, bias, sinks, *, n_groups, ncol, nw, q_col0, k_col0, v_col0, kv_share, dup):
    b, s, _ = proj.shape
    qw = ncol * LANES
    has_sink = sinks is not None
    kernel = functools.partial(_attn_kernel, ncol=ncol, nw=nw, dup=dup, has_sink=has_sink)
    n_rows = bias.shape[1]
    in_specs = [
        pl.BlockSpec((1, s, qw), lambda bi, g, *_: (bi, 0, q_col0 // ncol + g)),
        pl.BlockSpec((1, s, LANES), lambda bi, g, *_: (bi, 0, k_col0 + g // kv_share)),
        pl.BlockSpec((1, s, LANES), lambda bi, g, *_: (bi, 0, v_col0 + g // kv_share)),
        pl.BlockSpec((1, n_rows, nw * QBLK), lambda bi, g, *_: (g, 0, 0)),
    ]
    out_spec = pl.BlockSpec((1, s, qw), lambda bi, g, *_: (bi, 0, g))
    scratch = [pltpu.VMEM((s, LANES), BF16), pltpu.VMEM((s, LANES), BF16)] if dup else []
    grid_spec = pltpu.PrefetchScalarGridSpec(
        num_scalar_prefetch=1 if has_sink else 0,
        grid=(b, n_groups),
        in_specs=in_specs,
        out_specs=out_spec,
        scratch_shapes=scratch,
    )
    args = ((sinks,) if has_sink else ()) + (proj, proj, proj, bias)
    return pl.pallas_call(
        kernel,
        out_shape=jax.ShapeDtypeStruct((b, s, n_groups * qw), BF16),
        grid_spec=grid_spec,
        compiler_params=pltpu.CompilerParams(
            dimension_semantics=("arbitrary", "arbitrary"), vmem_limit_bytes=VMEM_LIMIT),
        name="attn_a" if dup else "attn_b",
    )(*args)


def _band_tables(rel_bias_b):
    qi = jnp.arange(QBLK)[:, None]
    r = qi // CHUNK

    def table(nw, n_prev, fn):
        kj = jnp.arange(nw * QBLK)[None, :]
        rel = (nw - 1) * QBLK + qi - kj
        inband = (kj >= r * CHUNK) & (kj < (r + n_prev + 1) * CHUNK)
        return jnp.where(inband[None], fn(rel), NEG_INF)

    slopes = jnp.exp2(-8.0 * jnp.arange(1, N_HEADS_A + 1, dtype=F32) / N_HEADS_A)
    nw_a = (N_PREV_A * CHUNK) // QBLK + 1
    nw_b = (N_PREV_B * CHUNK) // QBLK + 1
    bias_a = table(nw_a, N_PREV_A, lambda rel: -slopes[:, None, None] * jnp.abs(rel).astype(F32)[None])
    w_b = nw_b * QBLK
    p = QBLK + w_b - 1
    m = (np.arange(p) + QBLK - 1) % p - (QBLK - 1)
    dist = np.clip((nw_b - 1) * QBLK - m, -REL_CLIP, REL_CLIP) + REL_CLIP
    vec = rel_bias_b[:, dist].astype(F32)
    reps = -(-(QBLK * (p - 1)) // p)
    rel_b = jnp.tile(vec, (1, reps))[:, :QBLK * (p - 1)].reshape(N_HEADS_B, QBLK, p - 1)[:, :, :w_b]
    bias_b = table(nw_b, N_PREV_B, lambda rel: rel_b)
    rep = N_HEADS_A // N_KV_A
    bias_a = bias_a.reshape(N_KV_A, rep * QBLK, nw_a * QBLK)
    bias_b = bias_b.reshape(N_HEADS_B // 2, 2 * QBLK, nw_b * QBLK)
    return bias_a, nw_a, bias_b, nw_b


def _mix_kernel(oa_ref, ob_ref, x_ref, mod_ref, ga_ref, gb_ref, gf_ref, wo_ref, wr_ref, br_ref,
                x1_ref, h2_ref, rt_ref):
    half = oa_ref.shape[2]
    na = _rms(oa_ref[0].astype(F32), ga_ref[...]).astype(BF16)
    nb = _rms(ob_ref[0].astype(F32), gb_ref[...]).astype(BF16)
    acc = jnp.dot(na, wo_ref[:half, :], preferred_element_type=F32)
    acc = acc + jnp.dot(nb, wo_ref[half:, :], preferred_element_type=F32)
    x1 = x_ref[0] + mod_ref[0, 2:3, :] * acc
    x1_ref[0] = x1
    h2 = _rms(x1, gf_ref[...]) * (1.0 + mod_ref[0, 4:5, :]) + mod_ref[0, 3:4, :]
    h2_ref[0] = h2

    tm = h2.shape[0]
    h_hi = h2.astype(BF16)
    h_lo = (h2 - h_hi.astype(F32)).astype(BF16)
    w = wr_ref[...]
    w_hi = w.astype(BF16)
    w_lo = (w - w_hi.astype(F32)).astype(BF16)
    logits = jnp.dot(h_hi, w_hi, preferred_element_type=F32)
    logits = logits + jnp.dot(h_lo, w_hi, preferred_element_type=F32)
    logits = logits + jnp.dot(h_hi, w_lo, preferred_element_type=F32)
    logits = logits + br_ref[...]

    lane = lax.broadcasted_iota(jnp.int32, (tm, LANES), 1)
    lane_f = lane.astype(F32)
    big = float(LANES)
    ninf = -jnp.inf

    def first_max(vals):
        top = jnp.max(vals, axis=-1, keepdims=True)
        idx = jnp.min(jnp.where(vals == top, lane_f, big), axis=-1, keepdims=True)
        return top, idx

    is_g = lane < N_GROUPS
    g_top, g_idx = first_max(jnp.where(is_g, logits, ninf))
    p_g = 1.0 / jnp.sum(jnp.where(is_g, jnp.exp(logits - g_top), 0.0), axis=-1, keepdims=True)
    lo = N_GROUPS + g_idx * EXPERTS_PER_GROUP
    e_vals = jnp.where((lane_f >= lo) & (lane_f < lo + EXPERTS_PER_GROUP), logits, ninf)
    v1, i1 = first_max(e_vals)
    v2, i2 = first_max(jnp.where(lane_f == i1, ninf, e_vals))
    e2 = jnp.exp(v2 - v1)
    w1 = p_g / (1.0 + e2)
    w2 = p_g * e2 / (1.0 + e2)
    rt = jnp.where(lane == 0, i1 - N_GROUPS,
                   jnp.where(lane == 1, i2 - N_GROUPS,
                             jnp.where(lane == 2, w1, jnp.where(lane == 3, w2, 0.0))))
    rt_ref[...] = rt


def _mix(o_a, o_b, x, mod, g_out_a, g_out_b, g_ffn, w_out_bf16, w_router, b_router):
    b, s, d = x.shape
    half = o_a.shape[2]
    tm = 256
    nt = s // tm
    vec = lambda n: pl.BlockSpec((1, n), lambda bi, i: (0, 0))
    return pl.pallas_call(
        _mix_kernel,
        out_shape=(jax.ShapeDtypeStruct((b, s, d), F32),
                   jax.ShapeDtypeStruct((b, s, d), F32),
                   jax.ShapeDtypeStruct((b * s, LANES), F32)),
        grid=(b, nt),
        in_specs=[
            pl.BlockSpec((1, tm, half), lambda bi, i: (bi, i, 0)),
            pl.BlockSpec((1, tm, half), lambda bi, i: (bi, i, 0)),
            pl.BlockSpec((1, tm, d), lambda bi, i: (bi, i, 0)),
            pl.BlockSpec((1, 6, d), lambda bi, i: (bi, 0, 0)),
            vec(half), vec(half), vec(d),
            pl.BlockSpec((d, d), lambda bi, i: (0, 0), pipeline_mode=pl.Buffered(1)),
            pl.BlockSpec((d, LANES), lambda bi, i: (0, 0)),
            vec(LANES),
        ],
        out_specs=(pl.BlockSpec((1, tm, d), lambda bi, i: (bi, i, 0)),
                   pl.BlockSpec((1, tm, d), lambda bi, i: (bi, i, 0)),
                   pl.BlockSpec((tm, LANES), lambda bi, i: (bi * nt + i, 0))),
        compiler_params=pltpu.CompilerParams(
            dimension_semantics=("arbitrary", "arbitrary"), vmem_limit_bytes=VMEM_LIMIT),
        name="mix_out",
    )(o_a, o_b, x, mod, g_out_a.reshape(1, half), g_out_b.reshape(1, half), g_ffn.reshape(1, d),
      w_out_bf16, w_router, b_router.reshape(1, LANES))


MOE_TM = 256


def _moe_kernel(te_ref, nt_ref, src_ref, dst_ref, h_hbm, wg_ref, wu_ref, wd_ref, y_hbm,
                xbuf, ybuf, wgb, wub, wdb, gsem, ssem):
    i = pl.program_id(0)
    n_tiles = nt_ref[0]
    tm = xbuf.shape[1]
    spare0 = y_hbm.shape[0] - tm
    slot = i % 2

    def row_in(tile, r, s):
        tok = src_ref[tile * tm + r]
        pltpu.make_async_copy(h_hbm.at[pl.ds(tok, 1)], xbuf.at[s, pl.ds(r, 1)], gsem.at[s]).start()

    def row_out(tile, r, s):
        dst = dst_ref[(tile + 1) * tm + r]
        pltpu.make_async_copy(ybuf.at[s, pl.ds(r, 1)], y_hbm.at[pl.ds(dst, 1)], ssem.at[s]).start()

    def wait_in(s):
        pltpu.make_async_copy(h_hbm.at[pl.ds(0, tm)], xbuf.at[s], gsem.at[s]).wait()

    def wait_out(s):
        pltpu.make_async_copy(ybuf.at[s], y_hbm.at[pl.ds(0, tm)], ssem.at[s]).wait()

    @pl.when(i == 0)
    def _():
        ybuf[1] = jnp.zeros(ybuf.shape[1:], F32)
        fill = pltpu.make_async_copy(ybuf.at[1], y_hbm.at[pl.ds(spare0, tm)], ssem.at[1])
        fill.start()
        fill.wait()

        def row(r, carry):
            row_in(0, r, 0)
            return carry
        lax.fori_loop(0, tm, row, 0, unroll=8)

    def step(s):
        wait_in(s)

        @pl.when(i >= 1)
        def _():
            wait_out(s)

        @pl.when(i < n_tiles)
        def _():
            @pl.when(jnp.logical_or(i == 0, te_ref[i] != te_ref[jnp.maximum(i - 1, 0)]))
            def _():
                wgb[...] = wg_ref[0].astype(BF16)
                wub[...] = wu_ref[0].astype(BF16)
                wdb[...] = wd_ref[0].astype(BF16)

            xb = xbuf[s].astype(BF16)
            nxt = jnp.minimum(i + 1, n_tiles - 1)
            for r in range(tm):
                row_in(nxt, r, 1 - s)
                row_out(i - 1, r, 1 - s)
            g = jnp.dot(xb, wgb[...], preferred_element_type=F32)
            u = jnp.dot(xb, wub[...], preferred_element_type=F32)
            a = (g * jax.nn.sigmoid(g) * u).astype(BF16)
            ybuf[s] = jnp.dot(a, wdb[...], preferred_element_type=F32)

        @pl.when(i == n_tiles)
        def _():
            def row(r, carry):
                row_out(i - 1, r, 1 - s)
                return carry
            lax.fori_loop(0, tm, row, 0, unroll=8)
            wait_out(1 - s)

    for s in range(2):
        pl.when(jnp.logical_and(i <= n_tiles, slot == s))(functools.partial(step, s))


def _moe(h2, w_gate, w_up, w_down, tile_expert, n_tiles, src_tok, dst_row):
    t, d = h2.shape
    n_pad = src_tok.shape[0]
    tm = MOE_TM
    max_tiles = n_pad // tm
    de = w_gate.shape[2]
    grid_spec = pltpu.PrefetchScalarGridSpec(
        num_scalar_prefetch=4,
        grid=(max_tiles + 1,),
        in_specs=[
            pl.BlockSpec(memory_space=pl.ANY),
            pl.BlockSpec((1, d, de), lambda i, te, *_: (te[i], 0, 0)),
            pl.BlockSpec((1, d, de), lambda i, te, *_: (te[i], 0, 0)),
            pl.BlockSpec((1, de, d), lambda i, te, *_: (te[i], 0, 0)),
        ],
        out_specs=pl.BlockSpec(memory_space=pl.ANY),
        scratch_shapes=[
            pltpu.VMEM((2, tm, d), F32),
            pltpu.VMEM((2, tm, d), F32),
            pltpu.VMEM((d, de), BF16),
            pltpu.VMEM((d, de), BF16),
            pltpu.VMEM((de, d), BF16),
            pltpu.SemaphoreType.DMA((2,)),
            pltpu.SemaphoreType.DMA((2,)),
        ],
    )
    return pl.pallas_call(
        _moe_kernel,
        out_shape=jax.ShapeDtypeStruct((2 * t + tm, d), F32),
        grid_spec=grid_spec,
        compiler_params=pltpu.CompilerParams(
            dimension_semantics=("arbitrary",), vmem_limit_bytes=VMEM_LIMIT),
        name="moe",
    )(tile_expert, n_tiles, src_tok, dst_row, h2, w_gate, w_up, w_down)


def _route_plan(rt, n_tok):
    tm = MOE_TM
    n_asg = 2 * n_tok
    n_pad = n_asg + N_EXPERTS * tm
    max_tiles = n_pad // tm
    e_flat = rt[:, :2].astype(jnp.int32).reshape(-1)
    onehot = (e_flat[:, None] == jnp.arange(N_EXPERTS)[None, :]).astype(jnp.int32)
    csum = jnp.cumsum(onehot, axis=0)
    counts = csum[-1]
    rank = jnp.sum(csum * onehot, axis=1) - 1
    tiles_e = (counts + tm - 1) // tm
    tile_end = jnp.cumsum(tiles_e)
    base = (tile_end - tiles_e) * tm
    slot = jnp.sum(onehot * base[None, :], axis=1) + rank
    n_tiles = tile_end[-1]
    tile_id = jnp.arange(max_tiles + 1)
    te = jnp.sum((tile_id[:, None] >= tile_end[None, :]).astype(jnp.int32), axis=1)
    te_last = jnp.sum(((n_tiles - 1) >= tile_end).astype(jnp.int32))
    tile_expert = jnp.where(tile_id < n_tiles, te, te_last).astype(jnp.int32)
    asg = jnp.full((n_pad,), -1, jnp.int32).at[slot].set(jnp.arange(n_asg, dtype=jnp.int32),
                                                          unique_indices=True)
    tok = jnp.maximum(asg, 0) // 2
    src_tok = tok.astype(jnp.int32)
    spare = n_asg + jnp.arange(n_pad, dtype=jnp.int32) % tm
    dst_row = jnp.where(asg >= 0, (asg % 2) * n_tok + tok, spare).astype(jnp.int32)
    dst_row = jnp.concatenate([spare[:tm], dst_row])
    return tile_expert, n_tiles.reshape(1).astype(jnp.int32), src_tok, dst_row


def _final_kernel(x1_ref, y0_ref, y1_ref, rt_ref, mod_ref, modf_ref, g_ref, o_ref):
    rt = rt_ref[...]
    y = rt[:, 2:3] * y0_ref[...] + rt[:, 3:4] * y1_ref[...]
    x2 = x1_ref[0] + mod_ref[0, 5:6, :] * y
    o_ref[0] = _rms(x2, g_ref[...]) * (1.0 + modf_ref[0, 1:2, :]) + modf_ref[0, 0:1, :]


def _final(x1, y, rt, mod, modf, g_final):
    b, s, d = x1.shape
    tm = 256
    nt = s // tm
    n_tok_tiles = (b * s) // tm
    return pl.pallas_call(
        _final_kernel,
        out_shape=jax.ShapeDtypeStruct((b, s, d), F32),
        grid=(b, nt),
        in_specs=[
            pl.BlockSpec((1, tm, d), lambda bi, i: (bi, i, 0)),
            pl.BlockSpec((tm, d), lambda bi, i: (bi * nt + i, 0)),
            pl.BlockSpec((tm, d), lambda bi, i: (n_tok_tiles + bi * nt + i, 0)),
            pl.BlockSpec((tm, LANES), lambda bi, i: (bi * nt + i, 0)),
            pl.BlockSpec((1, 6, d), lambda bi, i: (bi, 0, 0)),
            pl.BlockSpec((1, 2, d), lambda bi, i: (bi, 0, 0)),
            pl.BlockSpec((1, d), lambda bi, i: (0, 0)),
        ],
        out_specs=pl.BlockSpec((1, tm, d), lambda bi, i: (bi, i, 0)),
        compiler_params=pltpu.CompilerParams(
            dimension_semantics=("arbitrary", "arbitrary"), vmem_limit_bytes=VMEM_LIMIT),
        name="final",
    )(x1, y, y, rt, mod, modf, g_final.reshape(1, d))


def kernel(x, c, w_ada, b_ada, g_mix, w_in, sinks_a, rel_bias_b, g_out_a, g_out_b, w_out, g_ffn,
           w_router_group, b_router_group, w_router_expert, b_router_expert, w_gate, w_up, w_down,
           w_ada_final, b_ada_final, g_final):
    b, s, d = x.shape
    assert w_ada.shape[0] == 1, "one layer"
    n_tok = b * s

    c_act = jax.nn.silu(c)
    a_rep = jnp.broadcast_to(c_act[:, :, None], (b, d, LANES))
    mod = _ada(a_rep, w_ada[0], b_ada[0]).reshape(b, 6, d)
    modf = _ada(a_rep, w_ada_final, b_ada_final).reshape(b, 2, d)

    proj = _proj(x, mod, g_mix[0], w_in[0].astype(BF16))

    bias_a, nw_a, bias_b, nw_b = _band_tables(rel_bias_b[0])
    kv_a0 = DA_Q // LANES
    o_a = _attention(proj, bias_a, sinks_a[0].astype(F32), n_groups=N_KV_A, ncol=2, nw=nw_a,
                     q_col0=0, k_col0=kv_a0, v_col0=kv_a0 + DA_KV // LANES, kv_share=2, dup=True)
    qb0 = (DA_Q + 2 * DA_KV) // LANES
    o_b = _attention(proj, bias_b, None, n_groups=N_HEADS_B // 2, ncol=1, nw=nw_b,
                     q_col0=qb0, k_col0=qb0 + DB // LANES, v_col0=qb0 + 2 * DB // LANES,
                     kv_share=1, dup=False)

    n_r = N_GROUPS + N_EXPERTS
    w_router = jnp.zeros((d, LANES), F32).at[:, :n_r].set(
        jnp.concatenate([w_router_group[0], w_router_expert[0]], axis=1))
    b_router = jnp.zeros((LANES,), F32).at[:n_r].set(
        jnp.concatenate([b_router_group[0], b_router_expert[0]]))
    x1, h2, rt = _mix(o_a, o_b, x, mod, g_out_a[0], g_out_b[0], g_ffn[0], w_out[0].astype(BF16),
                      w_router, b_router)

    tile_expert, n_tiles, src_tok, dst_row = _route_plan(rt, n_tok)
    y = _moe(h2.reshape(n_tok, d), w_gate[0], w_up[0], w_down[0], tile_expert, n_tiles, src_tok, dst_row)

    return _final(x1, y, rt, mod, modf, g_final)
```

```python
import functools

import jax
import jax.numpy as jnp
import numpy as np
from jax import lax
from jax.experimental import pallas as pl
from jax.experimental.pallas import tpu as pltpu

D_MODEL = 2048
CHUNK = 64
HEAD_DIM = 64
N_HEADS_A = 16
N_KV_A = 4
N_PREV_A = 2
N_HEADS_B = 16
N_PREV_B = 8
REL_CLIP = 128
DA_Q = N_HEADS_A * HEAD_DIM
DA_KV = N_KV_A * HEAD_DIM
DB = N_HEADS_B * HEAD_DIM
D_IN = DA_Q + 2 * DA_KV + 3 * DB
N_GROUPS = 4
EXPERTS_PER_GROUP = 8
N_EXPERTS = N_GROUPS * EXPERTS_PER_GROUP
D_EXPERT = D_MODEL // 4
EPS = 1e-6
NEG_INF = -1e30

LANES = 128
QBLK = 2 * CHUNK
VMEM_LIMIT = 56 * 1024 * 1024

F32 = jnp.float32
BF16 = jnp.bfloat16


def _rms(x, g):
    return x * lax.rsqrt(jnp.mean(x * x, axis=-1, keepdims=True) + EPS) * g


SLAB_ROWS = D_MODEL // 2 // LANES
U32 = jnp.uint32


def _store_slabs(ref, row0, x):
    rows, d = x.shape
    words = pltpu.pack_elementwise([x[:, :d // 2], x[:, d // 2:]], packed_dtype=BF16)
    for c in range(SLAB_ROWS):
        ref[pl.ds(row0 * SLAB_ROWS + c, rows, stride=SLAB_ROWS), :] = words[:, c * LANES:(c + 1) * LANES]


def _load_slabs(ref, row0, rows, dtype):
    words = [ref[pl.ds(row0 * SLAB_ROWS + c, rows, stride=SLAB_ROWS), :] for c in range(SLAB_ROWS)]
    halves = [pltpu.unpack_elementwise(w, index=k, packed_dtype=BF16, unpacked_dtype=F32).astype(dtype)
              for k in range(2) for w in words]
    return jnp.concatenate(halves, axis=1)


def _ada_kernel(a_ref, w_ref, b_ref, o_ref):
    n_b = a_ref.shape[0]
    tn = w_ref.shape[1]
    for j in range(tn // LANES):
        cols = slice(j * LANES, (j + 1) * LANES)
        w = w_ref[:, cols]
        for b in range(n_b):
            o_ref[b:b + 1, cols] = jnp.sum(a_ref[b] * w, axis=0, keepdims=True) + b_ref[:, cols]


def _ada(a_rep, w, bias):
    n_b, k, _ = a_rep.shape
    n = w.shape[1]
    tn = 1024
    return pl.pallas_call(
        _ada_kernel,
        out_shape=jax.ShapeDtypeStruct((n_b, n), F32),
        grid=(n // tn,),
        in_specs=[
            pl.BlockSpec((n_b, k, LANES), lambda j: (0, 0, 0)),
            pl.BlockSpec((k, tn), lambda j: (0, j)),
            pl.BlockSpec((1, tn), lambda j: (0, j)),
        ],
        out_specs=pl.BlockSpec((n_b, tn), lambda j: (0, j)),
        compiler_params=pltpu.CompilerParams(
            dimension_semantics=("arbitrary",), vmem_limit_bytes=VMEM_LIMIT),
        name="ada",
    )(a_rep, w, bias.reshape(1, n))


def _proj_kernel(x_ref, mod_ref, g_ref, w_ref, o_ref, *, n_chunk):
    h = _rms(x_ref[0], g_ref[...])
    h = h * (1.0 + mod_ref[0, 1:2, :]) + mod_ref[0, 0:1, :]
    hb = h.astype(BF16)
    for n0 in range(0, o_ref.shape[2], n_chunk):
        cols = slice(n0, n0 + n_chunk)
        o_ref[0, :, cols] = jnp.dot(hb, w_ref[:, cols], preferred_element_type=F32).astype(BF16)


def _proj(x, mod, g_mix, w_in_bf16):
    b, s, d = x.shape
    n = w_in_bf16.shape[1]
    tm = 512
    return pl.pallas_call(
        functools.partial(_proj_kernel, n_chunk=512),
        out_shape=jax.ShapeDtypeStruct((b, s, n), BF16),
        grid=(b, s // tm),
        in_specs=[
            pl.BlockSpec((1, tm, d), lambda bi, i: (bi, i, 0)),
            pl.BlockSpec((1, 6, d), lambda bi, i: (bi, 0, 0)),
            pl.BlockSpec((1, d), lambda bi, i: (0, 0)),
            pl.BlockSpec((d, n), lambda bi, i: (0, 0), pipeline_mode=pl.Buffered(1)),
        ],
        out_specs=pl.BlockSpec((1, tm, n), lambda bi, i: (bi, i, 0)),
        compiler_params=pltpu.CompilerParams(
            dimension_semantics=("arbitrary", "arbitrary"), vmem_limit_bytes=VMEM_LIMIT),
        name="proj",
    )(x, mod, g_mix.reshape(1, d), w_in_bf16)


def _attn_kernel(*refs, ncol, nw, dup, has_sink):
    if has_sink:
        sink_ref, q_ref, k_ref, v_ref, bias_ref, o_ref = refs[:6]
        scratch = refs[6:]
    else:
        q_ref, k_ref, v_ref, bias_ref, o_ref = refs[:5]
        scratch = refs[5:]
        sink_ref = None
    s_len = q_ref.shape[1]
    n_blk = s_len // QBLK
    n_stack = 2 * ncol
    grp = pl.program_id(1)

    if dup:
        kd_ref, vd_ref = scratch
        half = grp % 2
        rows = 512

        def dup_body(c, carry):
            r0 = pl.multiple_of(c * rows, rows)
            lane_half = lax.broadcasted_iota(jnp.int32, (rows, LANES), 1) // HEAD_DIM
            keep = lane_half == half
            for src, dst in ((k_ref, kd_ref), (v_ref, vd_ref)):
                t = src[0, pl.ds(r0, rows), :].astype(F32)
                dst[pl.ds(r0, rows), :] = jnp.where(keep, t, pltpu.roll(t, HEAD_DIM, 1)).astype(BF16)
            return carry

        lax.fori_loop(0, s_len // rows, dup_body, 0)
        k_src, v_src = kd_ref, vd_ref
    else:
        k_src, v_src = k_ref.at[0], v_ref.at[0]

    lane = lax.broadcasted_iota(jnp.int32, (QBLK, LANES), 1)
    low = lane < HEAD_DIM

    if has_sink:
        sink = jnp.max(jnp.concatenate(
            [jnp.full((QBLK, LANES), sink_ref[grp * n_stack + h], F32) for h in range(n_stack)], axis=0),
            axis=-1, keepdims=True)

    def attend(j, nvb):
        r0 = j * QBLK if isinstance(j, int) else pl.multiple_of(j * QBLK, QBLK)
        k0 = (j - (nvb - 1)) * QBLK
        if not isinstance(j, int):
            k0 = pl.multiple_of(k0, QBLK)
        qf = q_ref[0, pl.ds(r0, QBLK), :].astype(F32) * (HEAD_DIM ** -0.5)
        parts = []
        for c in range(ncol):
            qc = qf[:, c * LANES:(c + 1) * LANES]
            parts.append(jnp.where(low, qc, 0.0))
            parts.append(jnp.where(low, 0.0, qc))
        lhs = jnp.concatenate(parts, axis=0).astype(BF16)
        kw = k_src[pl.ds(k0, nvb * QBLK), :]
        vw = v_src[pl.ds(k0, nvb * QBLK), :]
        s = lax.dot_general(lhs, kw, (((1,), (1,)), ((), ())), preferred_element_type=F32)
        s = s + bias_ref[0, :, (nw - nvb) * QBLK:]
        m = jnp.max(s, axis=-1, keepdims=True)
        if has_sink:
            m = jnp.maximum(m, sink)
        p = jnp.exp(s - m)
        denom = jnp.sum(p, axis=-1, keepdims=True)
        if has_sink:
            denom = denom + jnp.exp(sink - m)
        o = jnp.dot(p.astype(BF16), vw, preferred_element_type=F32) / denom
        for c in range(ncol):
            o0 = o[(2 * c) * QBLK:(2 * c + 1) * QBLK]
            o1 = o[(2 * c + 1) * QBLK:(2 * c + 2) * QBLK]
            o_ref[0, pl.ds(r0, QBLK), c * LANES:(c + 1) * LANES] = jnp.where(low, o0, o1).astype(BF16)

    for j in range(nw - 1):
        attend(j, j + 1)

    first = nw - 1
    if (n_blk - first) % 2:
        attend(first, nw)
        first += 1

    def body(t, carry):
        j = first + 2 * t
        attend(j, nw)
        attend(j + 1, nw)
        return carry

    lax.fori_loop(0, (n_blk - first) // 2, body, 0)


def _attention(proj, bias, sinks, *, n_groups, ncol, nw, q_col0, k_col0, v_col0, kv_share, dup):
    b, s, _ = proj.shape
    qw = ncol * LANES
    has_sink = sinks is not None
    kernel = functools.partial(_attn_kernel, ncol=ncol, nw=nw, dup=dup, has_sink=has_sink)
    n_rows = bias.shape[1]
    in_specs = [
        pl.BlockSpec((1, s, qw), lambda bi, g, *_: (bi, 0, q_col0 // ncol + g)),
        pl.BlockSpec((1, s, LANES), lambda bi, g, *_: (bi, 0, k_col0 + g // kv_share)),
        pl.BlockSpec((1, s, LANES), lambda bi, g, *_: (bi, 0, v_col0 + g // kv_share)),
        pl.BlockSpec((1, n_rows, nw * QBLK), lambda bi, g, *_: (g, 0, 0)),
    ]
    out_spec = pl.BlockSpec((1, s, qw), lambda bi, g, *_: (bi, 0, g))
    scratch = [pltpu.VMEM((s, LANES), BF16), pltpu.VMEM((s, LANES), BF16)] if dup else []
    grid_spec = pltpu.PrefetchScalarGridSpec(
        num_scalar_prefetch=1 if has_sink else 0,
        grid=(b, n_groups),
        in_specs=in_specs,
        out_specs=out_spec,
        scratch_shapes=scratch,
    )
    args = ((sinks,) if has_sink else ()) + (proj, proj, proj, bias)
    return pl.pallas_call(
        kernel,
        out_shape=jax.ShapeDtypeStruct((b, s, n_groups * qw), BF16),
        grid_spec=grid_spec,
        compiler_params=pltpu.CompilerParams(
            dimension_semantics=("arbitrary", "arbitrary"), vmem_limit_bytes=VMEM_LIMIT),
        name="attn_a" if dup else "attn_b",
    )(*args)


def _band_tables(rel_bias_b):
    qi = jnp.arange(QBLK)[:, None]
    r = qi // CHUNK

    def table(nw, n_prev, fn):
        kj = jnp.arange(nw * QBLK)[None, :]
        rel = (nw - 1) * QBLK + qi - kj
        inband = (kj >= r * CHUNK) & (kj < (r + n_prev + 1) * CHUNK)
        return jnp.where(inband[None], fn(rel), NEG_INF)

    slopes = jnp.exp2(-8.0 * jnp.arange(1, N_HEADS_A + 1, dtype=F32) / N_HEADS_A)
    nw_a = (N_PREV_A * CHUNK) // QBLK + 1
    nw_b = (N_PREV_B * CHUNK) // QBLK + 1
    bias_a = table(nw_a, N_PREV_A, lambda rel: -slopes[:, None, None] * jnp.abs(rel).astype(F32)[None])
    w_b = nw_b * QBLK
    p = QBLK + w_b - 1
    m = (np.arange(p) + QBLK - 1) % p - (QBLK - 1)
    dist = np.clip((nw_b - 1) * QBLK - m, -REL_CLIP, REL_CLIP) + REL_CLIP
    vec = rel_bias_b[:, dist].astype(F32)
    reps = -(-(QBLK * (p - 1)) // p)
    rel_b = jnp.tile(vec, (1, reps))[:, :QBLK * (p - 1)].reshape(N_HEADS_B, QBLK, p - 1)[:, :, :w_b]
    bias_b = table(nw_b, N_PREV_B, lambda rel: rel_b)
    rep = N_HEADS_A // N_KV_A
    bias_a = bias_a.reshape(N_KV_A, rep * QBLK, nw_a * QBLK)
    bias_b = bias_b.reshape(N_HEADS_B // 2, 2 * QBLK, nw_b * QBLK)
    return bias_a, nw_a, bias_b, nw_b


def _mix_kernel(oa_ref, ob_ref, x_ref, mod_ref, ga_ref, gb_ref, gf_ref, wo_ref, wr_ref, br_ref,
                x1_ref, h2_ref, rt_ref):
    half = oa_ref.shape[2]
    na = _rms(oa_ref[0].astype(F32), ga_ref[...]).astype(BF16)
    nb = _rms(ob_ref[0].astype(F32), gb_ref[...]).astype(BF16)
    acc = jnp.dot(na, wo_ref[:half, :], preferred_element_type=F32)
    acc = acc + jnp.dot(nb, wo_ref[half:, :], preferred_element_type=F32)
    x1 = x_ref[0] + mod_ref[0, 2:3, :] * acc
    x1_ref[0] = x1
    h2 = _rms(x1, gf_ref[...]) * (1.0 + mod_ref[0, 4:5, :]) + mod_ref[0, 3:4, :]
    _store_slabs(h2_ref, 0, h2)

    tm = h2.shape[0]
    h_hi = h2.astype(BF16)
    h_lo = (h2 - h_hi.astype(F32)).astype(BF16)
    w = wr_ref[...]
    w_hi = w.astype(BF16)
    w_lo = (w - w_hi.astype(F32)).astype(BF16)
    logits = jnp.dot(h_hi, w_hi, preferred_element_type=F32)
    logits = logits + jnp.dot(h_lo, w_hi, preferred_element_type=F32)
    logits = logits + jnp.dot(h_hi, w_lo, preferred_element_type=F32)
    logits = logits + br_ref[...]

    lane = lax.broadcasted_iota(jnp.int32, (tm, LANES), 1)
    lane_f = lane.astype(F32)
    big = float(LANES)
    ninf = -jnp.inf

    def first_max(vals):
        top = jnp.max(vals, axis=-1, keepdims=True)
        idx = jnp.min(jnp.where(vals == top, lane_f, big), axis=-1, keepdims=True)
        return top, idx

    is_g = lane < N_GROUPS
    g_top, g_idx = first_max(jnp.where(is_g, logits, ninf))
    p_g = 1.0 / jnp.sum(jnp.where(is_g, jnp.exp(logits - g_top), 0.0), axis=-1, keepdims=True)
    lo = N_GROUPS + g_idx * EXPERTS_PER_GROUP
    e_vals = jnp.where((lane_f >= lo) & (lane_f < lo + EXPERTS_PER_GROUP), logits, ninf)
    v1, i1 = first_max(e_vals)
    v2, i2 = first_max(jnp.where(lane_f == i1, ninf, e_vals))
    e2 = jnp.exp(v2 - v1)
    w1 = p_g / (1.0 + e2)
    w2 = p_g * e2 / (1.0 + e2)
    rt = jnp.where(lane == 0, i1 - N_GROUPS,
                   jnp.where(lane == 1, i2 - N_GROUPS,
                             jnp.where(lane == 2, w1, jnp.where(lane == 3, w2, 0.0))))
    rt_ref[...] = rt


def _mix(o_a, o_b, x, mod, g_out_a, g_out_b, g_ffn, w_out_bf16, w_router, b_router):
    b, s, d = x.shape
    half = o_a.shape[2]
    tm = 256
    nt = s // tm
    vec = lambda n: pl.BlockSpec((1, n), lambda bi, i: (0, 0))
    return pl.pallas_call(
        _mix_kernel,
        out_shape=(jax.ShapeDtypeStruct((b, s, d), F32),
                   jax.ShapeDtypeStruct((b * s * SLAB_ROWS, LANES), U32),
                   jax.ShapeDtypeStruct((b * s, LANES), F32)),
        grid=(b, nt),
        in_specs=[
            pl.BlockSpec((1, tm, half), lambda bi, i: (bi, i, 0)),
            pl.BlockSpec((1, tm, half), lambda bi, i: (bi, i, 0)),
            pl.BlockSpec((1, tm, d), lambda bi, i: (bi, i, 0)),
            pl.BlockSpec((1, 6, d), lambda bi, i: (bi, 0, 0)),
            vec(half), vec(half), vec(d),
            pl.BlockSpec((d, d), lambda bi, i: (0, 0), pipeline_mode=pl.Buffered(1)),
            pl.BlockSpec((d, LANES), lambda bi, i: (0, 0)),
            vec(LANES),
        ],
        out_specs=(pl.BlockSpec((1, tm, d), lambda bi, i: (bi, i, 0)),
                   pl.BlockSpec((tm * SLAB_ROWS, LANES), lambda bi, i: (bi * nt + i, 0)),
                   pl.BlockSpec((tm, LANES), lambda bi, i: (bi * nt + i, 0))),
        compiler_params=pltpu.CompilerParams(
            dimension_semantics=("arbitrary", "arbitrary"), vmem_limit_bytes=VMEM_LIMIT),
        name="mix_out",
    )(o_a, o_b, x, mod, g_out_a.reshape(1, half), g_out_b.reshape(1, half), g_ffn.reshape(1, d),
      w_out_bf16, w_router, b_router.reshape(1, LANES))


MOE_TM = 256


def _moe_kernel(te_ref, nt_ref, src_ref, h_hbm, wg_ref, wu_ref, wd_ref, o_ref,
                xbuf, wgb, wub, wdb, gsem):
    i = pl.program_id(0)
    n_tiles = nt_ref[0]
    tm = xbuf.shape[1] // SLAB_ROWS

    def row_in(tile, r, s):
        tok = src_ref[tile * tm + r]
        pltpu.make_async_copy(h_hbm.at[tok], xbuf.at[s, pl.ds(r * SLAB_ROWS, SLAB_ROWS)],
                              gsem.at[s]).start()

    def wait_in(s):
        pltpu.make_async_copy(xbuf.at[s], xbuf.at[s], gsem.at[s]).wait()

    @pl.when(i == 0)
    def _():
        def row(r, carry):
            row_in(0, r, 0)
            return carry
        lax.fori_loop(0, tm, row, 0, unroll=8)

    def step(s):
        wait_in(s)

        @pl.when(jnp.logical_or(i == 0, te_ref[i] != te_ref[jnp.maximum(i - 1, 0)]))
        def _():
            wgb[...] = wg_ref[0].astype(BF16)
            wub[...] = wu_ref[0].astype(BF16)
            wdb[...] = wd_ref[0].astype(BF16)

        xb = _load_slabs(xbuf.at[s], 0, tm, BF16)
        nxt = jnp.minimum(i + 1, n_tiles - 1)
        for r in range(tm):
            row_in(nxt, r, 1 - s)
        g = jnp.dot(xb, wgb[...], preferred_element_type=F32)
        u = jnp.dot(xb, wub[...], preferred_element_type=F32)
        a = (g * jax.nn.sigmoid(g) * u).astype(BF16)
        _store_slabs(o_ref, 0, jnp.dot(a, wdb[...], preferred_element_type=F32))

        @pl.when(i == n_tiles - 1)
        def _():
            wait_in(1 - s)

    for s in range(2):
        pl.when(jnp.logical_and(i < n_tiles, i % 2 == s))(functools.partial(step, s))

    @pl.when(i >= n_tiles)
    def _():
        o_ref[...] = jnp.zeros(o_ref.shape, o_ref.dtype)


def _moe(h2_slabs, w_gate, w_up, w_down, tile_expert, n_tiles, src_tok):
    n_pad = src_tok.shape[0]
    tm = MOE_TM
    max_tiles = n_pad // tm
    _, d, de = w_gate.shape
    grid_spec = pltpu.PrefetchScalarGridSpec(
        num_scalar_prefetch=3,
        grid=(max_tiles,),
        in_specs=[
            pl.BlockSpec(memory_space=pl.ANY),
            pl.BlockSpec((1, d, de), lambda i, te, *_: (te[i], 0, 0)),
            pl.BlockSpec((1, d, de), lambda i, te, *_: (te[i], 0, 0)),
            pl.BlockSpec((1, de, d), lambda i, te, *_: (te[i], 0, 0)),
        ],
        out_specs=pl.BlockSpec((tm * SLAB_ROWS, LANES), lambda i, *_: (i, 0)),
        scratch_shapes=[
            pltpu.VMEM((2, tm * SLAB_ROWS, LANES), U32),
            pltpu.VMEM((d, de), BF16),
            pltpu.VMEM((d, de), BF16),
            pltpu.VMEM((de, d), BF16),
            pltpu.SemaphoreType.DMA((2,)),
        ],
    )
    return pl.pallas_call(
        _moe_kernel,
        out_shape=jax.ShapeDtypeStruct((n_pad * SLAB_ROWS, LANES), U32),
        grid_spec=grid_spec,
        compiler_params=pltpu.CompilerParams(
            dimension_semantics=("arbitrary",), vmem_limit_bytes=VMEM_LIMIT),
        name="moe",
    )(tile_expert, n_tiles, src_tok, h2_slabs.reshape(-1, SLAB_ROWS, LANES), w_gate, w_up, w_down)


def _route_plan(rt, n_tok):
    tm = MOE_TM
    n_asg = 2 * n_tok
    n_pad = n_asg + N_EXPERTS * tm
    max_tiles = n_pad // tm
    e_flat = rt[:, :2].astype(jnp.int32).reshape(-1)
    onehot = (e_flat[:, None] == jnp.arange(N_EXPERTS)[None, :]).astype(jnp.int32)
    csum = jnp.cumsum(onehot, axis=0)
    counts = csum[-1]
    rank = jnp.sum(csum * onehot, axis=1) - 1
    tiles_e = (counts + tm - 1) // tm
    tile_end = jnp.cumsum(tiles_e)
    base = (tile_end - tiles_e) * tm
    slot = jnp.sum(onehot * base[None, :], axis=1) + rank
    n_tiles = tile_end[-1]
    tile_id = jnp.arange(max_tiles)
    te = jnp.sum((tile_id[:, None] >= tile_end[None, :]).astype(jnp.int32), axis=1)
    te_last = jnp.sum(((n_tiles - 1) >= tile_end).astype(jnp.int32))
    tile_expert = jnp.where(tile_id < n_tiles, te, te_last).astype(jnp.int32)
    asg = jnp.full((n_pad,), -1, jnp.int32).at[slot].set(jnp.arange(n_asg, dtype=jnp.int32),
                                                          unique_indices=True)
    src_tok = (jnp.maximum(asg, 0) // 2).astype(jnp.int32)
    return tile_expert, n_tiles.reshape(1).astype(jnp.int32), src_tok, slot.astype(jnp.int32)


FINAL_TM = 256


def _final_kernel(slot_ref, x1_ref, rt_ref, mod_ref, modf_ref, g_ref, y_hbm, o_ref, ybuf, gsem):
    i = pl.program_id(0)
    n_steps = pl.num_programs(0)
    tm = x1_ref.shape[0]

    def row_in(tile, r, k, s):
        idx = slot_ref[2 * (tile * tm + r) + k]
        pltpu.make_async_copy(y_hbm.at[idx], ybuf.at[s, pl.ds((k * tm + r) * SLAB_ROWS, SLAB_ROWS)],
                              gsem.at[s]).start()

    def wait_in(s):
        pltpu.make_async_copy(ybuf.at[s], ybuf.at[s], gsem.at[s]).wait()

    @pl.when(i == 0)
    def _():
        def row(r, carry):
            row_in(0, r, 0, 0)
            row_in(0, r, 1, 0)
            return carry
        lax.fori_loop(0, tm, row, 0, unroll=8)

    def step(s):
        wait_in(s)
        y0 = _load_slabs(ybuf.at[s], 0, tm, F32)
        y1 = _load_slabs(ybuf.at[s], tm, tm, F32)
        nxt = jnp.minimum(i + 1, n_steps - 1)
        for r in range(tm):
            row_in(nxt, r, 0, 1 - s)
            row_in(nxt, r, 1, 1 - s)
        rt = rt_ref[...]
        y = rt[:, 2:3] * y0 + rt[:, 3:4] * y1
        x2 = x1_ref[...] + mod_ref[0, 5:6, :] * y
        o_ref[...] = _rms(x2, g_ref[...]) * (1.0 + modf_ref[0, 1:2, :]) + modf_ref[0, 0:1, :]

        @pl.when(i == n_steps - 1)
        def _():
            wait_in(1 - s)

    for s in range(2):
        pl.when(i % 2 == s)(functools.partial(step, s))


def _final(x1, y_slabs, slot, rt, mod, modf, g_final):
    b, s, d = x1.shape
    tm = FINAL_TM
    nt = s // tm
    grid_spec = pltpu.PrefetchScalarGridSpec(
        num_scalar_prefetch=1,
        grid=(b * nt,),
        in_specs=[
            pl.BlockSpec((tm, d), lambda i, *_: (i, 0)),
            pl.BlockSpec((tm, LANES), lambda i, *_: (i, 0)),
            pl.BlockSpec((1, 6, d), lambda i, *_: (i // nt, 0, 0)),
            pl.BlockSpec((1, 2, d), lambda i, *_: (i // nt, 0, 0)),
            pl.BlockSpec((1, d), lambda i, *_: (0, 0)),
            pl.BlockSpec(memory_space=pl.ANY),
        ],
        out_specs=pl.BlockSpec((tm, d), lambda i, *_: (i, 0)),
        scratch_shapes=[
            pltpu.VMEM((2, 2 * tm * SLAB_ROWS, LANES), U32),
            pltpu.SemaphoreType.DMA((2,)),
        ],
    )
    out = pl.pallas_call(
        _final_kernel,
        out_shape=jax.ShapeDtypeStruct((b * s, d), F32),
        grid_spec=grid_spec,
        compiler_params=pltpu.CompilerParams(
            dimension_semantics=("arbitrary",), vmem_limit_bytes=VMEM_LIMIT),
        name="final",
    )(slot, x1.reshape(b * s, d), rt, mod, modf, g_final.reshape(1, d),
      y_slabs.reshape(-1, SLAB_ROWS, LANES))
    return out.reshape(b, s, d)


def kernel(x, c, w_ada, b_ada, g_mix, w_in, sinks_a, rel_bias_b, g_out_a, g_out_b, w_out, g_ffn,
           w_router_group, b_router_group, w_router_expert, b_router_expert, w_gate, w_up, w_down,
           w_ada_final, b_ada_final, g_final):
    b, s, d = x.shape
    assert w_ada.shape[0] == 1, "one layer"
    n_tok = b * s

    c_act = jax.nn.silu(c)
    a_rep = jnp.broadcast_to(c_act[:, :, None], (b, d, LANES))
    mod = _ada(a_rep, w_ada[0], b_ada[0]).reshape(b, 6, d)
    modf = _ada(a_rep, w_ada_final, b_ada_final).reshape(b, 2, d)

    proj = _proj(x, mod, g_mix[0], w_in[0].astype(BF16))

    bias_a, nw_a, bias_b, nw_b = _band_tables(rel_bias_b[0])
    kv_a0 = DA_Q // LANES
    o_a = _attention(proj, bias_a, sinks_a[0].astype(F32), n_groups=N_KV_A, ncol=2, nw=nw_a,
                     q_col0=0, k_col0=kv_a0, v_col0=kv_a0 + DA_KV // LANES, kv_share=2, dup=True)
    qb0 = (DA_Q + 2 * DA_KV) // LANES
    o_b = _attention(proj, bias_b, None, n_groups=N_HEADS_B // 2, ncol=1, nw=nw_b,
                     q_col0=qb0, k_col0=qb0 + DB // LANES, v_col0=qb0 + 2 * DB // LANES,
                     kv_share=1, dup=False)

    n_r = N_GROUPS + N_EXPERTS
    w_router = jnp.zeros((d, LANES), F32).at[:, :n_r].set(
        jnp.concatenate([w_router_group[0], w_router_expert[0]], axis=1))
    b_router = jnp.zeros((LANES,), F32).at[:n_r].set(
        jnp.concatenate([b_router_group[0], b_router_expert[0]]))
    x1, h2, rt = _mix(o_a, o_b, x, mod, g_out_a[0], g_out_b[0], g_ffn[0], w_out[0].astype(BF16),
                      w_router, b_router)

    tile_expert, n_tiles, src_tok, slot = _route_plan(rt, n_tok)
    y = _moe(h2, w_gate[0], w_up[0], w_down[0], tile_expert, n_tiles, src_tok)

    return _final(x1, y, slot, rt, mod, modf, g_final)
```

```python
import functools

import jax
import jax.numpy as jnp
import numpy as np
from jax import lax
from jax.experimental import pallas as pl
from jax.experimental.pallas import tpu as pltpu

D_MODEL = 2048
CHUNK = 64
HEAD_DIM = 64
N_HEADS_A = 16
N_KV_A = 4
N_PREV_A = 2
N_HEADS_B = 16
N_PREV_B = 8
REL_CLIP = 128
DA_Q = N_HEADS_A * HEAD_DIM
DA_KV = N_KV_A * HEAD_DIM
DB = N_HEADS_B * HEAD_DIM
D_IN = DA_Q + 2 * DA_KV + 3 * DB
N_GROUPS = 4
EXPERTS_PER_GROUP = 8
N_EXPERTS = N_GROUPS * EXPERTS_PER_GROUP
D_EXPERT = D_MODEL // 4
EPS = 1e-6
NEG_INF = -1e30

LANES = 128
QBLK = 2 * CHUNK
VMEM_LIMIT = 56 * 1024 * 1024

F32 = jnp.float32
BF16 = jnp.bfloat16


def _rms(x, g):
    return x * lax.rsqrt(jnp.mean(x * x, axis=-1, keepdims=True) + EPS) * g


def _ada_kernel(a_ref, w_ref, b_ref, o_ref):
    n_b = a_ref.shape[0]
    tn = w_ref.shape[1]
    for j in range(tn // LANES):
        cols = slice(j * LANES, (j + 1) * LANES)
        w = w_ref[:, cols]
        for b in range(n_b):
            o_ref[b:b + 1, cols] = jnp.sum(a_ref[b] * w, axis=0, keepdims=True) + b_ref[:, cols]


def _ada(a_rep, w, bias):
    n_b, k, _ = a_rep.shape
    n = w.shape[1]
    tn = 1024
    return pl.pallas_call(
        _ada_kernel,
        out_shape=jax.ShapeDtypeStruct((n_b, n), F32),
        grid=(n // tn,),
        in_specs=[
            pl.BlockSpec((n_b, k, LANES), lambda j: (0, 0, 0)),
            pl.BlockSpec((k, tn), lambda j: (0, j)),
            pl.BlockSpec((1, tn), lambda j: (0, j)),
        ],
        out_specs=pl.BlockSpec((n_b, tn), lambda j: (0, j)),
        compiler_params=pltpu.CompilerParams(
            dimension_semantics=("arbitrary",), vmem_limit_bytes=VMEM_LIMIT),
        name="ada",
    )(a_rep, w, bias.reshape(1, n))


def _proj_kernel(x_ref, mod_ref, g_ref, w_ref, o_ref, *, n_chunk):
    h = _rms(x_ref[0], g_ref[...])
    h = h * (1.0 + mod_ref[0, 1:2, :]) + mod_ref[0, 0:1, :]
    hb = h.astype(BF16)
    for n0 in range(0, o_ref.shape[2], n_chunk):
        cols = slice(n0, n0 + n_chunk)
        o_ref[0, :, cols] = jnp.dot(hb, w_ref[:, cols], preferred_element_type=F32).astype(BF16)


def _proj(x, mod, g_mix, w_in_bf16):
    b, s, d = x.shape
    n = w_in_bf16.shape[1]
    tm = 512
    return pl.pallas_call(
        functools.partial(_proj_kernel, n_chunk=512),
        out_shape=jax.ShapeDtypeStruct((b, s, n), BF16),
        grid=(b, s // tm),
        in_specs=[
            pl.BlockSpec((1, tm, d), lambda bi, i: (bi, i, 0)),
            pl.BlockSpec((1, 6, d), lambda bi, i: (bi, 0, 0)),
            pl.BlockSpec((1, d), lambda bi, i: (0, 0)),
            pl.BlockSpec((d, n), lambda bi, i: (0, 0), pipeline_mode=pl.Buffered(1)),
        ],
        out_specs=pl.BlockSpec((1, tm, n), lambda bi, i: (bi, i, 0)),
        compiler_params=pltpu.CompilerParams(
            dimension_semantics=("arbitrary", "arbitrary"), vmem_limit_bytes=VMEM_LIMIT),
        name="proj",
    )(x, mod, g_mix.reshape(1, d), w_in_bf16)


def _attn_kernel(*refs, ncol, nw, dup, has_sink):
    if has_sink:
        sink_ref, q_ref, k_ref, v_ref, bias_ref, o_ref = refs[:6]
        scratch = refs[6:]
    else:
        q_ref, k_ref, v_ref, bias_ref, o_ref = refs[:5]
        scratch = refs[5:]
        sink_ref = None
    s_len = q_ref.shape[1]
    n_blk = s_len // QBLK
    n_stack = 2 * ncol
    grp = pl.program_id(1)

    if dup:
        kd_ref, vd_ref = scratch
        half = grp % 2
        rows = 512

        def dup_body(c, carry):
            r0 = pl.multiple_of(c * rows, rows)
            lane_half = lax.broadcasted_iota(jnp.int32, (rows, LANES), 1) // HEAD_DIM
            keep = lane_half == half
            for src, dst in ((k_ref, kd_ref), (v_ref, vd_ref)):
                t = src[0, pl.ds(r0, rows), :].astype(F32)
                dst[pl.ds(r0, rows), :] = jnp.where(keep, t, pltpu.roll(t, HEAD_DIM, 1)).astype(BF16)
            return carry

        lax.fori_loop(0, s_len // rows, dup_body, 0)
        k_src, v_src = kd_ref, vd_ref
    else:
        k_src, v_src = k_ref.at[0], v_ref.at[0]

    lane = lax.broadcasted_iota(jnp.int32, (QBLK, LANES), 1)
    low = lane < HEAD_DIM

    if has_sink:
        sink = jnp.max(jnp.concatenate(
            [jnp.full((QBLK, LANES), sink_ref[grp * n_stack + h], F32) for h in range(n_stack)], axis=0),
            axis=-1, keepdims=True)

    def attend(j, nvb):
        r0 = j * QBLK if isinstance(j, int) else pl.multiple_of(j * QBLK, QBLK)
        k0 = (j - (nvb - 1)) * QBLK
        if not isinstance(j, int):
            k0 = pl.multiple_of(k0, QBLK)
        qf = q_ref[0, pl.ds(r0, QBLK), :].astype(F32) * (HEAD_DIM ** -0.5)
        parts = []
        for c in range(ncol):
            qc = qf[:, c * LANES:(c + 1) * LANES]
            parts.append(jnp.where(low, qc, 0.0))
            parts.append(jnp.where(low, 0.0, qc))
        lhs = jnp.concatenate(parts, axis=0).astype(BF16)
        kw = k_src[pl.ds(k0, nvb * QBLK), :]
        vw = v_src[pl.ds(k0, nvb * QBLK), :]
        s = lax.dot_general(lhs, kw, (((1,), (1,)), ((), ())), preferred_element_type=F32)
        s = s + bias_ref[0, :, (nw - nvb) * QBLK:]
        m = jnp.max(s, axis=-1, keepdims=True)
        if has_sink:
            m = jnp.maximum(m, sink)
        p = jnp.exp(s - m)
        denom = jnp.sum(p, axis=-1, keepdims=True)
        if has_sink:
            denom = denom + jnp.exp(sink - m)
        o = jnp.dot(p.astype(BF16), vw, preferred_element_type=F32) / denom
        for c in range(ncol):
            o0 = o[(2 * c) * QBLK:(2 * c + 1) * QBLK]
            o1 = o[(2 * c + 1) * QBLK:(2 * c + 2) * QBLK]
            o_ref[0, pl.ds(r0, QBLK), c * LANES:(c + 1) * LANES] = jnp.where(low, o0, o1).astype(BF16)

    for j in range(nw - 1):
        attend(j, j + 1)

    first = nw - 1
    if (n_blk - first) % 2:
        attend(first, nw)
        first += 1

    def body(t, carry):
        j = first + 2 * t
        attend(j, nw)
        attend(j + 1, nw)
        return carry

    lax.fori_loop(0, (n_blk - first) // 2, body, 0)


def _attention(proj, bias, sinks, *, n_groups, ncol, nw, q_col0, k_col0, v_col0, kv_share, dup):
    b, s, _ = proj.shape
    qw = ncol * LANES
    has_sink = sinks is not None
    kernel = functools.partial(_attn_kernel, ncol=ncol, nw=nw, dup=dup, has_sink=has_sink)
    n_rows = bias.shape[1]
    in_specs = [
        pl.BlockSpec((1, s, qw), lambda bi, g, *_: (bi, 0, q_col0 // ncol + g)),
        pl.BlockSpec((1, s, LANES), lambda bi, g, *_: (bi, 0, k_col0 + g // kv_share)),
        pl.BlockSpec((1, s, LANES), lambda bi, g, *_: (bi, 0, v_col0 + g // kv_share)),
        pl.BlockSpec((1, n_rows, nw * QBLK), lambda bi, g, *_: (g, 0, 0)),
    ]
    out_spec = pl.BlockSpec((1, s, qw), lambda bi, g, *_: (bi, 0, g))
    scratch = [pltpu.VMEM((s, LANES), BF16), pltpu.VMEM((s, LANES), BF16)] if dup else []
    grid_spec = pltpu.PrefetchScalarGridSpec(
        num_scalar_prefetch=1 if has_sink else 0,
        grid=(b, n_groups),
        in_specs=in_specs,
        out_specs=out_spec,
        scratch_shapes=scratch,
    )
    args = ((sinks,) if has_sink else ()) + (proj, proj, proj, bias)
    return pl.pallas_call(
        kernel,
        out_shape=jax.ShapeDtypeStruct((b, s, n_groups * qw), BF16),
        grid_spec=grid_spec,
        compiler_params=pltpu.CompilerParams(
            dimension_semantics=("arbitrary", "arbitrary"), vmem_limit_bytes=VMEM_LIMIT),
        name="attn_a" if dup else "attn_b",
    )(*args)


def _band_tables(rel_bias_b):
    qi = jnp.arange(QBLK)[:, None]
    r = qi // CHUNK

    def table(nw, n_prev, fn):
        kj = jnp.arange(nw * QBLK)[None, :]
        rel = (nw - 1) * QBLK + qi - kj
        inband = (kj >= r * CHUNK) & (kj < (r + n_prev + 1) * CHUNK)
        return jnp.where(inband[None], fn(rel), NEG_INF)

    slopes = jnp.exp2(-8.0 * jnp.arange(1, N_HEADS_A + 1, dtype=F32) / N_HEADS_A)
    nw_a = (N_PREV_A * CHUNK) // QBLK + 1
    nw_b = (N_PREV_B * CHUNK) // QBLK + 1
    bias_a = table(nw_a, N_PREV_A, lambda rel: -slopes[:, None, None] * jnp.abs(rel).astype(F32)[None])
    w_b = nw_b * QBLK
    p = QBLK + w_b - 1
    m = (np.arange(p) + QBLK - 1) % p - (QBLK - 1)
    dist = np.clip((nw_b - 1) * QBLK - m, -REL_CLIP, REL_CLIP) + REL_CLIP
    vec = rel_bias_b[:, dist].astype(F32)
    reps = -(-(QBLK * (p - 1)) // p)
    rel_b = jnp.tile(vec, (1, reps))[:, :QBLK * (p - 1)].reshape(N_HEADS_B, QBLK, p - 1)[:, :, :w_b]
    bias_b = table(nw_b, N_PREV_B, lambda rel: rel_b)
    rep = N_HEADS_A // N_KV_A
    bias_a = bias_a.reshape(N_KV_A, rep * QBLK, nw_a * QBLK)
    bias_b = bias_b.reshape(N_HEADS_B // 2, 2 * QBLK, nw_b * QBLK)
    return bias_a, nw_a, bias_b, nw_b


def _mix_kernel(oa_ref, ob_ref, x_ref, mod_ref, ga_ref, gb_ref, gf_ref, wo_ref, wr_ref, br_ref,
                x1_ref, h2_ref, rt_ref):
    half = oa_ref.shape[2]
    na = _rms(oa_ref[0].astype(F32), ga_ref[...]).astype(BF16)
    nb = _rms(ob_ref[0].astype(F32), gb_ref[...]).astype(BF16)
    acc = jnp.dot(na, wo_ref[:half, :], preferred_element_type=F32)
    acc = acc + jnp.dot(nb, wo_ref[half:, :], preferred_element_type=F32)
    x1 = x_ref[0] + mod_ref[0, 2:3, :] * acc
    x1_ref[0] = x1
    h2 = _rms(x1, gf_ref[...]) * (1.0 + mod_ref[0, 4:5, :]) + mod_ref[0, 3:4, :]
    h2_ref[0] = h2

    tm = h2.shape[0]
    h_hi = h2.astype(BF16)
    h_lo = (h2 - h_hi.astype(F32)).astype(BF16)
    w = wr_ref[...]
    w_hi = w.astype(BF16)
    w_lo = (w - w_hi.astype(F32)).astype(BF16)
    logits = jnp.dot(h_hi, w_hi, preferred_element_type=F32)
    logits = logits + jnp.dot(h_lo, w_hi, preferred_element_type=F32)
    logits = logits + jnp.dot(h_hi, w_lo, preferred_element_type=F32)
    logits = logits + br_ref[...]

    lane = lax.broadcasted_iota(jnp.int32, (tm, LANES), 1)
    lane_f = lane.astype(F32)
    big = float(LANES)
    ninf = -jnp.inf

    def first_max(vals):
        top = jnp.max(vals, axis=-1, keepdims=True)
        idx = jnp.min(jnp.where(vals == top, lane_f, big), axis=-1, keepdims=True)
        return top, idx

    is_g = lane < N_GROUPS
    g_top, g_idx = first_max(jnp.where(is_g, logits, ninf))
    p_g = 1.0 / jnp.sum(jnp.where(is_g, jnp.exp(logits - g_top), 0.0), axis=-1, keepdims=True)
    lo = N_GROUPS + g_idx * EXPERTS_PER_GROUP
    e_vals = jnp.where((lane_f >= lo) & (lane_f < lo + EXPERTS_PER_GROUP), logits, ninf)
    v1, i1 = first_max(e_vals)
    v2, i2 = first_max(jnp.where(lane_f == i1, ninf, e_vals))
    e2 = jnp.exp(v2 - v1)
    w1 = p_g / (1.0 + e2)
    w2 = p_g * e2 / (1.0 + e2)
    rt = jnp.where(lane == 0, i1 - N_GROUPS,
                   jnp.where(lane == 1, i2 - N_GROUPS,
                             jnp.where(lane == 2, w1, jnp.where(lane == 3, w2, 0.0))))
    rt_ref[...] = rt


def _mix(o_a, o_b, x, mod, g_out_a, g_out_b, g_ffn, w_out_bf16, w_router, b_router):
    b, s, d = x.shape
    half = o_a.shape[2]
    tm = 256
    nt = s // tm
    vec = lambda n: pl.BlockSpec((1, n), lambda bi, i: (0, 0))
    return pl.pallas_call(
        _mix_kernel,
        out_shape=(jax.ShapeDtypeStruct((b, s, d), F32),
                   jax.ShapeDtypeStruct((b, s, d), F32),
                   jax.ShapeDtypeStruct((b * s, LANES), F32)),
        grid=(b, nt),
        in_specs=[
            pl.BlockSpec((1, tm, half), lambda bi, i: (bi, i, 0)),
            pl.BlockSpec((1, tm, half), lambda bi, i: (bi, i, 0)),
            pl.BlockSpec((1, tm, d), lambda bi, i: (bi, i, 0)),
            pl.BlockSpec((1, 6, d), lambda bi, i: (bi, 0, 0)),
            vec(half), vec(half), vec(d),
            pl.BlockSpec((d, d), lambda bi, i: (0, 0), pipeline_mode=pl.Buffered(1)),
            pl.BlockSpec((d, LANES), lambda bi, i: (0, 0)),
            vec(LANES),
        ],
        out_specs=(pl.BlockSpec((1, tm, d), lambda bi, i: (bi, i, 0)),
                   pl.BlockSpec((1, tm, d), lambda bi, i: (bi, i, 0)),
                   pl.BlockSpec((tm, LANES), lambda bi, i: (bi * nt + i, 0))),
        compiler_params=pltpu.CompilerParams(
            dimension_semantics=("arbitrary", "arbitrary"), vmem_limit_bytes=VMEM_LIMIT),
        name="mix_out",
    )(o_a, o_b, x, mod, g_out_a.reshape(1, half), g_out_b.reshape(1, half), g_ffn.reshape(1, d),
      w_out_bf16, w_router, b_router.reshape(1, LANES))


MOE_TM = 256


ROW_BUFS = 3


def _moe_kernel(te_ref, nxe_ref, nt_ref, src_ref, h_hbm, wg_hbm, wu_hbm, wd_hbm, o_ref,
                xbuf, wg32, wu32, wd32, wgb, wub, wdb, gsem, wsem):
    i = pl.program_id(0)
    n_tiles = nt_ref[0]
    tm = xbuf.shape[1]

    def row_in(tile, r, s):
        tok = src_ref[tile * tm + r]
        pltpu.make_async_copy(h_hbm.at[pl.ds(tok, 1)], xbuf.at[s, pl.ds(r, 1)],
                              gsem.at[s]).start(priority=r % 2 if isinstance(r, int) else 0)

    def wait_in(s):
        pltpu.make_async_copy(h_hbm.at[pl.ds(0, tm)], xbuf.at[s], gsem.at[s]).wait()

    def weight_copies(e):
        return [pltpu.make_async_copy(src.at[e], dst, wsem.at[0])
                for src, dst in ((wg_hbm, wg32), (wu_hbm, wu32), (wd_hbm, wd32))]

    @pl.when(i == 0)
    def _():
        for cp in weight_copies(te_ref[0]):
            cp.start()
        for t in range(ROW_BUFS - 1):
            tile = jnp.minimum(t, n_tiles - 1)

            def row(r, carry):
                row_in(tile, r, t)
                return carry
            lax.fori_loop(0, tm, row, 0, unroll=8)

    def step(s):
        wait_in(s)

        @pl.when(jnp.logical_or(i == 0, te_ref[i] != te_ref[jnp.maximum(i - 1, 0)]))
        def _():
            for cp in weight_copies(0):
                cp.wait()
            wgb[...] = wg32[...].astype(BF16)
            wub[...] = wu32[...].astype(BF16)
            wdb[...] = wd32[...].astype(BF16)

            @pl.when(nxe_ref[i] >= 0)
            def _():
                for cp in weight_copies(nxe_ref[i]):
                    cp.start()

        xb = xbuf[s].astype(BF16)
        nxt = jnp.minimum(i + ROW_BUFS - 1, n_tiles - 1)
        for r in range(tm):
            row_in(nxt, r, (s + ROW_BUFS - 1) % ROW_BUFS)
        g = jnp.dot(xb, wgb[...], preferred_element_type=F32)
        u = jnp.dot(xb, wub[...], preferred_element_type=F32)
        a = (g * jax.nn.sigmoid(g) * u).astype(BF16)
        o_ref[...] = jnp.dot(a, wdb[...], preferred_element_type=F32)

        @pl.when(i == n_tiles - 1)
        def _():
            for t in range(1, ROW_BUFS):
                wait_in((s + t) % ROW_BUFS)

    for s in range(ROW_BUFS):
        pl.when(jnp.logical_and(i < n_tiles, i % ROW_BUFS == s))(functools.partial(step, s))

    @pl.when(i >= n_tiles)
    def _():
        o_ref[...] = jnp.zeros(o_ref.shape, o_ref.dtype)


def _moe(h2, w_gate, w_up, w_down, tile_expert, next_expert, n_tiles, src_tok):
    n_pad = src_tok.shape[0]
    tm = MOE_TM
    max_tiles = n_pad // tm
    _, d, de = w_gate.shape
    grid_spec = pltpu.PrefetchScalarGridSpec(
        num_scalar_prefetch=4,
        grid=(max_tiles,),
        in_specs=[pl.BlockSpec(memory_space=pl.ANY)] * 4,
        out_specs=pl.BlockSpec((tm, d), lambda i, *_: (i, 0)),
        scratch_shapes=[
            pltpu.VMEM((ROW_BUFS, tm, d), F32),
            pltpu.VMEM((d, de), F32),
            pltpu.VMEM((d, de), F32),
            pltpu.VMEM((de, d), F32),
            pltpu.VMEM((d, de), BF16),
            pltpu.VMEM((d, de), BF16),
            pltpu.VMEM((de, d), BF16),
            pltpu.SemaphoreType.DMA((ROW_BUFS,)),
            pltpu.SemaphoreType.DMA((1,)),
        ],
    )
    return pl.pallas_call(
        _moe_kernel,
        out_shape=jax.ShapeDtypeStruct((n_pad, d), F32),
        grid_spec=grid_spec,
        compiler_params=pltpu.CompilerParams(
            dimension_semantics=("arbitrary",), vmem_limit_bytes=VMEM_LIMIT),
        name="moe",
    )(tile_expert, next_expert, n_tiles, src_tok, h2, w_gate, w_up, w_down)


def _route_plan(rt, n_tok):
    tm = MOE_TM
    n_asg = 2 * n_tok
    n_pad = n_asg + N_EXPERTS * tm
    max_tiles = n_pad // tm
    e_flat = rt[:, :2].astype(jnp.int32).reshape(-1)
    onehot = (e_flat[:, None] == jnp.arange(N_EXPERTS)[None, :]).astype(jnp.int32)
    csum = jnp.cumsum(onehot, axis=0)
    counts = csum[-1]
    rank = jnp.sum(csum * onehot, axis=1) - 1
    tiles_e = (counts + tm - 1) // tm
    tile_end = jnp.cumsum(tiles_e)
    base = (tile_end - tiles_e) * tm
    slot = jnp.sum(onehot * base[None, :], axis=1) + rank
    n_tiles = tile_end[-1]
    tile_id = jnp.arange(max_tiles)
    te = jnp.sum((tile_id[:, None] >= tile_end[None, :]).astype(jnp.int32), axis=1)
    te_last = jnp.sum(((n_tiles - 1) >= tile_end).astype(jnp.int32))
    tile_expert = jnp.where(tile_id < n_tiles, te, te_last).astype(jnp.int32)
    ex = jnp.arange(N_EXPERTS)
    later = (ex[None, :] > ex[:, None]) & (counts[None, :] > 0)
    nxt_e = jnp.min(jnp.where(later, ex[None, :], N_EXPERTS), axis=1)
    nxt_e = jnp.where(nxt_e < N_EXPERTS, nxt_e, -1)
    sel = (tile_expert[:, None] == ex[None, :]).astype(jnp.int32)
    next_expert = jnp.sum(sel * nxt_e[None, :], axis=1).astype(jnp.int32)
    asg = jnp.full((n_pad,), -1, jnp.int32).at[slot].set(jnp.arange(n_asg, dtype=jnp.int32),
                                                          unique_indices=True)
    src_tok = (jnp.maximum(asg, 0) // 2).astype(jnp.int32)
    return (tile_expert, next_expert, n_tiles.reshape(1).astype(jnp.int32), src_tok,
            slot.astype(jnp.int32))


FINAL_TM = 256


def _final_kernel(slot_ref, x1_ref, rt_ref, mod_ref, modf_ref, g_ref, y_hbm, o_ref, ybuf, gsem):
    i = pl.program_id(0)
    n_steps = pl.num_programs(0)
    tm = x1_ref.shape[0]

    def row_in(tile, r, k, s):
        idx = slot_ref[2 * (tile * tm + r) + k]
        pltpu.make_async_copy(y_hbm.at[pl.ds(idx, 1)], ybuf.at[s, pl.ds(k * tm + r, 1)],
                              gsem.at[s]).start(priority=k)

    def wait_in(s):
        pltpu.make_async_copy(y_hbm.at[pl.ds(0, 2 * tm)], ybuf.at[s], gsem.at[s]).wait()

    @pl.when(i == 0)
    def _():
        for t in range(ROW_BUFS - 1):
            tile = jnp.minimum(t, n_steps - 1)

            def row(r, carry):
                row_in(tile, r, 0, t)
                row_in(tile, r, 1, t)
                return carry
            lax.fori_loop(0, tm, row, 0, unroll=8)

    def step(s):
        wait_in(s)
        rt = rt_ref[...]
        y = rt[:, 2:3] * ybuf[s, :tm, :] + rt[:, 3:4] * ybuf[s, tm:, :]
        nxt = jnp.minimum(i + ROW_BUFS - 1, n_steps - 1)
        for r in range(tm):
            row_in(nxt, r, 0, (s + ROW_BUFS - 1) % ROW_BUFS)
            row_in(nxt, r, 1, (s + ROW_BUFS - 1) % ROW_BUFS)
        x2 = x1_ref[...] + mod_ref[0, 5:6, :] * y
        o_ref[...] = _rms(x2, g_ref[...]) * (1.0 + modf_ref[0, 1:2, :]) + modf_ref[0, 0:1, :]

        @pl.when(i == n_steps - 1)
        def _():
            for t in range(1, ROW_BUFS):
                wait_in((s + t) % ROW_BUFS)

    for s in range(ROW_BUFS):
        pl.when(i % ROW_BUFS == s)(functools.partial(step, s))


def _final(x1, y_rows, slot, rt, mod, modf, g_final):
    b, s, d = x1.shape
    tm = FINAL_TM
    nt = s // tm
    grid_spec = pltpu.PrefetchScalarGridSpec(
        num_scalar_prefetch=1,
        grid=(b * nt,),
        in_specs=[
            pl.BlockSpec((tm, d), lambda i, *_: (i, 0)),
            pl.BlockSpec((tm, LANES), lambda i, *_: (i, 0)),
            pl.BlockSpec((1, 6, d), lambda i, *_: (i // nt, 0, 0)),
            pl.BlockSpec((1, 2, d), lambda i, *_: (i // nt, 0, 0)),
            pl.BlockSpec((1, d), lambda i, *_: (0, 0)),
            pl.BlockSpec(memory_space=pl.ANY),
        ],
        out_specs=pl.BlockSpec((tm, d), lambda i, *_: (i, 0)),
        scratch_shapes=[
            pltpu.VMEM((ROW_BUFS, 2 * tm, d), F32),
            pltpu.SemaphoreType.DMA((ROW_BUFS,)),
        ],
    )
    out = pl.pallas_call(
        _final_kernel,
        out_shape=jax.ShapeDtypeStruct((b * s, d), F32),
        grid_spec=grid_spec,
        compiler_params=pltpu.CompilerParams(
            dimension_semantics=("arbitrary",), vmem_limit_bytes=VMEM_LIMIT),
        name="final",
    )(slot, x1.reshape(b * s, d), rt, mod, modf, g_final.reshape(1, d), y_rows)
    return out.reshape(b, s, d)


def kernel(x, c, w_ada, b_ada, g_mix, w_in, sinks_a, rel_bias_b, g_out_a, g_out_b, w_out, g_ffn,
           w_router_group, b_router_group, w_router_expert, b_router_expert, w_gate, w_up, w_down,
           w_ada_final, b_ada_final, g_final):
    b, s, d = x.shape
    assert w_ada.shape[0] == 1, "one layer"
    n_tok = b * s

    c_act = jax.nn.silu(c)
    a_rep = jnp.broadcast_to(c_act[:, :, None], (b, d, LANES))
    mod = _ada(a_rep, w_ada[0], b_ada[0]).reshape(b, 6, d)
    modf = _ada(a_rep, w_ada_final, b_ada_final).reshape(b, 2, d)

    proj = _proj(x, mod, g_mix[0], w_in[0].astype(BF16))

    bias_a, nw_a, bias_b, nw_b = _band_tables(rel_bias_b[0])
    kv_a0 = DA_Q // LANES
    o_a = _attention(proj, bias_a, sinks_a[0].astype(F32), n_groups=N_KV_A, ncol=2, nw=nw_a,
                     q_col0=0, k_col0=kv_a0, v_col0=kv_a0 + DA_KV // LANES, kv_share=2, dup=True)
    qb0 = (DA_Q + 2 * DA_KV) // LANES
    o_b = _attention(proj, bias_b, None, n_groups=N_HEADS_B // 2, ncol=1, nw=nw_b,
                     q_col0=qb0, k_col0=qb0 + DB // LANES, v_col0=qb0 + 2 * DB // LANES,
                     kv_share=1, dup=False)

    n_r = N_GROUPS + N_EXPERTS
    w_router = jnp.zeros((d, LANES), F32).at[:, :n_r].set(
        jnp.concatenate([w_router_group[0], w_router_expert[0]], axis=1))
    b_router = jnp.zeros((LANES,), F32).at[:n_r].set(
        jnp.concatenate([b_router_group[0], b_router_expert[0]]))
    x1, h2, rt = _mix(o_a, o_b, x, mod, g_out_a[0], g_out_b[0], g_ffn[0], w_out[0].astype(BF16),
                      w_router, b_router)

    tile_expert, next_expert, n_tiles, src_tok, slot = _route_plan(rt, n_tok)
    y = _moe(h2.reshape(n_tok, d), w_gate[0], w_up[0], w_down[0], tile_expert, next_expert, n_tiles,
             src_tok)

    return _final(x1, y, slot, rt, mod, modf, g_final)
```

```python
import functools

import jax
import jax.numpy as jnp
import numpy as np
from jax import lax
from jax.experimental import pallas as pl
from jax.experimental.pallas import tpu as pltpu

D_MODEL = 2048
CHUNK = 64
HEAD_DIM = 64
N_HEADS_A = 16
N_KV_A = 4
N_PREV_A = 2
N_HEADS_B = 16
N_PREV_B = 8
REL_CLIP = 128
DA_Q = N_HEADS_A * HEAD_DIM
DA_KV = N_KV_A * HEAD_DIM
DB = N_HEADS_B * HEAD_DIM
D_IN = DA_Q + 2 * DA_KV + 3 * DB
N_GROUPS = 4
EXPERTS_PER_GROUP = 8
N_EXPERTS = N_GROUPS * EXPERTS_PER_GROUP
D_EXPERT = D_MODEL // 4
EPS = 1e-6
NEG_INF = -1e30

LANES = 128
QBLK = 2 * CHUNK
VMEM_LIMIT = 56 * 1024 * 1024

F32 = jnp.float32
BF16 = jnp.bfloat16


def _rms(x, g):
    return x * lax.rsqrt(jnp.mean(x * x, axis=-1, keepdims=True) + EPS) * g


def _ada_kernel(a_ref, w_ref, b_ref, o_ref):
    n_b = a_ref.shape[0]
    tn = w_ref.shape[1]
    for j in range(tn // LANES):
        cols = slice(j * LANES, (j + 1) * LANES)
        w = w_ref[:, cols]
        for b in range(n_b):
            o_ref[b:b + 1, cols] = jnp.sum(a_ref[b] * w, axis=0, keepdims=True) + b_ref[:, cols]


def _ada(a_rep, w, bias):
    n_b, k, _ = a_rep.shape
    n = w.shape[1]
    tn = 1024
    return pl.pallas_call(
        _ada_kernel,
        out_shape=jax.ShapeDtypeStruct((n_b, n), F32),
        grid=(n // tn,),
        in_specs=[
            pl.BlockSpec((n_b, k, LANES), lambda j: (0, 0, 0)),
            pl.BlockSpec((k, tn), lambda j: (0, j)),
            pl.BlockSpec((1, tn), lambda j: (0, j)),
        ],
        out_specs=pl.BlockSpec((n_b, tn), lambda j: (0, j)),
        compiler_params=pltpu.CompilerParams(
            dimension_semantics=("arbitrary",), vmem_limit_bytes=VMEM_LIMIT),
        name="ada",
    )(a_rep, w, bias.reshape(1, n))


def _proj_kernel(x_ref, mod_ref, g_ref, w_ref, o_ref, *, n_chunk):
    h = _rms(x_ref[0], g_ref[...])
    h = h * (1.0 + mod_ref[0, 1:2, :]) + mod_ref[0, 0:1, :]
    hb = h.astype(BF16)
    for n0 in range(0, o_ref.shape[2], n_chunk):
        cols = slice(n0, n0 + n_chunk)
        o_ref[0, :, cols] = jnp.dot(hb, w_ref[:, cols], preferred_element_type=F32).astype(BF16)


def _proj(x, mod, g_mix, w_in_bf16):
    b, s, d = x.shape
    n = w_in_bf16.shape[1]
    tm = 512
    return pl.pallas_call(
        functools.partial(_proj_kernel, n_chunk=512),
        out_shape=jax.ShapeDtypeStruct((b, s, n), BF16),
        grid=(b, s // tm),
        in_specs=[
            pl.BlockSpec((1, tm, d), lambda bi, i: (bi, i, 0)),
            pl.BlockSpec((1, 6, d), lambda bi, i: (bi, 0, 0)),
            pl.BlockSpec((1, d), lambda bi, i: (0, 0)),
            pl.BlockSpec((d, n), lambda bi, i: (0, 0), pipeline_mode=pl.Buffered(1)),
        ],
        out_specs=pl.BlockSpec((1, tm, n), lambda bi, i: (bi, i, 0)),
        compiler_params=pltpu.CompilerParams(
            dimension_semantics=("arbitrary", "arbitrary"), vmem_limit_bytes=VMEM_LIMIT),
        name="proj",
    )(x, mod, g_mix.reshape(1, d), w_in_bf16)


def _attn_kernel(*refs, ncol, nw, dup, has_sink):
    if has_sink:
        sink_ref, q_ref, k_ref, v_ref, bias_ref, o_ref = refs[:6]
        scratch = refs[6:]
    else:
        q_ref, k_ref, v_ref, bias_ref, o_ref = refs[:5]
        scratch = refs[5:]
        sink_ref = None
    s_len = q_ref.shape[1]
    n_blk = s_len // QBLK
    n_stack = 2 * ncol
    grp = pl.program_id(1)

    if dup:
        kd_ref, vd_ref = scratch
        half = grp % 2
        rows = 512

        def dup_body(c, carry):
            r0 = pl.multiple_of(c * rows, rows)
            lane_half = lax.broadcasted_iota(jnp.int32, (rows, LANES), 1) // HEAD_DIM
            keep = lane_half == half
            for src, dst in ((k_ref, kd_ref), (v_ref, vd_ref)):
                t = src[0, pl.ds(r0, rows), :].astype(F32)
                dst[pl.ds(r0, rows), :] = jnp.where(keep, t, pltpu.roll(t, HEAD_DIM, 1)).astype(BF16)
            return carry

        lax.fori_loop(0, s_len // rows, dup_body, 0)
        k_src, v_src = kd_ref, vd_ref
    else:
        k_src, v_src = k_ref.at[0], v_ref.at[0]

    lane = lax.broadcasted_iota(jnp.int32, (QBLK, LANES), 1)
    low = lane < HEAD_DIM

    if has_sink:
        sink = jnp.max(jnp.concatenate(
            [jnp.full((QBLK, LANES), sink_ref[grp * n_stack + h], F32) for h in range(n_stack)], axis=0),
            axis=-1, keepdims=True)

    def attend(j, nvb):
        r0 = j * QBLK if isinstance(j, int) else pl.multiple_of(j * QBLK, QBLK)
        k0 = (j - (nvb - 1)) * QBLK
        if not isinstance(j, int):
            k0 = pl.multiple_of(k0, QBLK)
        qf = q_ref[0, pl.ds(r0, QBLK), :].astype(F32) * (HEAD_DIM ** -0.5)
        parts = []
        for c in range(ncol):
            qc = qf[:, c * LANES:(c + 1) * LANES]
            parts.append(jnp.where(low, qc, 0.0))
            parts.append(jnp.where(low, 0.0, qc))
        lhs = jnp.concatenate(parts, axis=0).astype(BF16)
        kw = k_src[pl.ds(k0, nvb * QBLK), :]
        vw = v_src[pl.ds(k0, nvb * QBLK), :]
        s = lax.dot_general(lhs, kw, (((1,), (1,)), ((), ())), preferred_element_type=F32)
        s = s + bias_ref[0, :, (nw - nvb) * QBLK:]
        m = jnp.max(s, axis=-1, keepdims=True)
        if has_sink:
            m = jnp.maximum(m, sink)
        p = jnp.exp(s - m)
        denom = jnp.sum(p, axis=-1, keepdims=True)
        if has_sink:
            denom = denom + jnp.exp(sink - m)
        o = jnp.dot(p.astype(BF16), vw, preferred_element_type=F32) / denom
        for c in range(ncol):
            o0 = o[(2 * c) * QBLK:(2 * c + 1) * QBLK]
            o1 = o[(2 * c + 1) * QBLK:(2 * c + 2) * QBLK]
            o_ref[0, pl.ds(r0, QBLK), c * LANES:(c + 1) * LANES] = jnp.where(low, o0, o1).astype(BF16)

    for j in range(nw - 1):
        attend(j, j + 1)

    first = nw - 1
    if (n_blk - first) % 2:
        attend(first, nw)
        first += 1

    def body(t, carry):
        j = first + 2 * t
        attend(j, nw)
        attend(j + 1, nw)
        return carry

    lax.fori_loop(0, (n_blk - first) // 2, body, 0)


def _attention(proj, bias, sinks, *, n_groups, ncol, nw, q_col0, k_col0, v_col0, kv_share, dup):
    b, s, _ = proj.shape
    qw = ncol * LANES
    has_sink = sinks is not None
    kernel = functools.partial(_attn_kernel, ncol=ncol, nw=nw, dup=dup, has_sink=has_sink)
    n_rows = bias.shape[1]
    in_specs = [
        pl.BlockSpec((1, s, qw), lambda bi, g, *_: (bi, 0, q_col0 // ncol + g)),
        pl.BlockSpec((1, s, LANES), lambda bi, g, *_: (bi, 0, k_col0 + g // kv_share)),
        pl.BlockSpec((1, s, LANES), lambda bi, g, *_: (bi, 0, v_col0 + g // kv_share)),
        pl.BlockSpec((1, n_rows, nw * QBLK), lambda bi, g, *_: (g, 0, 0)),
    ]
    out_spec = pl.BlockSpec((1, s, qw), lambda bi, g, *_: (bi, 0, g))
    scratch = [pltpu.VMEM((s, LANES), BF16), pltpu.VMEM((s, LANES), BF16)] if dup else []
    grid_spec = pltpu.PrefetchScalarGridSpec(
        num_scalar_prefetch=1 if has_sink else 0,
        grid=(b, n_groups),
        in_specs=in_specs,
        out_specs=out_spec,
        scratch_shapes=scratch,
    )
    args = ((sinks,) if has_sink else ()) + (proj, proj, proj, bias)
    return pl.pallas_call(
        kernel,
        out_shape=jax.ShapeDtypeStruct((b, s, n_groups * qw), BF16),
        grid_spec=grid_spec,
        compiler_params=pltpu.CompilerParams(
            dimension_semantics=("arbitrary", "arbitrary"), vmem_limit_bytes=VMEM_LIMIT),
        name="attn_a" if dup else "attn_b",
    )(*args)


def _band_tables(rel_bias_b):
    qi = jnp.arange(QBLK)[:, None]
    r = qi // CHUNK

    def table(nw, n_prev, fn):
        kj = jnp.arange(nw * QBLK)[None, :]
        rel = (nw - 1) * QBLK + qi - kj
        inband = (kj >= r * CHUNK) & (kj < (r + n_prev + 1) * CHUNK)
        return jnp.where(inband[None], fn(rel), NEG_INF)

    slopes = jnp.exp2(-8.0 * jnp.arange(1, N_HEADS_A + 1, dtype=F32) / N_HEADS_A)
    nw_a = (N_PREV_A * CHUNK) // QBLK + 1
    nw_b = (N_PREV_B * CHUNK) // QBLK + 1
    bias_a = table(nw_a, N_PREV_A, lambda rel: -slopes[:, None, None] * jnp.abs(rel).astype(F32)[None])
    w_b = nw_b * QBLK
    p = QBLK + w_b - 1
    m = (np.arange(p) + QBLK - 1) % p - (QBLK - 1)
    dist = np.clip((nw_b - 1) * QBLK - m, -REL_CLIP, REL_CLIP) + REL_CLIP
    vec = rel_bias_b[:, dist].astype(F32)
    reps = -(-(QBLK * (p - 1)) // p)
    rel_b = jnp.tile(vec, (1, reps))[:, :QBLK * (p - 1)].reshape(N_HEADS_B, QBLK, p - 1)[:, :, :w_b]
    bias_b = table(nw_b, N_PREV_B, lambda rel: rel_b)
    rep = N_HEADS_A // N_KV_A
    bias_a = bias_a.reshape(N_KV_A, rep * QBLK, nw_a * QBLK)
    bias_b = bias_b.reshape(N_HEADS_B // 2, 2 * QBLK, nw_b * QBLK)
    return bias_a, nw_a, bias_b, nw_b


def _mix_kernel(oa_ref, ob_ref, x_ref, mod_ref, ga_ref, gb_ref, gf_ref, wo_ref, wr_ref, br_ref,
                x1_ref, h2_ref, rt_ref):
    half = oa_ref.shape[2]
    na = _rms(oa_ref[0].astype(F32), ga_ref[...]).astype(BF16)
    nb = _rms(ob_ref[0].astype(F32), gb_ref[...]).astype(BF16)
    acc = jnp.dot(na, wo_ref[:half, :], preferred_element_type=F32)
    acc = acc + jnp.dot(nb, wo_ref[half:, :], preferred_element_type=F32)
    x1 = x_ref[0] + mod_ref[0, 2:3, :] * acc
    x1_ref[0] = x1
    h2 = _rms(x1, gf_ref[...]) * (1.0 + mod_ref[0, 4:5, :]) + mod_ref[0, 3:4, :]
    h2_ref[0] = h2

    tm = h2.shape[0]
    h_hi = h2.astype(BF16)
    h_lo = (h2 - h_hi.astype(F32)).astype(BF16)
    w = wr_ref[...]
    w_hi = w.astype(BF16)
    w_lo = (w - w_hi.astype(F32)).astype(BF16)
    logits = jnp.dot(h_hi, w_hi, preferred_element_type=F32)
    logits = logits + jnp.dot(h_lo, w_hi, preferred_element_type=F32)
    logits = logits + jnp.dot(h_hi, w_lo, preferred_element_type=F32)
    logits = logits + br_ref[...]

    lane = lax.broadcasted_iota(jnp.int32, (tm, LANES), 1)
    lane_f = lane.astype(F32)
    big = float(LANES)
    ninf = -jnp.inf

    def first_max(vals):
        top = jnp.max(vals, axis=-1, keepdims=True)
        idx = jnp.min(jnp.where(vals == top, lane_f, big), axis=-1, keepdims=True)
        return top, idx

    is_g = lane < N_GROUPS
    g_top, g_idx = first_max(jnp.where(is_g, logits, ninf))
    p_g = 1.0 / jnp.sum(jnp.where(is_g, jnp.exp(logits - g_top), 0.0), axis=-1, keepdims=True)
    lo = N_GROUPS + g_idx * EXPERTS_PER_GROUP
    e_vals = jnp.where((lane_f >= lo) & (lane_f < lo + EXPERTS_PER_GROUP), logits, ninf)
    v1, i1 = first_max(e_vals)
    v2, i2 = first_max(jnp.where(lane_f == i1, ninf, e_vals))
    e2 = jnp.exp(v2 - v1)
    w1 = p_g / (1.0 + e2)
    w2 = p_g * e2 / (1.0 + e2)
    rt = jnp.where(lane == 0, i1 - N_GROUPS,
                   jnp.where(lane == 1, i2 - N_GROUPS,
                             jnp.where(lane == 2, w1, jnp.where(lane == 3, w2, 0.0))))
    rt_ref[...] = rt


def _mix(o_a, o_b, x, mod, g_out_a, g_out_b, g_ffn, w_out_bf16, w_router, b_router):
    b, s, d = x.shape
    half = o_a.shape[2]
    tm = 256
    nt = s // tm
    vec = lambda n: pl.BlockSpec((1, n), lambda bi, i: (0, 0))
    return pl.pallas_call(
        _mix_kernel,
        out_shape=(jax.ShapeDtypeStruct((b, s, d), F32),
                   jax.ShapeDtypeStruct((b, s, d), F32),
                   jax.ShapeDtypeStruct((b * s, LANES), F32)),
        grid=(b, nt),
        in_specs=[
            pl.BlockSpec((1, tm, half), lambda bi, i: (bi, i, 0)),
            pl.BlockSpec((1, tm, half), lambda bi, i: (bi, i, 0)),
            pl.BlockSpec((1, tm, d), lambda bi, i: (bi, i, 0)),
            pl.BlockSpec((1, 6, d), lambda bi, i: (bi, 0, 0)),
            vec(half), vec(half), vec(d),
            pl.BlockSpec((d, d), lambda bi, i: (0, 0), pipeline_mode=pl.Buffered(1)),
            pl.BlockSpec((d, LANES), lambda bi, i: (0, 0)),
            vec(LANES),
        ],
        out_specs=(pl.BlockSpec((1, tm, d), lambda bi, i: (bi, i, 0)),
                   pl.BlockSpec((1, tm, d), lambda bi, i: (bi, i, 0)),
                   pl.BlockSpec((tm, LANES), lambda bi, i: (bi * nt + i, 0))),
        compiler_params=pltpu.CompilerParams(
            dimension_semantics=("arbitrary", "arbitrary"), vmem_limit_bytes=VMEM_LIMIT),
        name="mix_out",
    )(o_a, o_b, x, mod, g_out_a.reshape(1, half), g_out_b.reshape(1, half), g_ffn.reshape(1, d),
      w_out_bf16, w_router, b_router.reshape(1, LANES))


MOE_TM = 256


ROW_BUFS = 3
DISPATCH_TM = 256
DISPATCH_STEPS_MIN = ROW_BUFS


def _dispatch_kernel(slot_ref, zf_ref, h_ref, xs_hbm, buf, zbuf, sem, zsem):
    i = pl.program_id(0)
    n_steps = pl.num_programs(0)
    td = h_ref.shape[0]
    tm = zbuf.shape[0]
    n_tiles_max = xs_hbm.shape[0] // tm

    @pl.when(i == 0)
    def _():
        zbuf[...] = jnp.zeros(zbuf.shape, zbuf.dtype)

        def fill(t, carry):
            @pl.when(zf_ref[t] > 0)
            def _():
                row0 = pl.multiple_of(t * tm, tm)
                pltpu.make_async_copy(zbuf, xs_hbm.at[pl.ds(row0, tm)], zsem.at[0]).start()
            return carry
        lax.fori_loop(0, n_tiles_max, fill, 0)

        def drain(t, carry):
            @pl.when(zf_ref[t] > 0)
            def _():
                pltpu.make_async_copy(zbuf, xs_hbm.at[pl.ds(0, tm)], zsem.at[0]).wait()
            return carry
        lax.fori_loop(0, n_tiles_max, drain, 0)

    def wait_rows(s):
        for _ in range(2):
            pltpu.make_async_copy(buf.at[s], xs_hbm.at[pl.ds(0, td)], sem.at[s]).wait()

    def step(s):
        @pl.when(i >= ROW_BUFS)
        def _():
            wait_rows(s)

        buf[s] = h_ref[...]
        for r in range(td):
            for k in range(2):
                dst = slot_ref[2 * (i * td + r) + k]
                pltpu.make_async_copy(buf.at[s, pl.ds(r, 1)], xs_hbm.at[pl.ds(dst, 1)],
                                      sem.at[s]).start(priority=k)

        @pl.when(i == n_steps - 1)
        def _():
            for t in range(min(ROW_BUFS, DISPATCH_STEPS_MIN)):
                wait_rows((s + ROW_BUFS - t) % ROW_BUFS)

    for s in range(ROW_BUFS):
        pl.when(i % ROW_BUFS == s)(functools.partial(step, s))


def _dispatch(h2, slot, zero_fill, n_pad):
    t, d = h2.shape
    tm = MOE_TM
    assert t // DISPATCH_TM >= DISPATCH_STEPS_MIN
    grid_spec = pltpu.PrefetchScalarGridSpec(
        num_scalar_prefetch=2,
        grid=(t // DISPATCH_TM,),
        in_specs=[pl.BlockSpec((DISPATCH_TM, d), lambda i, *_: (i, 0))],
        out_specs=pl.BlockSpec(memory_space=pl.ANY),
        scratch_shapes=[
            pltpu.VMEM((ROW_BUFS, DISPATCH_TM, d), F32),
            pltpu.VMEM((tm, d), F32),
            pltpu.SemaphoreType.DMA((ROW_BUFS,)),
            pltpu.SemaphoreType.DMA((1,)),
        ],
    )
    return pl.pallas_call(
        _dispatch_kernel,
        out_shape=jax.ShapeDtypeStruct((n_pad, d), F32),
        grid_spec=grid_spec,
        compiler_params=pltpu.CompilerParams(
            dimension_semantics=("arbitrary",), vmem_limit_bytes=VMEM_LIMIT),
        name="dispatch",
    )(slot, zero_fill, h2)


def _moe_kernel(te_ref, nxe_ref, nt_ref, x_ref, wg_hbm, wu_hbm, wd_hbm, o_ref,
                wg32, wu32, wd32, wgb, wub, wdb, wsem):
    i = pl.program_id(0)
    n_tiles = nt_ref[0]

    def weight_copies(e):
        return [pltpu.make_async_copy(src.at[e], dst, wsem.at[0])
                for src, dst in ((wg_hbm, wg32), (wu_hbm, wu32), (wd_hbm, wd32))]

    @pl.when(i == 0)
    def _():
        for cp in weight_copies(te_ref[0]):
            cp.start()

    @pl.when(i < n_tiles)
    def _():
        @pl.when(jnp.logical_or(i == 0, te_ref[i] != te_ref[jnp.maximum(i - 1, 0)]))
        def _():
            for cp in weight_copies(0):
                cp.wait()
            wgb[...] = wg32[...].astype(BF16)
            wub[...] = wu32[...].astype(BF16)
            wdb[...] = wd32[...].astype(BF16)

            @pl.when(nxe_ref[i] >= 0)
            def _():
                for cp in weight_copies(nxe_ref[i]):
                    cp.start()

        xb = x_ref[...].astype(BF16)
        g = jnp.dot(xb, wgb[...], preferred_element_type=F32)
        u = jnp.dot(xb, wub[...], preferred_element_type=F32)
        a = (g * jax.nn.sigmoid(g) * u).astype(BF16)
        o_ref[...] = jnp.dot(a, wdb[...], preferred_element_type=F32)

    @pl.when(i >= n_tiles)
    def _():
        o_ref[...] = jnp.zeros(o_ref.shape, o_ref.dtype)


def _moe(xs, w_gate, w_up, w_down, tile_expert, next_expert, n_tiles):
    n_pad, d = xs.shape
    tm = MOE_TM
    max_tiles = n_pad // tm
    de = w_gate.shape[2]
    grid_spec = pltpu.PrefetchScalarGridSpec(
        num_scalar_prefetch=3,
        grid=(max_tiles,),
        in_specs=[pl.BlockSpec((tm, d), lambda i, te, nxe, nt: (jnp.minimum(i, nt[0] - 1), 0))]
        + [pl.BlockSpec(memory_space=pl.ANY)] * 3,
        out_specs=pl.BlockSpec((tm, d), lambda i, *_: (i, 0)),
        scratch_shapes=[
            pltpu.VMEM((d, de), F32),
            pltpu.VMEM((d, de), F32),
            pltpu.VMEM((de, d), F32),
            pltpu.VMEM((d, de), BF16),
            pltpu.VMEM((d, de), BF16),
            pltpu.VMEM((de, d), BF16),
            pltpu.SemaphoreType.DMA((1,)),
        ],
    )
    return pl.pallas_call(
        _moe_kernel,
        out_shape=jax.ShapeDtypeStruct((n_pad, d), F32),
        grid_spec=grid_spec,
        compiler_params=pltpu.CompilerParams(
            dimension_semantics=("arbitrary",), vmem_limit_bytes=VMEM_LIMIT),
        name="moe",
    )(tile_expert, next_expert, n_tiles, xs, w_gate, w_up, w_down)


def _route_plan(rt, n_tok):
    tm = MOE_TM
    n_asg = 2 * n_tok
    n_pad = n_asg + N_EXPERTS * tm
    max_tiles = n_pad // tm
    e_flat = rt[:, :2].astype(jnp.int32).reshape(-1)
    onehot = (e_flat[:, None] == jnp.arange(N_EXPERTS)[None, :]).astype(jnp.int32)
    csum = jnp.cumsum(onehot, axis=0)
    counts = csum[-1]
    rank = jnp.sum(csum * onehot, axis=1) - 1
    tiles_e = (counts + tm - 1) // tm
    tile_end = jnp.cumsum(tiles_e)
    base = (tile_end - tiles_e) * tm
    slot = jnp.sum(onehot * base[None, :], axis=1) + rank
    n_tiles = tile_end[-1]
    tile_id = jnp.arange(max_tiles)
    te = jnp.sum((tile_id[:, None] >= tile_end[None, :]).astype(jnp.int32), axis=1)
    te_last = jnp.sum(((n_tiles - 1) >= tile_end).astype(jnp.int32))
    tile_expert = jnp.where(tile_id < n_tiles, te, te_last).astype(jnp.int32)
    ex = jnp.arange(N_EXPERTS)
    later = (ex[None, :] > ex[:, None]) & (counts[None, :] > 0)
    nxt_e = jnp.min(jnp.where(later, ex[None, :], N_EXPERTS), axis=1)
    nxt_e = jnp.where(nxt_e < N_EXPERTS, nxt_e, -1)
    sel = (tile_expert[:, None] == ex[None, :]).astype(jnp.int32)
    next_expert = jnp.sum(sel * nxt_e[None, :], axis=1).astype(jnp.int32)
    last_tile = jnp.sum(sel * (tile_end - 1)[None, :], axis=1)
    zero_fill = ((tile_id >= n_tiles) | (tile_id == last_tile)).astype(jnp.int32)
    return (tile_expert, next_expert, n_tiles.reshape(1).astype(jnp.int32), zero_fill,
            slot.astype(jnp.int32), n_pad)


FINAL_TM = 256


def _final_kernel(slot_ref, x1_ref, rt_ref, mod_ref, modf_ref, g_ref, y_hbm, o_ref, ybuf, gsem):
    i = pl.program_id(0)
    n_steps = pl.num_programs(0)
    tm = x1_ref.shape[0]

    def row_in(tile, r, k, s):
        idx = slot_ref[2 * (tile * tm + r) + k]
        pltpu.make_async_copy(y_hbm.at[pl.ds(idx, 1)], ybuf.at[s, pl.ds(k * tm + r, 1)],
                              gsem.at[s]).start(priority=k)

    def wait_in(s):
        pltpu.make_async_copy(y_hbm.at[pl.ds(0, 2 * tm)], ybuf.at[s], gsem.at[s]).wait()

    @pl.when(i == 0)
    def _():
        for t in range(ROW_BUFS - 1):
            tile = jnp.minimum(t, n_steps - 1)

            def row(r, carry):
                row_in(tile, r, 0, t)
                row_in(tile, r, 1, t)
                return carry
            lax.fori_loop(0, tm, row, 0, unroll=8)

    def step(s):
        wait_in(s)
        rt = rt_ref[...]
        y = rt[:, 2:3] * ybuf[s, :tm, :] + rt[:, 3:4] * ybuf[s, tm:, :]
        nxt = jnp.minimum(i + ROW_BUFS - 1, n_steps - 1)
        for r in range(tm):
            row_in(nxt, r, 0, (s + ROW_BUFS - 1) % ROW_BUFS)
            row_in(nxt, r, 1, (s + ROW_BUFS - 1) % ROW_BUFS)
        x2 = x1_ref[...] + mod_ref[0, 5:6, :] * y
        o_ref[...] = _rms(x2, g_ref[...]) * (1.0 + modf_ref[0, 1:2, :]) + modf_ref[0, 0:1, :]

        @pl.when(i == n_steps - 1)
        def _():
            for t in range(1, ROW_BUFS):
                wait_in((s + t) % ROW_BUFS)

    for s in range(ROW_BUFS):
        pl.when(i % ROW_BUFS == s)(functools.partial(step, s))


def _final(x1, y_rows, slot, rt, mod, modf, g_final):
    b, s, d = x1.shape
    tm = FINAL_TM
    nt = s // tm
    grid_spec = pltpu.PrefetchScalarGridSpec(
        num_scalar_prefetch=1,
        grid=(b * nt,),
        in_specs=[
            pl.BlockSpec((tm, d), lambda i, *_: (i, 0)),
            pl.BlockSpec((tm, LANES), lambda i, *_: (i, 0)),
            pl.BlockSpec((1, 6, d), lambda i, *_: (i // nt, 0, 0)),
            pl.BlockSpec((1, 2, d), lambda i, *_: (i // nt, 0, 0)),
            pl.BlockSpec((1, d), lambda i, *_: (0, 0)),
            pl.BlockSpec(memory_space=pl.ANY),
        ],
        out_specs=pl.BlockSpec((tm, d), lambda i, *_: (i, 0)),
        scratch_shapes=[
            pltpu.VMEM((ROW_BUFS, 2 * tm, d), F32),
            pltpu.SemaphoreType.DMA((ROW_BUFS,)),
        ],
    )
    out = pl.pallas_call(
        _final_kernel,
        out_shape=jax.ShapeDtypeStruct((b * s, d), F32),
        grid_spec=grid_spec,
        compiler_params=pltpu.CompilerParams(
            dimension_semantics=("arbitrary",), vmem_limit_bytes=VMEM_LIMIT),
        name="final",
    )(slot, x1.reshape(b * s, d), rt, mod, modf, g_final.reshape(1, d), y_rows)
    return out.reshape(b, s, d)


def kernel(x, c, w_ada, b_ada, g_mix, w_in, sinks_a, rel_bias_b, g_out_a, g_out_b, w_out, g_ffn,
           w_router_group, b_router_group, w_router_expert, b_router_expert, w_gate, w_up, w_down,
           w_ada_final, b_ada_final, g_final):
    b, s, d = x.shape
    assert w_ada.shape[0] == 1, "one layer"
    n_tok = b * s

    c_act = jax.nn.silu(c)
    a_rep = jnp.broadcast_to(c_act[:, :, None], (b, d, LANES))
    mod = _ada(a_rep, w_ada[0], b_ada[0]).reshape(b, 6, d)
    modf = _ada(a_rep, w_ada_final, b_ada_final).reshape(b, 2, d)

    proj = _proj(x, mod, g_mix[0], w_in[0].astype(BF16))

    bias_a, nw_a, bias_b, nw_b = _band_tables(rel_bias_b[0])
    kv_a0 = DA_Q // LANES
    o_a = _attention(proj, bias_a, sinks_a[0].astype(F32), n_groups=N_KV_A, ncol=2, nw=nw_a,
                     q_col0=0, k_col0=kv_a0, v_col0=kv_a0 + DA_KV // LANES, kv_share=2, dup=True)
    qb0 = (DA_Q + 2 * DA_KV) // LANES
    o_b = _attention(proj, bias_b, None, n_groups=N_HEADS_B // 2, ncol=1, nw=nw_b,
                     q_col0=qb0, k_col0=qb0 + DB // LANES, v_col0=qb0 + 2 * DB // LANES,
                     kv_share=1, dup=False)

    n_r = N_GROUPS + N_EXPERTS
    w_router = jnp.zeros((d, LANES), F32).at[:, :n_r].set(
        jnp.concatenate([w_router_group[0], w_router_expert[0]], axis=1))
    b_router = jnp.zeros((LANES,), F32).at[:n_r].set(
        jnp.concatenate([b_router_group[0], b_router_expert[0]]))
    x1, h2, rt = _mix(o_a, o_b, x, mod, g_out_a[0], g_out_b[0], g_ffn[0], w_out[0].astype(BF16),
                      w_router, b_router)

    tile_expert, next_expert, n_tiles, zero_fill, slot, n_pad = _route_plan(rt, n_tok)
    xs = _dispatch(h2.reshape(n_tok, d), slot, zero_fill, n_pad)
    y = _moe(xs, w_gate[0], w_up[0], w_down[0], tile_expert, next_expert, n_tiles)

    return _final(x1, y, slot, rt, mod, modf, g_final)
```

```python
import functools

import jax
import jax.numpy as jnp
import numpy as np
from jax import lax
from jax.experimental import pallas as pl
from jax.experimental.pallas import tpu as pltpu

D_MODEL = 2048
CHUNK = 64
HEAD_DIM = 64
N_HEADS_A = 16
N_KV_A = 4
N_PREV_A = 2
N_HEADS_B = 16
N_PREV_B = 8
REL_CLIP = 128
DA_Q = N_HEADS_A * HEAD_DIM
DA_KV = N_KV_A * HEAD_DIM
DB = N_HEADS_B * HEAD_DIM
D_IN = DA_Q + 2 * DA_KV + 3 * DB
N_GROUPS = 4
EXPERTS_PER_GROUP = 8
N_EXPERTS = N_GROUPS * EXPERTS_PER_GROUP
D_EXPERT = D_MODEL // 4
EPS = 1e-6
NEG_INF = -1e30

LANES = 128
QBLK = 2 * CHUNK
VMEM_LIMIT = 56 * 1024 * 1024

F32 = jnp.float32
BF16 = jnp.bfloat16


def _rms(x, g):
    return x * lax.rsqrt(jnp.mean(x * x, axis=-1, keepdims=True) + EPS) * g


def _ada_kernel(a_ref, w_ref, b_ref, o_ref):
    n_b = a_ref.shape[0]
    tn = w_ref.shape[1]
    for j in range(tn // LANES):
        cols = slice(j * LANES, (j + 1) * LANES)
        w = w_ref[:, cols]
        for b in range(n_b):
            o_ref[b:b + 1, cols] = jnp.sum(a_ref[b] * w, axis=0, keepdims=True) + b_ref[:, cols]


def _ada(a_rep, w, bias):
    n_b, k, _ = a_rep.shape
    n = w.shape[1]
    tn = 1024
    return pl.pallas_call(
        _ada_kernel,
        out_shape=jax.ShapeDtypeStruct((n_b, n), F32),
        grid=(n // tn,),
        in_specs=[
            pl.BlockSpec((n_b, k, LANES), lambda j: (0, 0, 0)),
            pl.BlockSpec((k, tn), lambda j: (0, j)),
            pl.BlockSpec((1, tn), lambda j: (0, j)),
        ],
        out_specs=pl.BlockSpec((n_b, tn), lambda j: (0, j)),
        compiler_params=pltpu.CompilerParams(
            dimension_semantics=("arbitrary",), vmem_limit_bytes=VMEM_LIMIT),
        name="ada",
    )(a_rep, w, bias.reshape(1, n))


def _proj_kernel(x_ref, mod_ref, g_ref, w_ref, o_ref, *, n_chunk):
    h = _rms(x_ref[0], g_ref[...])
    h = h * (1.0 + mod_ref[0, 1:2, :]) + mod_ref[0, 0:1, :]
    hb = h.astype(BF16)
    for n0 in range(0, o_ref.shape[2], n_chunk):
        cols = slice(n0, n0 + n_chunk)
        o_ref[0, :, cols] = jnp.dot(hb, w_ref[:, cols], preferred_element_type=F32).astype(BF16)


def _proj(x, mod, g_mix, w_in_bf16):
    b, s, d = x.shape
    n = w_in_bf16.shape[1]
    tm = 512
    return pl.pallas_call(
        functools.partial(_proj_kernel, n_chunk=512),
        out_shape=jax.ShapeDtypeStruct((b, s, n), BF16),
        grid=(b, s // tm),
        in_specs=[
            pl.BlockSpec((1, tm, d), lambda bi, i: (bi, i, 0)),
            pl.BlockSpec((1, 6, d), lambda bi, i: (bi, 0, 0)),
            pl.BlockSpec((1, d), lambda bi, i: (0, 0)),
            pl.BlockSpec((d, n), lambda bi, i: (0, 0), pipeline_mode=pl.Buffered(1)),
        ],
        out_specs=pl.BlockSpec((1, tm, n), lambda bi, i: (bi, i, 0)),
        compiler_params=pltpu.CompilerParams(
            dimension_semantics=("arbitrary", "arbitrary"), vmem_limit_bytes=VMEM_LIMIT),
        name="proj",
    )(x, mod, g_mix.reshape(1, d), w_in_bf16)


def _attn_kernel(*refs, ncol, nw, dup, has_sink):
    if has_sink:
        sink_ref, q_ref, k_ref, v_ref, bias_ref, o_ref = refs[:6]
        scratch = refs[6:]
    else:
        q_ref, k_ref, v_ref, bias_ref, o_ref = refs[:5]
        scratch = refs[5:]
        sink_ref = None
    s_len = q_ref.shape[1]
    n_blk = s_len // QBLK
    n_stack = 2 * ncol
    grp = pl.program_id(1)

    if dup:
        kd_ref, vd_ref = scratch[:2]
        half = grp % 2
        rows = 512

        def dup_body(c, carry):
            r0 = pl.multiple_of(c * rows, rows)
            lane_half = lax.broadcasted_iota(jnp.int32, (rows, LANES), 1) // HEAD_DIM
            keep = lane_half == half
            for src, dst in ((k_ref, kd_ref), (v_ref, vd_ref)):
                t = src[0, pl.ds(r0, rows), :].astype(F32)
                dst[pl.ds(r0, rows), :] = jnp.where(keep, t, pltpu.roll(t, HEAD_DIM, 1)).astype(BF16)
            return carry

        lax.fori_loop(0, s_len // rows, dup_body, 0)
        k_src, v_src = kd_ref, vd_ref
    else:
        k_src, v_src = k_ref.at[0], v_ref.at[0]

    lane = lax.broadcasted_iota(jnp.int32, (QBLK, LANES), 1)
    low = lane < HEAD_DIM

    if has_sink:
        sink = jnp.max(jnp.concatenate(
            [jnp.full((QBLK, LANES), sink_ref[grp * n_stack + h], F32) for h in range(n_stack)], axis=0),
            axis=-1, keepdims=True)

    def rows_of(j):
        return j * QBLK if isinstance(j, int) else pl.multiple_of(j * QBLK, QBLK)

    def logits(j, nvb):
        qf = q_ref[0, pl.ds(rows_of(j), QBLK), :].astype(F32) * (HEAD_DIM ** -0.5)
        parts = []
        for c in range(ncol):
            qc = qf[:, c * LANES:(c + 1) * LANES]
            parts.append(jnp.where(low, qc, 0.0))
            parts.append(jnp.where(low, 0.0, qc))
        lhs = jnp.concatenate(parts, axis=0).astype(BF16)
        kw = k_src[pl.ds(rows_of(j - (nvb - 1)), nvb * QBLK), :]
        s = lax.dot_general(lhs, kw, (((1,), (1,)), ((), ())), preferred_element_type=F32)
        return s + bias_ref[0, :, (nw - nvb) * QBLK:]

    def softmax(s):
        m = jnp.max(s, axis=-1, keepdims=True)
        if has_sink:
            m = jnp.maximum(m, sink)
        p = jnp.exp(s - m)
        denom = jnp.sum(p, axis=-1, keepdims=True)
        if has_sink:
            denom = denom + jnp.exp(sink - m)
        return p.astype(BF16), denom

    def emit(j, nvb, p, denom):
        vw = v_src[pl.ds(rows_of(j - (nvb - 1)), nvb * QBLK), :]
        o = jnp.dot(p, vw, preferred_element_type=F32) / denom
        for c in range(ncol):
            o0 = o[(2 * c) * QBLK:(2 * c + 1) * QBLK]
            o1 = o[(2 * c + 1) * QBLK:(2 * c + 2) * QBLK]
            o_ref[0, pl.ds(rows_of(j), QBLK), c * LANES:(c + 1) * LANES] = (
                jnp.where(low, o0, o1).astype(BF16))

    first = nw - 1
    for j in range(first + (n_blk - first) % 2):
        nvb = min(j + 1, nw)
        emit(j, nvb, *softmax(logits(j, nvb)))
    first += (n_blk - first) % 2

    s_scr, p_scr = scratch[-2:]
    last = n_blk - 1
    s_scr[0] = logits(first, nw)
    p0, d0 = softmax(s_scr[0])
    p_scr[0] = p0
    s_scr[1] = logits(first + 1, nw)

    def body(t, d_even):
        j = first + 2 * t
        emit(j, nw, p_scr[0], d_even)
        p1, d_odd = softmax(s_scr[1])
        p_scr[1] = p1
        s_scr[0] = logits(jnp.minimum(j + 2, last), nw)
        emit(j + 1, nw, p_scr[1], d_odd)
        p0, d_next = softmax(s_scr[0])
        p_scr[0] = p0
        s_scr[1] = logits(jnp.minimum(j + 3, last), nw)
        return d_next

    lax.fori_loop(0, (n_blk - first) // 2, body, d0)


def _attention(proj, bias, sinks, *, n_groups, ncol, nw, q_col0, k_col0, v_col0, kv_share, dup):
    b, s, _ = proj.shape
    qw = ncol * LANES
    has_sink = sinks is not None
    kernel = functools.partial(_attn_kernel, ncol=ncol, nw=nw, dup=dup, has_sink=has_sink)
    n_rows = bias.shape[1]
    in_specs = [
        pl.BlockSpec((1, s, qw), lambda bi, g, *_: (bi, 0, q_col0 // ncol + g)),
        pl.BlockSpec((1, s, LANES), lambda bi, g, *_: (bi, 0, k_col0 + g // kv_share)),
        pl.BlockSpec((1, s, LANES), lambda bi, g, *_: (bi, 0, v_col0 + g // kv_share)),
        pl.BlockSpec((1, n_rows, nw * QBLK), lambda bi, g, *_: (g, 0, 0)),
    ]
    out_spec = pl.BlockSpec((1, s, qw), lambda bi, g, *_: (bi, 0, g))
    scratch = [pltpu.VMEM((s, LANES), BF16), pltpu.VMEM((s, LANES), BF16)] if dup else []
    scratch += [pltpu.VMEM((2, n_rows, nw * QBLK), F32), pltpu.VMEM((2, n_rows, nw * QBLK), BF16)]
    grid_spec = pltpu.PrefetchScalarGridSpec(
        num_scalar_prefetch=1 if has_sink else 0,
        grid=(b, n_groups),
        in_specs=in_specs,
        out_specs=out_spec,
        scratch_shapes=scratch,
    )
    args = ((sinks,) if has_sink else ()) + (proj, proj, proj, bias)
    return pl.pallas_call(
        kernel,
        out_shape=jax.ShapeDtypeStruct((b, s, n_groups * qw), BF16),
        grid_spec=grid_spec,
        compiler_params=pltpu.CompilerParams(
            dimension_semantics=("arbitrary", "arbitrary"), vmem_limit_bytes=VMEM_LIMIT),
        name="attn_a" if dup else "attn_b",
    )(*args)


def _band_tables(rel_bias_b):
    qi = jnp.arange(QBLK)[:, None]
    r = qi // CHUNK

    def table(nw, n_prev, fn):
        kj = jnp.arange(nw * QBLK)[None, :]
        rel = (nw - 1) * QBLK + qi - kj
        inband = (kj >= r * CHUNK) & (kj < (r + n_prev + 1) * CHUNK)
        return jnp.where(inband[None], fn(rel), NEG_INF)

    slopes = jnp.exp2(-8.0 * jnp.arange(1, N_HEADS_A + 1, dtype=F32) / N_HEADS_A)
    nw_a = (N_PREV_A * CHUNK) // QBLK + 1
    nw_b = (N_PREV_B * CHUNK) // QBLK + 1
    bias_a = table(nw_a, N_PREV_A, lambda rel: -slopes[:, None, None] * jnp.abs(rel).astype(F32)[None])
    w_b = nw_b * QBLK
    p = QBLK + w_b - 1
    m = (np.arange(p) + QBLK - 1) % p - (QBLK - 1)
    dist = np.clip((nw_b - 1) * QBLK - m, -REL_CLIP, REL_CLIP) + REL_CLIP
    vec = rel_bias_b[:, dist].astype(F32)
    reps = -(-(QBLK * (p - 1)) // p)
    rel_b = jnp.tile(vec, (1, reps))[:, :QBLK * (p - 1)].reshape(N_HEADS_B, QBLK, p - 1)[:, :, :w_b]
    bias_b = table(nw_b, N_PREV_B, lambda rel: rel_b)
    rep = N_HEADS_A // N_KV_A
    bias_a = bias_a.reshape(N_KV_A, rep * QBLK, nw_a * QBLK)
    bias_b = bias_b.reshape(N_HEADS_B // 2, 2 * QBLK, nw_b * QBLK)
    return bias_a, nw_a, bias_b, nw_b


def _mix_kernel(oa_ref, ob_ref, x_ref, mod_ref, ga_ref, gb_ref, gf_ref, wo_ref, wr_ref, br_ref,
                x1_ref, h2_ref, rt_ref):
    half = oa_ref.shape[2]
    na = _rms(oa_ref[0].astype(F32), ga_ref[...]).astype(BF16)
    nb = _rms(ob_ref[0].astype(F32), gb_ref[...]).astype(BF16)
    acc = jnp.dot(na, wo_ref[:half, :], preferred_element_type=F32)
    acc = acc + jnp.dot(nb, wo_ref[half:, :], preferred_element_type=F32)
    x1 = x_ref[0] + mod_ref[0, 2:3, :] * acc
    x1_ref[0] = x1
    h2 = _rms(x1, gf_ref[...]) * (1.0 + mod_ref[0, 4:5, :]) + mod_ref[0, 3:4, :]
    h2_ref[0] = h2

    tm = h2.shape[0]
    h_hi = h2.astype(BF16)
    h_lo = (h2 - h_hi.astype(F32)).astype(BF16)
    w = wr_ref[...]
    w_hi = w.astype(BF16)
    w_lo = (w - w_hi.astype(F32)).astype(BF16)
    logits = jnp.dot(h_hi, w_hi, preferred_element_type=F32)
    logits = logits + jnp.dot(h_lo, w_hi, preferred_element_type=F32)
    logits = logits + jnp.dot(h_hi, w_lo, preferred_element_type=F32)
    logits = logits + br_ref[...]

    lane = lax.broadcasted_iota(jnp.int32, (tm, LANES), 1)
    lane_f = lane.astype(F32)
    big = float(LANES)
    ninf = -jnp.inf

    def first_max(vals):
        top = jnp.max(vals, axis=-1, keepdims=True)
        idx = jnp.min(jnp.where(vals == top, lane_f, big), axis=-1, keepdims=True)
        return top, idx

    is_g = lane < N_GROUPS
    g_top, g_idx = first_max(jnp.where(is_g, logits, ninf))
    p_g = 1.0 / jnp.sum(jnp.where(is_g, jnp.exp(logits - g_top), 0.0), axis=-1, keepdims=True)
    lo = N_GROUPS + g_idx * EXPERTS_PER_GROUP
    e_vals = jnp.where((lane_f >= lo) & (lane_f < lo + EXPERTS_PER_GROUP), logits, ninf)
    v1, i1 = first_max(e_vals)
    v2, i2 = first_max(jnp.where(lane_f == i1, ninf, e_vals))
    e2 = jnp.exp(v2 - v1)
    w1 = p_g / (1.0 + e2)
    w2 = p_g * e2 / (1.0 + e2)
    rt = jnp.where(lane == 0, i1 - N_GROUPS,
                   jnp.where(lane == 1, i2 - N_GROUPS,
                             jnp.where(lane == 2, w1, jnp.where(lane == 3, w2, 0.0))))
    rt_ref[...] = rt


def _mix(o_a, o_b, x, mod, g_out_a, g_out_b, g_ffn, w_out_bf16, w_router, b_router):
    b, s, d = x.shape
    half = o_a.shape[2]
    tm = 256
    nt = s // tm
    vec = lambda n: pl.BlockSpec((1, n), lambda bi, i: (0, 0))
    return pl.pallas_call(
        _mix_kernel,
        out_shape=(jax.ShapeDtypeStruct((b, s, d), F32),
                   jax.ShapeDtypeStruct((b, s, d), F32),
                   jax.ShapeDtypeStruct((b * s, LANES), F32)),
        grid=(b, nt),
        in_specs=[
            pl.BlockSpec((1, tm, half), lambda bi, i: (bi, i, 0)),
            pl.BlockSpec((1, tm, half), lambda bi, i: (bi, i, 0)),
            pl.BlockSpec((1, tm, d), lambda bi, i: (bi, i, 0)),
            pl.BlockSpec((1, 6, d), lambda bi, i: (bi, 0, 0)),
            vec(half), vec(half), vec(d),
            pl.BlockSpec((d, d), lambda bi, i: (0, 0), pipeline_mode=pl.Buffered(1)),
            pl.BlockSpec((d, LANES), lambda bi, i: (0, 0)),
            vec(LANES),
        ],
        out_specs=(pl.BlockSpec((1, tm, d), lambda bi, i: (bi, i, 0)),
                   pl.BlockSpec((1, tm, d), lambda bi, i: (bi, i, 0)),
                   pl.BlockSpec((tm, LANES), lambda bi, i: (bi * nt + i, 0))),
        compiler_params=pltpu.CompilerParams(
            dimension_semantics=("arbitrary", "arbitrary"), vmem_limit_bytes=VMEM_LIMIT),
        name="mix_out",
    )(o_a, o_b, x, mod, g_out_a.reshape(1, half), g_out_b.reshape(1, half), g_ffn.reshape(1, d),
      w_out_bf16, w_router, b_router.reshape(1, LANES))


MOE_TM = 256


ROW_BUFS = 3
DISPATCH_TM = 256
DISPATCH_STEPS_MIN = ROW_BUFS


def _dispatch_kernel(slot_ref, zf_ref, h_ref, xs_hbm, buf, zbuf, sem, zsem):
    i = pl.program_id(0)
    n_steps = pl.num_programs(0)
    td = h_ref.shape[0]
    tm = zbuf.shape[0]
    n_tiles_max = xs_hbm.shape[0] // tm

    @pl.when(i == 0)
    def _():
        zbuf[...] = jnp.zeros(zbuf.shape, zbuf.dtype)

        def fill(t, carry):
            @pl.when(zf_ref[t] > 0)
            def _():
                row0 = pl.multiple_of(t * tm, tm)
                pltpu.make_async_copy(zbuf, xs_hbm.at[pl.ds(row0, tm)], zsem.at[0]).start()
            return carry
        lax.fori_loop(0, n_tiles_max, fill, 0)

        def drain(t, carry):
            @pl.when(zf_ref[t] > 0)
            def _():
                pltpu.make_async_copy(zbuf, xs_hbm.at[pl.ds(0, tm)], zsem.at[0]).wait()
            return carry
        lax.fori_loop(0, n_tiles_max, drain, 0)

    def wait_rows(s):
        for _ in range(2):
            pltpu.make_async_copy(buf.at[s], xs_hbm.at[pl.ds(0, td)], sem.at[s]).wait()

    def step(s):
        @pl.when(i >= ROW_BUFS)
        def _():
            wait_rows(s)

        buf[s] = h_ref[...]
        for r in range(td):
            for k in range(2):
                dst = slot_ref[2 * (i * td + r) + k]
                pltpu.make_async_copy(buf.at[s, pl.ds(r, 1)], xs_hbm.at[pl.ds(dst, 1)],
                                      sem.at[s]).start(priority=k)

        @pl.when(i == n_steps - 1)
        def _():
            for t in range(min(ROW_BUFS, DISPATCH_STEPS_MIN)):
                wait_rows((s + ROW_BUFS - t) % ROW_BUFS)

    for s in range(ROW_BUFS):
        pl.when(i % ROW_BUFS == s)(functools.partial(step, s))


def _dispatch(h2, slot, zero_fill, n_pad):
    t, d = h2.shape
    tm = MOE_TM
    assert t // DISPATCH_TM >= DISPATCH_STEPS_MIN
    grid_spec = pltpu.PrefetchScalarGridSpec(
        num_scalar_prefetch=2,
        grid=(t // DISPATCH_TM,),
        in_specs=[pl.BlockSpec((DISPATCH_TM, d), lambda i, *_: (i, 0))],
        out_specs=pl.BlockSpec(memory_space=pl.ANY),
        scratch_shapes=[
            pltpu.VMEM((ROW_BUFS, DISPATCH_TM, d), F32),
            pltpu.VMEM((tm, d), F32),
            pltpu.SemaphoreType.DMA((ROW_BUFS,)),
            pltpu.SemaphoreType.DMA((1,)),
        ],
    )
    return pl.pallas_call(
        _dispatch_kernel,
        out_shape=jax.ShapeDtypeStruct((n_pad, d), F32),
        grid_spec=grid_spec,
        compiler_params=pltpu.CompilerParams(
            dimension_semantics=("arbitrary",), vmem_limit_bytes=VMEM_LIMIT),
        name="dispatch",
    )(slot, zero_fill, h2)


def _moe_kernel(te_ref, nxe_ref, nt_ref, x_ref, wg_hbm, wu_hbm, wd_hbm, o_ref,
                wg32, wu32, wd32, wgb, wub, wdb, wsem):
    i = pl.program_id(0)
    n_tiles = nt_ref[0]

    def weight_copies(e):
        return [pltpu.make_async_copy(src.at[e], dst, wsem.at[0])
                for src, dst in ((wg_hbm, wg32), (wu_hbm, wu32), (wd_hbm, wd32))]

    @pl.when(i == 0)
    def _():
        for cp in weight_copies(te_ref[0]):
            cp.start()

    @pl.when(i < n_tiles)
    def _():
        @pl.when(jnp.logical_or(i == 0, te_ref[i] != te_ref[jnp.maximum(i - 1, 0)]))
        def _():
            for cp in weight_copies(0):
                cp.wait()
            wgb[...] = wg32[...].astype(BF16)
            wub[...] = wu32[...].astype(BF16)
            wdb[...] = wd32[...].astype(BF16)

            @pl.when(nxe_ref[i] >= 0)
            def _():
                for cp in weight_copies(nxe_ref[i]):
                    cp.start()

        xb = x_ref[...].astype(BF16)
        g = jnp.dot(xb, wgb[...], preferred_element_type=F32)
        u = jnp.dot(xb, wub[...], preferred_element_type=F32)
        a = (g * jax.nn.sigmoid(g) * u).astype(BF16)
        o_ref[...] = jnp.dot(a, wdb[...], preferred_element_type=F32)

    @pl.when(i >= n_tiles)
    def _():
        o_ref[...] = jnp.zeros(o_ref.shape, o_ref.dtype)


def _moe(xs, w_gate, w_up, w_down, tile_expert, next_expert, n_tiles):
    n_pad, d = xs.shape
    tm = MOE_TM
    max_tiles = n_pad // tm
    de = w_gate.shape[2]
    grid_spec = pltpu.PrefetchScalarGridSpec(
        num_scalar_prefetch=3,
        grid=(max_tiles,),
        in_specs=[pl.BlockSpec((tm, d), lambda i, te, nxe, nt: (jnp.minimum(i, nt[0] - 1), 0))]
        + [pl.BlockSpec(memory_space=pl.ANY)] * 3,
        out_specs=pl.BlockSpec((tm, d), lambda i, *_: (i, 0)),
        scratch_shapes=[
            pltpu.VMEM((d, de), F32),
            pltpu.VMEM((d, de), F32),
            pltpu.VMEM((de, d), F32),
            pltpu.VMEM((d, de), BF16),
            pltpu.VMEM((d, de), BF16),
            pltpu.VMEM((de, d), BF16),
            pltpu.SemaphoreType.DMA((1,)),
        ],
    )
    return pl.pallas_call(
        _moe_kernel,
        out_shape=jax.ShapeDtypeStruct((n_pad, d), F32),
        grid_spec=grid_spec,
        compiler_params=pltpu.CompilerParams(
            dimension_semantics=("arbitrary",), vmem_limit_bytes=VMEM_LIMIT),
        name="moe",
    )(tile_expert, next_expert, n_tiles, xs, w_gate, w_up, w_down)


def _route_plan(rt, n_tok):
    tm = MOE_TM
    n_asg = 2 * n_tok
    n_pad = n_asg + N_EXPERTS * tm
    max_tiles = n_pad // tm
    e_flat = rt[:, :2].astype(jnp.int32).reshape(-1)
    onehot = (e_flat[:, None] == jnp.arange(N_EXPERTS)[None, :]).astype(jnp.int32)
    csum = jnp.cumsum(onehot, axis=0)
    counts = csum[-1]
    rank = jnp.sum(csum * onehot, axis=1) - 1
    tiles_e = (counts + tm - 1) // tm
    tile_end = jnp.cumsum(tiles_e)
    base = (tile_end - tiles_e) * tm
    slot = jnp.sum(onehot * base[None, :], axis=1) + rank
    n_tiles = tile_end[-1]
    tile_id = jnp.arange(max_tiles)
    te = jnp.sum((tile_id[:, None] >= tile_end[None, :]).astype(jnp.int32), axis=1)
    te_last = jnp.sum(((n_tiles - 1) >= tile_end).astype(jnp.int32))
    tile_expert = jnp.where(tile_id < n_tiles, te, te_last).astype(jnp.int32)
    ex = jnp.arange(N_EXPERTS)
    later = (ex[None, :] > ex[:, None]) & (counts[None, :] > 0)
    nxt_e = jnp.min(jnp.where(later, ex[None, :], N_EXPERTS), axis=1)
    nxt_e = jnp.where(nxt_e < N_EXPERTS, nxt_e, -1)
    sel = (tile_expert[:, None] == ex[None, :]).astype(jnp.int32)
    next_expert = jnp.sum(sel * nxt_e[None, :], axis=1).astype(jnp.int32)
    last_tile = jnp.sum(sel * (tile_end - 1)[None, :], axis=1)
    zero_fill = ((tile_id >= n_tiles) | (tile_id == last_tile)).astype(jnp.int32)
    return (tile_expert, next_expert, n_tiles.reshape(1).astype(jnp.int32), zero_fill,
            slot.astype(jnp.int32), n_pad)


FINAL_TM = 256


def _final_kernel(slot_ref, x1_ref, rt_ref, mod_ref, modf_ref, g_ref, y_hbm, o_ref, ybuf, gsem):
    i = pl.program_id(0)
    n_steps = pl.num_programs(0)
    tm = x1_ref.shape[0]

    def row_in(tile, r, k, s):
        idx = slot_ref[2 * (tile * tm + r) + k]
        pltpu.make_async_copy(y_hbm.at[pl.ds(idx, 1)], ybuf.at[s, pl.ds(k * tm + r, 1)],
                              gsem.at[s]).start(priority=k)

    def wait_in(s):
        pltpu.make_async_copy(y_hbm.at[pl.ds(0, 2 * tm)], ybuf.at[s], gsem.at[s]).wait()

    @pl.when(i == 0)
    def _():
        for t in range(ROW_BUFS - 1):
            tile = jnp.minimum(t, n_steps - 1)

            def row(r, carry):
                row_in(tile, r, 0, t)
                row_in(tile, r, 1, t)
                return carry
            lax.fori_loop(0, tm, row, 0, unroll=8)

    def step(s):
        wait_in(s)
        rt = rt_ref[...]
        y = rt[:, 2:3] * ybuf[s, :tm, :] + rt[:, 3:4] * ybuf[s, tm:, :]
        nxt = jnp.minimum(i + ROW_BUFS - 1, n_steps - 1)
        for r in range(tm):
            row_in(nxt, r, 0, (s + ROW_BUFS - 1) % ROW_BUFS)
            row_in(nxt, r, 1, (s + ROW_BUFS - 1) % ROW_BUFS)
        x2 = x1_ref[...] + mod_ref[0, 5:6, :] * y
        o_ref[...] = _rms(x2, g_ref[...]) * (1.0 + modf_ref[0, 1:2, :]) + modf_ref[0, 0:1, :]

        @pl.when(i == n_steps - 1)
        def _():
            for t in range(1, ROW_BUFS):
                wait_in((s + t) % ROW_BUFS)

    for s in range(ROW_BUFS):
        pl.when(i % ROW_BUFS == s)(functools.partial(step, s))


def _final(x1, y_rows, slot, rt, mod, modf, g_final):
    b, s, d = x1.shape
    tm = FINAL_TM
    nt = s // tm
    grid_spec = pltpu.PrefetchScalarGridSpec(
        num_scalar_prefetch=1,
        grid=(b * nt,),
        in_specs=[
            pl.BlockSpec((tm, d), lambda i, *_: (i, 0)),
            pl.BlockSpec((tm, LANES), lambda i, *_: (i, 0)),
            pl.BlockSpec((1, 6, d), lambda i, *_: (i // nt, 0, 0)),
            pl.BlockSpec((1, 2, d), lambda i, *_: (i // nt, 0, 0)),
            pl.BlockSpec((1, d), lambda i, *_: (0, 0)),
            pl.BlockSpec(memory_space=pl.ANY),
        ],
        out_specs=pl.BlockSpec((tm, d), lambda i, *_: (i, 0)),
        scratch_shapes=[
            pltpu.VMEM((ROW_BUFS, 2 * tm, d), F32),
            pltpu.SemaphoreType.DMA((ROW_BUFS,)),
        ],
    )
    out = pl.pallas_call(
        _final_kernel,
        out_shape=jax.ShapeDtypeStruct((b * s, d), F32),
        grid_spec=grid_spec,
        compiler_params=pltpu.CompilerParams(
            dimension_semantics=("arbitrary",), vmem_limit_bytes=VMEM_LIMIT),
        name="final",
    )(slot, x1.reshape(b * s, d), rt, mod, modf, g_final.reshape(1, d), y_rows)
    return out.reshape(b, s, d)


def kernel(x, c, w_ada, b_ada, g_mix, w_in, sinks_a, rel_bias_b, g_out_a, g_out_b, w_out, g_ffn,
           w_router_group, b_router_group, w_router_expert, b_router_expert, w_gate, w_up, w_down,
           w_ada_final, b_ada_final, g_final):
    b, s, d = x.shape
    assert w_ada.shape[0] == 1, "one layer"
    n_tok = b * s

    c_act = jax.nn.silu(c)
    a_rep = jnp.broadcast_to(c_act[:, :, None], (b, d, LANES))
    mod = _ada(a_rep, w_ada[0], b_ada[0]).reshape(b, 6, d)
    modf = _ada(a_rep, w_ada_final, b_ada_final).reshape(b, 2, d)

    proj = _proj(x, mod, g_mix[0], w_in[0].astype(BF16))

    bias_a, nw_a, bias_b, nw_b = _band_tables(rel_bias_b[0])
    kv_a0 = DA_Q // LANES
    o_a = _attention(proj, bias_a, sinks_a[0].astype(F32), n_groups=N_KV_A, ncol=2, nw=nw_a,
                     q_col0=0, k_col0=kv_a0, v_col0=kv_a0 + DA_KV // LANES, kv_share=2, dup=True)
    qb0 = (DA_Q + 2 * DA_KV) // LANES
    o_b = _attention(proj, bias_b, None, n_groups=N_HEADS_B // 2, ncol=1, nw=nw_b,
                     q_col0=qb0, k_col0=qb0 + DB // LANES, v_col0=qb0 + 2 * DB // LANES,
                     kv_share=1, dup=False)

    n_r = N_GROUPS + N_EXPERTS
    w_router = jnp.zeros((d, LANES), F32).at[:, :n_r].set(
        jnp.concatenate([w_router_group[0], w_router_expert[0]], axis=1))
    b_router = jnp.zeros((LANES,), F32).at[:n_r].set(
        jnp.concatenate([b_router_group[0], b_router_expert[0]]))
    x1, h2, rt = _mix(o_a, o_b, x, mod, g_out_a[0], g_out_b[0], g_ffn[0], w_out[0].astype(BF16),
                      w_router, b_router)

    tile_expert, next_expert, n_tiles, zero_fill, slot, n_pad = _route_plan(rt, n_tok)
    xs = _dispatch(h2.reshape(n_tok, d), slot, zero_fill, n_pad)
    y = _moe(xs, w_gate[0], w_up[0], w_down[0], tile_expert, next_expert, n_tiles)

    return _final(x1, y, slot, rt, mod, modf, g_final)
```

```python
import functools

import jax
import jax.numpy as jnp
import numpy as np
from jax import lax
from jax.experimental import pallas as pl
from jax.experimental.pallas import tpu as pltpu

D_MODEL = 2048
CHUNK = 64
HEAD_DIM = 64
N_HEADS_A = 16
N_KV_A = 4
N_PREV_A = 2
N_HEADS_B = 16
N_PREV_B = 8
REL_CLIP = 128
DA_Q = N_HEADS_A * HEAD_DIM
DA_KV = N_KV_A * HEAD_DIM
DB = N_HEADS_B * HEAD_DIM
D_IN = DA_Q + 2 * DA_KV + 3 * DB
N_GROUPS = 4
EXPERTS_PER_GROUP = 8
N_EXPERTS = N_GROUPS * EXPERTS_PER_GROUP
D_EXPERT = D_MODEL // 4
EPS = 1e-6
NEG_INF = -1e30
LOG2E = 1.4426950408889634

LANES = 128
ROUTER_LO_LANE = 64
QBLK = 2 * CHUNK
VMEM_LIMIT = 56 * 1024 * 1024

F32 = jnp.float32
BF16 = jnp.bfloat16


def _rms(x, g):
    return x * lax.rsqrt(jnp.mean(x * x, axis=-1, keepdims=True) + EPS) * g


ADA_JB = 8


def _ada_kernel(a_ref, w_ref, b_ref, o_ref):
    n_b, k, _ = a_ref.shape
    tn = w_ref.shape[1]
    sub = 8
    for jb in range(tn // (ADA_JB * LANES)):
        col0 = jb * ADA_JB * LANES

        def body(kc, accs):
            k0 = pl.multiple_of(kc * sub, sub)
            a_rows = [a_ref[b, pl.ds(k0, sub), :] for b in range(n_b)]
            out = []
            for j in range(ADA_JB):
                w = w_ref[pl.ds(k0, sub), col0 + j * LANES:col0 + (j + 1) * LANES]
                out.append([accs[j][b] + a_rows[b] * w for b in range(n_b)])
            return out

        zero = jnp.zeros((sub, LANES), F32)
        accs = lax.fori_loop(0, k // sub, body, [[zero] * n_b for _ in range(ADA_JB)], unroll=4)
        for j in range(ADA_JB):
            cols = slice(col0 + j * LANES, col0 + (j + 1) * LANES)
            for b in range(n_b):
                o_ref[b:b + 1, cols] = jnp.sum(accs[j][b], axis=0, keepdims=True) + b_ref[:, cols]


def _ada(a_rep, w, bias):
    n_b, k, _ = a_rep.shape
    n = w.shape[1]
    tn = 2048
    return pl.pallas_call(
        _ada_kernel,
        out_shape=jax.ShapeDtypeStruct((n_b, n), F32),
        grid=(n // tn,),
        in_specs=[
            pl.BlockSpec((n_b, k, LANES), lambda j: (0, 0, 0)),
            pl.BlockSpec((k, tn), lambda j: (0, j)),
            pl.BlockSpec((1, tn), lambda j: (0, j)),
        ],
        out_specs=pl.BlockSpec((n_b, tn), lambda j: (0, j)),
        compiler_params=pltpu.CompilerParams(
            dimension_semantics=("arbitrary",), vmem_limit_bytes=VMEM_LIMIT),
        name="ada",
    )(a_rep, w, bias.reshape(1, n))


def _proj_kernel(x_ref, mod_ref, g_ref, w_ref, o_ref, *, n_chunk):
    h = _rms(x_ref[0], g_ref[...])
    h = h * (1.0 + mod_ref[0, 1:2, :]) + mod_ref[0, 0:1, :]
    hb = h.astype(BF16)
    for n0 in range(0, o_ref.shape[2], n_chunk):
        cols = slice(n0, n0 + n_chunk)
        o_ref[0, :, cols] = jnp.dot(hb, w_ref[:, cols], preferred_element_type=F32).astype(BF16)


def _proj(x, mod, g_mix, w_in_bf16):
    b, s, d = x.shape
    n = w_in_bf16.shape[1]
    tm = 512
    return pl.pallas_call(
        functools.partial(_proj_kernel, n_chunk=512),
        out_shape=jax.ShapeDtypeStruct((b, s, n), BF16),
        grid=(b, s // tm),
        in_specs=[
            pl.BlockSpec((1, tm, d), lambda bi, i: (bi, i, 0)),
            pl.BlockSpec((1, 6, d), lambda bi, i: (bi, 0, 0)),
            pl.BlockSpec((1, d), lambda bi, i: (0, 0)),
            pl.BlockSpec((d, n), lambda bi, i: (0, 0), pipeline_mode=pl.Buffered(1)),
        ],
        out_specs=pl.BlockSpec((1, tm, n), lambda bi, i: (bi, i, 0)),
        compiler_params=pltpu.CompilerParams(
            dimension_semantics=("arbitrary", "arbitrary"), vmem_limit_bytes=VMEM_LIMIT),
        name="proj",
    )(x, mod, g_mix.reshape(1, d), w_in_bf16)


def _attn_kernel(*refs, ncol, nw, dup, has_sink):
    if has_sink:
        sink_ref, q_ref, k_ref, v_ref, bias_ref, o_ref = refs[:6]
        scratch = refs[6:]
    else:
        q_ref, k_ref, v_ref, bias_ref, o_ref = refs[:5]
        scratch = refs[5:]
        sink_ref = None
    s_len = q_ref.shape[1]
    n_blk = s_len // QBLK
    n_stack = 2 * ncol
    grp = pl.program_id(1)

    if dup:
        kd_ref, vd_ref = scratch[:2]
        half = grp % 2
        rows = 512

        def dup_body(c, carry):
            r0 = pl.multiple_of(c * rows, rows)
            lane_half = lax.broadcasted_iota(jnp.int32, (rows, LANES), 1) // HEAD_DIM
            keep = lane_half == half
            for src, dst in ((k_ref, kd_ref), (v_ref, vd_ref)):
                t = src[0, pl.ds(r0, rows), :].astype(F32)
                dst[pl.ds(r0, rows), :] = jnp.where(keep, t, pltpu.roll(t, HEAD_DIM, 1)).astype(BF16)
            return carry

        lax.fori_loop(0, s_len // rows, dup_body, 0)
        k_src, v_src = kd_ref, vd_ref
    else:
        k_src, v_src = k_ref.at[0], v_ref.at[0]

    lane = lax.broadcasted_iota(jnp.int32, (QBLK, LANES), 1)
    low = lane < HEAD_DIM

    if has_sink:
        sink = jnp.max(jnp.concatenate(
            [jnp.full((QBLK, LANES), sink_ref[grp * n_stack + h] * LOG2E, F32) for h in range(n_stack)],
            axis=0),
            axis=-1, keepdims=True)

    def rows_of(j):
        return j * QBLK if isinstance(j, int) else pl.multiple_of(j * QBLK, QBLK)

    def logits(j, nvb):
        qf = q_ref[0, pl.ds(rows_of(j), QBLK), :].astype(F32) * (HEAD_DIM ** -0.5 * LOG2E)
        parts = []
        for c in range(ncol):
            qc = qf[:, c * LANES:(c + 1) * LANES]
            parts.append(jnp.where(low, qc, 0.0))
            parts.append(jnp.where(low, 0.0, qc))
        lhs = jnp.concatenate(parts, axis=0).astype(BF16)
        kw = k_src[pl.ds(rows_of(j - (nvb - 1)), nvb * QBLK), :]
        s = lax.dot_general(lhs, kw, (((1,), (1,)), ((), ())), preferred_element_type=F32)
        return s + bias_ref[0, :, (nw - nvb) * QBLK:]

    def softmax(s):
        m = jnp.max(s, axis=-1, keepdims=True)
        if has_sink:
            m = jnp.maximum(m, sink)
        p = jnp.exp2(s - m)
        denom = jnp.sum(p, axis=-1, keepdims=True)
        if has_sink:
            denom = denom + jnp.exp2(sink - m)
        return p.astype(BF16), denom

    def emit(j, nvb, p, denom):
        vw = v_src[pl.ds(rows_of(j - (nvb - 1)), nvb * QBLK), :]
        o = jnp.dot(p, vw, preferred_element_type=F32) / denom
        for c in range(ncol):
            o0 = o[(2 * c) * QBLK:(2 * c + 1) * QBLK]
            o1 = o[(2 * c + 1) * QBLK:(2 * c + 2) * QBLK]
            o_ref[0, pl.ds(rows_of(j), QBLK), c * LANES:(c + 1) * LANES] = (
                jnp.where(low, o0, o1).astype(BF16))

    first = nw - 1
    for j in range(first + (n_blk - first) % 2):
        nvb = min(j + 1, nw)
        emit(j, nvb, *softmax(logits(j, nvb)))
    first += (n_blk - first) % 2

    s_scr, p_scr = scratch[-2:]
    last = n_blk - 1
    s_scr[0] = logits(first, nw)
    p0, d0 = softmax(s_scr[0])
    p_scr[0] = p0
    s_scr[1] = logits(first + 1, nw)

    def body(t, d_even):
        j = first + 2 * t
        emit(j, nw, p_scr[0], d_even)
        p1, d_odd = softmax(s_scr[1])
        p_scr[1] = p1
        s_scr[0] = logits(jnp.minimum(j + 2, last), nw)
        emit(j + 1, nw, p_scr[1], d_odd)
        p0, d_next = softmax(s_scr[0])
        p_scr[0] = p0
        s_scr[1] = logits(jnp.minimum(j + 3, last), nw)
        return d_next

    lax.fori_loop(0, (n_blk - first) // 2, body, d0)


def _attention(proj, bias, sinks, *, n_groups, ncol, nw, q_col0, k_col0, v_col0, kv_share, dup):
    b, s, _ = proj.shape
    qw = ncol * LANES
    has_sink = sinks is not None
    kernel = functools.partial(_attn_kernel, ncol=ncol, nw=nw, dup=dup, has_sink=has_sink)
    n_rows = bias.shape[1]
    in_specs = [
        pl.BlockSpec((1, s, qw), lambda bi, g, *_: (bi, 0, q_col0 // ncol + g)),
        pl.BlockSpec((1, s, LANES), lambda bi, g, *_: (bi, 0, k_col0 + g // kv_share)),
        pl.BlockSpec((1, s, LANES), lambda bi, g, *_: (bi, 0, v_col0 + g // kv_share)),
        pl.BlockSpec((1, n_rows, nw * QBLK), lambda bi, g, *_: (g, 0, 0)),
    ]
    out_spec = pl.BlockSpec((1, s, qw), lambda bi, g, *_: (bi, 0, g))
    scratch = [pltpu.VMEM((s, LANES), BF16), pltpu.VMEM((s, LANES), BF16)] if dup else []
    scratch += [pltpu.VMEM((2, n_rows, nw * QBLK), F32), pltpu.VMEM((2, n_rows, nw * QBLK), BF16)]
    grid_spec = pltpu.PrefetchScalarGridSpec(
        num_scalar_prefetch=1 if has_sink else 0,
        grid=(b, n_groups),
        in_specs=in_specs,
        out_specs=out_spec,
        scratch_shapes=scratch,
    )
    args = ((sinks,) if has_sink else ()) + (proj, proj, proj, bias)
    return pl.pallas_call(
        kernel,
        out_shape=jax.ShapeDtypeStruct((b, s, n_groups * qw), BF16),
        grid_spec=grid_spec,
        compiler_params=pltpu.CompilerParams(
            dimension_semantics=("arbitrary", "arbitrary"), vmem_limit_bytes=VMEM_LIMIT),
        name="attn_a" if dup else "attn_b",
    )(*args)


def _band_tables(rel_bias_b):
    qi = jnp.arange(QBLK)[:, None]
    r = qi // CHUNK

    def table(nw, n_prev, fn):
        kj = jnp.arange(nw * QBLK)[None, :]
        rel = (nw - 1) * QBLK + qi - kj
        inband = (kj >= r * CHUNK) & (kj < (r + n_prev + 1) * CHUNK)
        return jnp.where(inband[None], fn(rel), NEG_INF)

    slopes = jnp.exp2(-8.0 * jnp.arange(1, N_HEADS_A + 1, dtype=F32) / N_HEADS_A)
    nw_a = (N_PREV_A * CHUNK) // QBLK + 1
    nw_b = (N_PREV_B * CHUNK) // QBLK + 1
    bias_a = table(nw_a, N_PREV_A, lambda rel: -slopes[:, None, None] * jnp.abs(rel).astype(F32)[None])
    w_b = nw_b * QBLK
    p = QBLK + w_b - 1
    m = (np.arange(p) + QBLK - 1) % p - (QBLK - 1)
    dist = np.clip((nw_b - 1) * QBLK - m, -REL_CLIP, REL_CLIP) + REL_CLIP
    vec = rel_bias_b[:, dist].astype(F32)
    reps = -(-(QBLK * (p - 1)) // p)
    rel_b = jnp.tile(vec, (1, reps))[:, :QBLK * (p - 1)].reshape(N_HEADS_B, QBLK, p - 1)[:, :, :w_b]
    bias_b = table(nw_b, N_PREV_B, lambda rel: rel_b)
    rep = N_HEADS_A // N_KV_A
    bias_a = bias_a.reshape(N_KV_A, rep * QBLK, nw_a * QBLK)
    bias_b = bias_b.reshape(N_HEADS_B // 2, 2 * QBLK, nw_b * QBLK)
    return bias_a * LOG2E, nw_a, bias_b * LOG2E, nw_b


def _mix_kernel(oa_ref, ob_ref, x_ref, mod_ref, ga_ref, gb_ref, gf_ref, wo_ref, wr_ref, br_ref,
                x1_ref, h2_ref, rt_ref):
    half = oa_ref.shape[2]
    na = _rms(oa_ref[0].astype(F32), ga_ref[...]).astype(BF16)
    nb = _rms(ob_ref[0].astype(F32), gb_ref[...]).astype(BF16)
    acc = jnp.dot(na, wo_ref[:half, :], preferred_element_type=F32)
    acc = acc + jnp.dot(nb, wo_ref[half:, :], preferred_element_type=F32)
    x1 = x_ref[0] + mod_ref[0, 2:3, :] * acc
    x1_ref[0] = x1
    h2 = _rms(x1, gf_ref[...]) * (1.0 + mod_ref[0, 4:5, :]) + mod_ref[0, 3:4, :]
    h2_ref[0] = h2

    tm = h2.shape[0]
    h_hi = h2.astype(BF16)
    h_lo = (h2 - h_hi.astype(F32)).astype(BF16)
    r = jnp.dot(jnp.concatenate([h_hi, h_lo], axis=0), wr_ref[...], preferred_element_type=F32)
    r = r[:tm] + r[tm:]
    logits = r + pltpu.roll(r, LANES - ROUTER_LO_LANE, 1) + br_ref[...]

    lane = lax.broadcasted_iota(jnp.int32, (tm, LANES), 1)
    lane_f = lane.astype(F32)
    big = float(LANES)
    ninf = -jnp.inf

    def first_max(vals):
        top = jnp.max(vals, axis=-1, keepdims=True)
        idx = jnp.min(jnp.where(vals == top, lane_f, big), axis=-1, keepdims=True)
        return top, idx

    is_g = lane < N_GROUPS
    g_top, g_idx = first_max(jnp.where(is_g, logits, ninf))
    p_g = 1.0 / jnp.sum(jnp.where(is_g, jnp.exp(logits - g_top), 0.0), axis=-1, keepdims=True)
    lo = N_GROUPS + g_idx * EXPERTS_PER_GROUP
    e_vals = jnp.where((lane_f >= lo) & (lane_f < lo + EXPERTS_PER_GROUP), logits, ninf)
    v1, i1 = first_max(e_vals)
    v2, i2 = first_max(jnp.where(lane_f == i1, ninf, e_vals))
    e2 = jnp.exp(v2 - v1)
    w1 = p_g / (1.0 + e2)
    w2 = p_g * e2 / (1.0 + e2)
    rt = jnp.where(lane == 0, i1 - N_GROUPS,
                   jnp.where(lane == 1, i2 - N_GROUPS,
                             jnp.where(lane == 2, w1, jnp.where(lane == 3, w2, 0.0))))
    rt_ref[...] = rt


def _mix(o_a, o_b, x, mod, g_out_a, g_out_b, g_ffn, w_out_bf16, w_router, b_router):
    b, s, d = x.shape
    half = o_a.shape[2]
    tm = 256
    nt = s // tm
    vec = lambda n: pl.BlockSpec((1, n), lambda bi, i: (0, 0))
    return pl.pallas_call(
        _mix_kernel,
        out_shape=(jax.ShapeDtypeStruct((b, s, d), F32),
                   jax.ShapeDtypeStruct((b, s, d), F32),
                   jax.ShapeDtypeStruct((b * s, LANES), F32)),
        grid=(b, nt),
        in_specs=[
            pl.BlockSpec((1, tm, half), lambda bi, i: (bi, i, 0)),
            pl.BlockSpec((1, tm, half), lambda bi, i: (bi, i, 0)),
            pl.BlockSpec((1, tm, d), lambda bi, i: (bi, i, 0)),
            pl.BlockSpec((1, 6, d), lambda bi, i: (bi, 0, 0)),
            vec(half), vec(half), vec(d),
            pl.BlockSpec((d, d), lambda bi, i: (0, 0), pipeline_mode=pl.Buffered(1)),
            pl.BlockSpec((d, LANES), lambda bi, i: (0, 0)),
            vec(LANES),
        ],
        out_specs=(pl.BlockSpec((1, tm, d), lambda bi, i: (bi, i, 0)),
                   pl.BlockSpec((1, tm, d), lambda bi, i: (bi, i, 0)),
                   pl.BlockSpec((tm, LANES), lambda bi, i: (bi * nt + i, 0))),
        compiler_params=pltpu.CompilerParams(
            dimension_semantics=("arbitrary", "arbitrary"), vmem_limit_bytes=VMEM_LIMIT),
        name="mix_out",
    )(o_a, o_b, x, mod, g_out_a.reshape(1, half), g_out_b.reshape(1, half), g_ffn.reshape(1, d),
      w_out_bf16, w_router, b_router.reshape(1, LANES))


MOE_TM = 256


ROW_BUFS = 3
DISPATCH_TM = 256
DISPATCH_STEPS_MIN = ROW_BUFS


def _dispatch_kernel(slot_ref, zf_ref, h_ref, xs_hbm, buf, zbuf, sem, zsem):
    i = pl.program_id(0)
    n_steps = pl.num_programs(0)
    td = h_ref.shape[0]
    tm = zbuf.shape[0]
    n_tiles_max = xs_hbm.shape[0] // tm

    @pl.when(i == 0)
    def _():
        zbuf[...] = jnp.zeros(zbuf.shape, zbuf.dtype)

        def fill(t, carry):
            @pl.when(zf_ref[t] > 0)
            def _():
                row0 = pl.multiple_of(t * tm, tm)
                pltpu.make_async_copy(zbuf, xs_hbm.at[pl.ds(row0, tm)], zsem.at[0]).start()
            return carry
        lax.fori_loop(0, n_tiles_max, fill, 0)

        def drain(t, carry):
            @pl.when(zf_ref[t] > 0)
            def _():
                pltpu.make_async_copy(zbuf, xs_hbm.at[pl.ds(0, tm)], zsem.at[0]).wait()
            return carry
        lax.fori_loop(0, n_tiles_max, drain, 0)

    def wait_rows(s):
        for _ in range(2):
            pltpu.make_async_copy(buf.at[s], xs_hbm.at[pl.ds(0, td)], sem.at[s]).wait()

    def step(s):
        @pl.when(i >= ROW_BUFS)
        def _():
            wait_rows(s)

        buf[s] = h_ref[...]
        for r in range(td):
            for k in range(2):
                dst = slot_ref[2 * (i * td + r) + k]
                pltpu.make_async_copy(buf.at[s, pl.ds(r, 1)], xs_hbm.at[pl.ds(dst, 1)],
                                      sem.at[s]).start(priority=k)

        @pl.when(i == n_steps - 1)
        def _():
            for t in range(min(ROW_BUFS, DISPATCH_STEPS_MIN)):
                wait_rows((s + ROW_BUFS - t) % ROW_BUFS)

    for s in range(ROW_BUFS):
        pl.when(i % ROW_BUFS == s)(functools.partial(step, s))


def _dispatch(h2, slot, zero_fill, n_pad):
    t, d = h2.shape
    tm = MOE_TM
    assert t // DISPATCH_TM >= DISPATCH_STEPS_MIN
    grid_spec = pltpu.PrefetchScalarGridSpec(
        num_scalar_prefetch=2,
        grid=(t // DISPATCH_TM,),
        in_specs=[pl.BlockSpec((DISPATCH_TM, d), lambda i, *_: (i, 0))],
        out_specs=pl.BlockSpec(memory_space=pl.ANY),
        scratch_shapes=[
            pltpu.VMEM((ROW_BUFS, DISPATCH_TM, d), F32),
            pltpu.VMEM((tm, d), F32),
            pltpu.SemaphoreType.DMA((ROW_BUFS,)),
            pltpu.SemaphoreType.DMA((1,)),
        ],
    )
    return pl.pallas_call(
        _dispatch_kernel,
        out_shape=jax.ShapeDtypeStruct((n_pad, d), F32),
        grid_spec=grid_spec,
        compiler_params=pltpu.CompilerParams(
            dimension_semantics=("arbitrary",), vmem_limit_bytes=VMEM_LIMIT),
        name="dispatch",
    )(slot, zero_fill, h2)


def _moe_kernel(te_ref, nxe_ref, nt_ref, x_ref, wg_hbm, wu_hbm, wd_hbm, o_ref,
                wg32, wu32, wd32, wgb, wub, wdb, wsem):
    i = pl.program_id(0)
    n_tiles = nt_ref[0]

    def weight_copies(e):
        return [pltpu.make_async_copy(src.at[e], dst, wsem.at[0])
                for src, dst in ((wg_hbm, wg32), (wu_hbm, wu32), (wd_hbm, wd32))]

    @pl.when(i == 0)
    def _():
        for cp in weight_copies(te_ref[0]):
            cp.start()

    @pl.when(i < n_tiles)
    def _():
        @pl.when(jnp.logical_or(i == 0, te_ref[i] != te_ref[jnp.maximum(i - 1, 0)]))
        def _():
            for cp in weight_copies(0):
                cp.wait()
            wgb[...] = wg32[...].astype(BF16)
            wub[...] = wu32[...].astype(BF16)
            wdb[...] = wd32[...].astype(BF16)

            @pl.when(nxe_ref[i] >= 0)
            def _():
                for cp in weight_copies(nxe_ref[i]):
                    cp.start()

        xb = x_ref[...].astype(BF16)
        g = jnp.dot(xb, wgb[...], preferred_element_type=F32)
        u = jnp.dot(xb, wub[...], preferred_element_type=F32)
        a = (g * jax.nn.sigmoid(g) * u).astype(BF16)
        o_ref[...] = jnp.dot(a, wdb[...], preferred_element_type=F32)

    @pl.when(i >= n_tiles)
    def _():
        o_ref[...] = jnp.zeros(o_ref.shape, o_ref.dtype)


def _moe(xs, w_gate, w_up, w_down, tile_expert, next_expert, n_tiles):
    n_pad, d = xs.shape
    tm = MOE_TM
    max_tiles = n_pad // tm
    de = w_gate.shape[2]
    grid_spec = pltpu.PrefetchScalarGridSpec(
        num_scalar_prefetch=3,
        grid=(max_tiles,),
        in_specs=[pl.BlockSpec((tm, d), lambda i, te, nxe, nt: (jnp.minimum(i, nt[0] - 1), 0))]
        + [pl.BlockSpec(memory_space=pl.ANY)] * 3,
        out_specs=pl.BlockSpec((tm, d), lambda i, *_: (i, 0)),
        scratch_shapes=[
            pltpu.VMEM((d, de), F32),
            pltpu.VMEM((d, de), F32),
            pltpu.VMEM((de, d), F32),
            pltpu.VMEM((d, de), BF16),
            pltpu.VMEM((d, de), BF16),
            pltpu.VMEM((de, d), BF16),
            pltpu.SemaphoreType.DMA((1,)),
        ],
    )
    return pl.pallas_call(
        _moe_kernel,
        out_shape=jax.ShapeDtypeStruct((n_pad, d), F32),
        grid_spec=grid_spec,
        compiler_params=pltpu.CompilerParams(
            dimension_semantics=("arbitrary",), vmem_limit_bytes=VMEM_LIMIT),
        name="moe",
    )(tile_expert, next_expert, n_tiles, xs, w_gate, w_up, w_down)


def _route_plan(rt, n_tok):
    tm = MOE_TM
    n_asg = 2 * n_tok
    n_pad = n_asg + N_EXPERTS * tm
    max_tiles = n_pad // tm
    e_flat = rt[:, :2].astype(jnp.int32).reshape(-1)
    onehot = (e_flat[:, None] == jnp.arange(N_EXPERTS)[None, :]).astype(jnp.int32)
    csum = jnp.cumsum(onehot, axis=0)
    counts = csum[-1]
    rank = jnp.sum(csum * onehot, axis=1) - 1
    tiles_e = (counts + tm - 1) // tm
    tile_end = jnp.cumsum(tiles_e)
    base = (tile_end - tiles_e) * tm
    slot = jnp.sum(onehot * base[None, :], axis=1) + rank
    n_tiles = tile_end[-1]
    tile_id = jnp.arange(max_tiles)
    te = jnp.sum((tile_id[:, None] >= tile_end[None, :]).astype(jnp.int32), axis=1)
    te_last = jnp.sum(((n_tiles - 1) >= tile_end).astype(jnp.int32))
    tile_expert = jnp.where(tile_id < n_tiles, te, te_last).astype(jnp.int32)
    ex = jnp.arange(N_EXPERTS)
    later = (ex[None, :] > ex[:, None]) & (counts[None, :] > 0)
    nxt_e = jnp.min(jnp.where(later, ex[None, :], N_EXPERTS), axis=1)
    nxt_e = jnp.where(nxt_e < N_EXPERTS, nxt_e, -1)
    sel = (tile_expert[:, None] == ex[None, :]).astype(jnp.int32)
    next_expert = jnp.sum(sel * nxt_e[None, :], axis=1).astype(jnp.int32)
    last_tile = jnp.sum(sel * (tile_end - 1)[None, :], axis=1)
    zero_fill = ((tile_id >= n_tiles) | (tile_id == last_tile)).astype(jnp.int32)
    return (tile_expert, next_expert, n_tiles.reshape(1).astype(jnp.int32), zero_fill,
            slot.astype(jnp.int32), n_pad)


FINAL_TM = 256


def _final_kernel(slot_ref, x1_ref, rt_ref, mod_ref, modf_ref, g_ref, y_hbm, o_ref, ybuf, gsem):
    i = pl.program_id(0)
    n_steps = pl.num_programs(0)
    tm = x1_ref.shape[0]

    def row_in(tile, r, k, s):
        idx = slot_ref[2 * (tile * tm + r) + k]
        pltpu.make_async_copy(y_hbm.at[pl.ds(idx, 1)], ybuf.at[s, pl.ds(k * tm + r, 1)],
                              gsem.at[s]).start(priority=k)

    def wait_in(s):
        pltpu.make_async_copy(y_hbm.at[pl.ds(0, 2 * tm)], ybuf.at[s], gsem.at[s]).wait()

    @pl.when(i == 0)
    def _():
        for t in range(ROW_BUFS - 1):
            tile = jnp.minimum(t, n_steps - 1)

            def row(r, carry):
                row_in(tile, r, 0, t)
                row_in(tile, r, 1, t)
                return carry
            lax.fori_loop(0, tm, row, 0, unroll=8)

    def step(s):
        wait_in(s)
        rt = rt_ref[...]
        y = rt[:, 2:3] * ybuf[s, :tm, :] + rt[:, 3:4] * ybuf[s, tm:, :]
        nxt = jnp.minimum(i + ROW_BUFS - 1, n_steps - 1)
        for r in range(tm):
            row_in(nxt, r, 0, (s + ROW_BUFS - 1) % ROW_BUFS)
            row_in(nxt, r, 1, (s + ROW_BUFS - 1) % ROW_BUFS)
        x2 = x1_ref[...] + mod_ref[0, 5:6, :] * y
        o_ref[...] = _rms(x2, g_ref[...]) * (1.0 + modf_ref[0, 1:2, :]) + modf_ref[0, 0:1, :]

        @pl.when(i == n_steps - 1)
        def _():
            for t in range(1, ROW_BUFS):
                wait_in((s + t) % ROW_BUFS)

    for s in range(ROW_BUFS):
        pl.when(i % ROW_BUFS == s)(functools.partial(step, s))


def _final(x1, y_rows, slot, rt, mod, modf, g_final):
    b, s, d = x1.shape
    tm = FINAL_TM
    nt = s // tm
    grid_spec = pltpu.PrefetchScalarGridSpec(
        num_scalar_prefetch=1,
        grid=(b * nt,),
        in_specs=[
            pl.BlockSpec((tm, d), lambda i, *_: (i, 0)),
            pl.BlockSpec((tm, LANES), lambda i, *_: (i, 0)),
            pl.BlockSpec((1, 6, d), lambda i, *_: (i // nt, 0, 0)),
            pl.BlockSpec((1, 2, d), lambda i, *_: (i // nt, 0, 0)),
            pl.BlockSpec((1, d), lambda i, *_: (0, 0)),
            pl.BlockSpec(memory_space=pl.ANY),
        ],
        out_specs=pl.BlockSpec((tm, d), lambda i, *_: (i, 0)),
        scratch_shapes=[
            pltpu.VMEM((ROW_BUFS, 2 * tm, d), F32),
            pltpu.SemaphoreType.DMA((ROW_BUFS,)),
        ],
    )
    out = pl.pallas_call(
        _final_kernel,
        out_shape=jax.ShapeDtypeStruct((b * s, d), F32),
        grid_spec=grid_spec,
        compiler_params=pltpu.CompilerParams(
            dimension_semantics=("arbitrary",), vmem_limit_bytes=VMEM_LIMIT),
        name="final",
    )(slot, x1.reshape(b * s, d), rt, mod, modf, g_final.reshape(1, d), y_rows)
    return out.reshape(b, s, d)


def kernel(x, c, w_ada, b_ada, g_mix, w_in, sinks_a, rel_bias_b, g_out_a, g_out_b, w_out, g_ffn,
           w_router_group, b_router_group, w_router_expert, b_router_expert, w_gate, w_up, w_down,
           w_ada_final, b_ada_final, g_final):
    b, s, d = x.shape
    assert w_ada.shape[0] == 1, "one layer"
    n_tok = b * s

    c_act = jax.nn.silu(c)
    a_rep = jnp.broadcast_to(c_act[:, :, None], (b, d, LANES))
    mod = _ada(a_rep, w_ada[0], b_ada[0]).reshape(b, 6, d)
    modf = _ada(a_rep, w_ada_final, b_ada_final).reshape(b, 2, d)

    proj = _proj(x, mod, g_mix[0], w_in[0].astype(BF16))

    bias_a, nw_a, bias_b, nw_b = _band_tables(rel_bias_b[0])
    kv_a0 = DA_Q // LANES
    o_a = _attention(proj, bias_a, sinks_a[0].astype(F32), n_groups=N_KV_A, ncol=2, nw=nw_a,
                     q_col0=0, k_col0=kv_a0, v_col0=kv_a0 + DA_KV // LANES, kv_share=2, dup=True)
    qb0 = (DA_Q + 2 * DA_KV) // LANES
    o_b = _attention(proj, bias_b, None, n_groups=N_HEADS_B // 2, ncol=1, nw=nw_b,
                     q_col0=qb0, k_col0=qb0 + DB // LANES, v_col0=qb0 + 2 * DB // LANES,
                     kv_share=1, dup=False)

    n_r = N_GROUPS + N_EXPERTS
    assert n_r <= ROUTER_LO_LANE
    w_r = jnp.concatenate([w_router_group[0], w_router_expert[0]], axis=1)
    w_r_hi = w_r.astype(BF16)
    w_r_lo = (w_r - w_r_hi.astype(F32)).astype(BF16)
    w_router = (jnp.zeros((d, LANES), BF16).at[:, :n_r].set(w_r_hi)
                .at[:, ROUTER_LO_LANE:ROUTER_LO_LANE + n_r].set(w_r_lo))
    b_router = jnp.zeros((LANES,), F32).at[:n_r].set(
        jnp.concatenate([b_router_group[0], b_router_expert[0]]))
    x1, h2, rt = _mix(o_a, o_b, x, mod, g_out_a[0], g_out_b[0], g_ffn[0], w_out[0].astype(BF16),
                      w_router, b_router)

    tile_expert, next_expert, n_tiles, zero_fill, slot, n_pad = _route_plan(rt, n_tok)
    xs = _dispatch(h2.reshape(n_tok, d), slot, zero_fill, n_pad)
    y = _moe(xs, w_gate[0], w_up[0], w_down[0], tile_expert, next_expert, n_tiles)

    return _final(x1, y, slot, rt, mod, modf, g_final)
```

```python
import functools

import jax
import jax.numpy as jnp
import numpy as np
from jax import lax
from jax.experimental import pallas as pl
from jax.experimental.pallas import tpu as pltpu

D_MODEL = 2048
CHUNK = 64
HEAD_DIM = 64
N_HEADS_A = 16
N_KV_A = 4
N_PREV_A = 2
N_HEADS_B = 16
N_PREV_B = 8
REL_CLIP = 128
DA_Q = N_HEADS_A * HEAD_DIM
DA_KV = N_KV_A * HEAD_DIM
DB = N_HEADS_B * HEAD_DIM
D_IN = DA_Q + 2 * DA_KV + 3 * DB
N_GROUPS = 4
EXPERTS_PER_GROUP = 8
N_EXPERTS = N_GROUPS * EXPERTS_PER_GROUP
D_EXPERT = D_MODEL // 4
EPS = 1e-6
NEG_INF = -1e30
LOG2E = 1.4426950408889634

LANES = 128
ROUTER_LO_LANE = 64
QBLK = 2 * CHUNK
VMEM_LIMIT = 56 * 1024 * 1024

F32 = jnp.float32
BF16 = jnp.bfloat16


def _rms(x, g):
    return x * lax.rsqrt(jnp.mean(x * x, axis=-1, keepdims=True) + EPS) * g


ADA_JB = 8


def _ada_kernel(a_ref, w_ref, b_ref, o_ref):
    n_b, k, _ = a_ref.shape
    tn = w_ref.shape[1]
    sub = 8
    for jb in range(tn // (ADA_JB * LANES)):
        col0 = jb * ADA_JB * LANES

        def body(kc, accs):
            k0 = pl.multiple_of(kc * sub, sub)
            a_rows = [a_ref[b, pl.ds(k0, sub), :] for b in range(n_b)]
            out = []
            for j in range(ADA_JB):
                w = w_ref[pl.ds(k0, sub), col0 + j * LANES:col0 + (j + 1) * LANES]
                out.append([accs[j][b] + a_rows[b] * w for b in range(n_b)])
            return out

        zero = jnp.zeros((sub, LANES), F32)
        accs = lax.fori_loop(0, k // sub, body, [[zero] * n_b for _ in range(ADA_JB)], unroll=4)
        for j in range(ADA_JB):
            cols = slice(col0 + j * LANES, col0 + (j + 1) * LANES)
            for b in range(n_b):
                o_ref[b:b + 1, cols] = jnp.sum(accs[j][b], axis=0, keepdims=True) + b_ref[:, cols]


def _ada(a_rep, w, bias):
    n_b, k, _ = a_rep.shape
    n = w.shape[1]
    tn = 2048
    return pl.pallas_call(
        _ada_kernel,
        out_shape=jax.ShapeDtypeStruct((n_b, n), F32),
        grid=(n // tn,),
        in_specs=[
            pl.BlockSpec((n_b, k, LANES), lambda j: (0, 0, 0)),
            pl.BlockSpec((k, tn), lambda j: (0, j)),
            pl.BlockSpec((1, tn), lambda j: (0, j)),
        ],
        out_specs=pl.BlockSpec((n_b, tn), lambda j: (0, j)),
        compiler_params=pltpu.CompilerParams(
            dimension_semantics=("arbitrary",), vmem_limit_bytes=VMEM_LIMIT),
        name="ada",
    )(a_rep, w, bias.reshape(1, n))


def _proj_kernel(x_ref, mod_ref, g_ref, w_ref, o_ref, *, n_chunk):
    h = _rms(x_ref[0], g_ref[...])
    h = h * (1.0 + mod_ref[0, 1:2, :]) + mod_ref[0, 0:1, :]
    hb = h.astype(BF16)
    for n0 in range(0, o_ref.shape[2], n_chunk):
        cols = slice(n0, n0 + n_chunk)
        o_ref[0, :, cols] = jnp.dot(hb, w_ref[:, cols], preferred_element_type=F32).astype(BF16)


def _proj(x, mod, g_mix, w_in_bf16):
    b, s, d = x.shape
    n = w_in_bf16.shape[1]
    tm = 512
    return pl.pallas_call(
        functools.partial(_proj_kernel, n_chunk=512),
        out_shape=jax.ShapeDtypeStruct((b, s, n), BF16),
        grid=(b, s // tm),
        in_specs=[
            pl.BlockSpec((1, tm, d), lambda bi, i: (bi, i, 0)),
            pl.BlockSpec((1, 6, d), lambda bi, i: (bi, 0, 0)),
            pl.BlockSpec((1, d), lambda bi, i: (0, 0)),
            pl.BlockSpec((d, n), lambda bi, i: (0, 0), pipeline_mode=pl.Buffered(1)),
        ],
        out_specs=pl.BlockSpec((1, tm, n), lambda bi, i: (bi, i, 0)),
        compiler_params=pltpu.CompilerParams(
            dimension_semantics=("arbitrary", "arbitrary"), vmem_limit_bytes=VMEM_LIMIT),
        name="proj",
    )(x, mod, g_mix.reshape(1, d), w_in_bf16)


def _attn_kernel(*refs, ncol, nw, dup, has_sink, pair_stages):
    if has_sink:
        sink_ref, q_ref, k_ref, v_ref, bias_ref, o_ref = refs[:6]
        scratch = refs[6:]
    else:
        q_ref, k_ref, v_ref, bias_ref, o_ref = refs[:5]
        scratch = refs[5:]
        sink_ref = None
    s_len = q_ref.shape[1]
    n_blk = s_len // QBLK
    n_stack = 2 * ncol
    grp = pl.program_id(1)

    rows = 512
    lane = lax.broadcasted_iota(jnp.int32, (QBLK, LANES), 1)
    low = lane < HEAD_DIM
    if dup:
        kd_ref, vt_ref = scratch[:2]
        half = grp % 2
        keep = (lax.broadcasted_iota(jnp.int32, (rows, LANES), 1) // HEAD_DIM) == half

        def spread(t):
            return jnp.where(keep, t, pltpu.roll(t, HEAD_DIM, 1))

        def dup_body(c, carry):
            r0 = pl.multiple_of(c * rows, rows)
            kd_ref[pl.ds(r0, rows), :] = spread(k_ref[0, pl.ds(r0, rows), :].astype(F32)).astype(BF16)
            return carry

        lax.fori_loop(0, s_len // rows, dup_body, 0)
        k_src = kd_ref
    else:
        vt_ref = scratch[0]
        k_src = k_ref.at[0]

        def spread(t):
            return t

    for c in range(s_len // rows):
        t = spread(v_ref[0, c * rows:(c + 1) * rows, :].astype(F32))
        vt_ref[:, c * rows:(c + 1) * rows] = t.T.astype(BF16)

    top = lax.broadcasted_iota(jnp.int32, (LANES, QBLK), 0) < HEAD_DIM

    if has_sink:
        sink = jnp.concatenate(
            [jnp.full((1, QBLK), sink_ref[grp * n_stack + h] * LOG2E, F32) for h in range(n_stack)],
            axis=1)

    def rows_of(j):
        return j * QBLK if isinstance(j, int) else pl.multiple_of(j * QBLK, QBLK)

    def logits(j, nvb):
        qf = q_ref[0, pl.ds(rows_of(j), QBLK), :].astype(F32) * (HEAD_DIM ** -0.5 * LOG2E)
        parts = []
        for c in range(ncol):
            qc = qf[:, c * LANES:(c + 1) * LANES]
            parts.append(jnp.where(low, qc, 0.0))
            parts.append(jnp.where(low, 0.0, qc))
        lhs = jnp.concatenate(parts, axis=0).astype(BF16)
        kw = k_src[pl.ds(rows_of(j - (nvb - 1)), nvb * QBLK), :]
        s = lax.dot_general(kw, lhs, (((1,), (1,)), ((), ())), preferred_element_type=F32)
        return s + bias_ref[0, (nw - nvb) * QBLK:, :]

    def softmax(s):
        m = jnp.max(s, axis=0, keepdims=True)
        if has_sink:
            m = jnp.maximum(m, sink)
        p = jnp.exp2(s - m)
        denom = jnp.sum(p, axis=0, keepdims=True)
        if has_sink:
            denom = denom + jnp.exp2(sink - m)
        return p.astype(BF16), denom

    def emit(j, nvb, p, denom):
        vw = vt_ref[:, pl.ds(rows_of(j - (nvb - 1)), nvb * QBLK)]
        o = jnp.dot(vw, p, preferred_element_type=F32) / denom
        for c in range(ncol):
            o0 = o[:, (2 * c) * QBLK:(2 * c + 1) * QBLK]
            o1 = o[:, (2 * c + 1) * QBLK:(2 * c + 2) * QBLK]
            o_ref[0, pl.ds(rows_of(j), QBLK), c * LANES:(c + 1) * LANES] = (
                jnp.where(top, o0, o1).T.astype(BF16))

    first = nw - 1
    for j in range(first + (n_blk - first) % 2):
        nvb = min(j + 1, nw)
        emit(j, nvb, *softmax(logits(j, nvb)))
    first += (n_blk - first) % 2

    s_scr, p_scr = scratch[-2:]
    last = n_blk - 1
    if pair_stages:
        den = []
        for u in range(2):
            pu, du = softmax(logits(first + u, nw))
            p_scr[u] = pu
            den.append(du)
        for u in range(2):
            s_scr[u] = logits(min(first + 2 + u, last), nw)

        def body(t, den):
            j = first + 2 * t
            for u in range(2):
                emit(j + u, nw, p_scr[u], den[u])
            nxt = [softmax(s_scr[u]) for u in range(2)]
            for u in range(2):
                p_scr[u] = nxt[u][0]
            for u in range(2):
                s_scr[u] = logits(jnp.minimum(j + 4 + u, last), nw)
            return [nxt[0][1], nxt[1][1]]

        lax.fori_loop(0, (n_blk - first) // 2, body, den)
    else:
        s_scr[0] = logits(first, nw)
        p0, d0 = softmax(s_scr[0])
        p_scr[0] = p0
        s_scr[1] = logits(first + 1, nw)

        def body(t, d_even):
            j = first + 2 * t
            emit(j, nw, p_scr[0], d_even)
            p1, d_odd = softmax(s_scr[1])
            p_scr[1] = p1
            s_scr[0] = logits(jnp.minimum(j + 2, last), nw)
            emit(j + 1, nw, p_scr[1], d_odd)
            p0, d_next = softmax(s_scr[0])
            p_scr[0] = p0
            s_scr[1] = logits(jnp.minimum(j + 3, last), nw)
            return d_next

        lax.fori_loop(0, (n_blk - first) // 2, body, d0)


def _attention(proj, bias, sinks, *, n_groups, ncol, nw, q_col0, k_col0, v_col0, kv_share, dup):
    b, s, _ = proj.shape
    qw = ncol * LANES
    has_sink = sinks is not None
    kernel = functools.partial(_attn_kernel, ncol=ncol, nw=nw, dup=dup, has_sink=has_sink,
                               pair_stages=ncol == 1)
    _, n_keys, n_q = bias.shape
    in_specs = [
        pl.BlockSpec((1, s, qw), lambda bi, g, *_: (bi, 0, q_col0 // ncol + g)),
        pl.BlockSpec((1, s, LANES), lambda bi, g, *_: (bi, 0, k_col0 + g // kv_share)),
        pl.BlockSpec((1, s, LANES), lambda bi, g, *_: (bi, 0, v_col0 + g // kv_share)),
        pl.BlockSpec((1, n_keys, n_q), lambda bi, g, *_: (g, 0, 0)),
    ]
    out_spec = pl.BlockSpec((1, s, qw), lambda bi, g, *_: (bi, 0, g))
    scratch = [pltpu.VMEM((s, LANES), BF16)] if dup else []
    scratch += [pltpu.VMEM((LANES, s), BF16),
                pltpu.VMEM((2, n_keys, n_q), F32), pltpu.VMEM((2, n_keys, n_q), BF16)]
    grid_spec = pltpu.PrefetchScalarGridSpec(
        num_scalar_prefetch=1 if has_sink else 0,
        grid=(b, n_groups),
        in_specs=in_specs,
        out_specs=out_spec,
        scratch_shapes=scratch,
    )
    args = ((sinks,) if has_sink else ()) + (proj, proj, proj, bias)
    return pl.pallas_call(
        kernel,
        out_shape=jax.ShapeDtypeStruct((b, s, n_groups * qw), BF16),
        grid_spec=grid_spec,
        compiler_params=pltpu.CompilerParams(
            dimension_semantics=("arbitrary", "arbitrary"), vmem_limit_bytes=VMEM_LIMIT),
        name="attn_a" if dup else "attn_b",
    )(*args)


def _band_tables(rel_bias_b):
    qi = jnp.arange(QBLK)[:, None]
    r = qi // CHUNK

    def table(nw, n_prev, fn):
        kj = jnp.arange(nw * QBLK)[None, :]
        rel = (nw - 1) * QBLK + qi - kj
        inband = (kj >= r * CHUNK) & (kj < (r + n_prev + 1) * CHUNK)
        return jnp.where(inband[None], fn(rel), NEG_INF)

    slopes = jnp.exp2(-8.0 * jnp.arange(1, N_HEADS_A + 1, dtype=F32) / N_HEADS_A)
    nw_a = (N_PREV_A * CHUNK) // QBLK + 1
    nw_b = (N_PREV_B * CHUNK) // QBLK + 1
    bias_a = table(nw_a, N_PREV_A, lambda rel: -slopes[:, None, None] * jnp.abs(rel).astype(F32)[None])
    w_b = nw_b * QBLK
    p = QBLK + w_b - 1
    m = (np.arange(p) + QBLK - 1) % p - (QBLK - 1)
    dist = np.clip((nw_b - 1) * QBLK - m, -REL_CLIP, REL_CLIP) + REL_CLIP
    vec = rel_bias_b[:, dist].astype(F32)
    reps = -(-(QBLK * (p - 1)) // p)
    rel_b = jnp.tile(vec, (1, reps))[:, :QBLK * (p - 1)].reshape(N_HEADS_B, QBLK, p - 1)[:, :, :w_b]
    bias_b = table(nw_b, N_PREV_B, lambda rel: rel_b)
    rep = N_HEADS_A // N_KV_A
    bias_a = bias_a.reshape(N_KV_A, rep * QBLK, nw_a * QBLK).transpose(0, 2, 1)
    bias_b = bias_b.reshape(N_HEADS_B // 2, 2 * QBLK, nw_b * QBLK).transpose(0, 2, 1)
    return bias_a * LOG2E, nw_a, bias_b * LOG2E, nw_b


def _mix_kernel(oa_ref, ob_ref, x_ref, mod_ref, ga_ref, gb_ref, gf_ref, wo_ref, wr_ref, br_ref,
                x1_ref, h2_ref, rt_ref):
    half = oa_ref.shape[2]
    na = _rms(oa_ref[0].astype(F32), ga_ref[...]).astype(BF16)
    nb = _rms(ob_ref[0].astype(F32), gb_ref[...]).astype(BF16)
    acc = jnp.dot(na, wo_ref[:half, :], preferred_element_type=F32)
    acc = acc + jnp.dot(nb, wo_ref[half:, :], preferred_element_type=F32)
    x1 = x_ref[0] + mod_ref[0, 2:3, :] * acc
    x1_ref[0] = x1
    h2 = _rms(x1, gf_ref[...]) * (1.0 + mod_ref[0, 4:5, :]) + mod_ref[0, 3:4, :]
    h2_ref[0] = h2

    tm = h2.shape[0]
    h_hi = h2.astype(BF16)
    h_lo = (h2 - h_hi.astype(F32)).astype(BF16)
    r = jnp.dot(jnp.concatenate([h_hi, h_lo], axis=0), wr_ref[...], preferred_element_type=F32)
    r = r[:tm] + r[tm:]
    logits = r + pltpu.roll(r, LANES - ROUTER_LO_LANE, 1) + br_ref[...]

    lane = lax.broadcasted_iota(jnp.int32, (tm, LANES), 1)
    lane_f = lane.astype(F32)
    big = float(LANES)
    ninf = -jnp.inf

    def first_max(vals):
        top = jnp.max(vals, axis=-1, keepdims=True)
        idx = jnp.min(jnp.where(vals == top, lane_f, big), axis=-1, keepdims=True)
        return top, idx

    is_g = lane < N_GROUPS
    g_top, g_idx = first_max(jnp.where(is_g, logits, ninf))
    p_g = 1.0 / jnp.sum(jnp.where(is_g, jnp.exp(logits - g_top), 0.0), axis=-1, keepdims=True)
    lo = N_GROUPS + g_idx * EXPERTS_PER_GROUP
    e_vals = jnp.where((lane_f >= lo) & (lane_f < lo + EXPERTS_PER_GROUP), logits, ninf)
    v1, i1 = first_max(e_vals)
    v2, i2 = first_max(jnp.where(lane_f == i1, ninf, e_vals))
    e2 = jnp.exp(v2 - v1)
    w1 = p_g / (1.0 + e2)
    w2 = p_g * e2 / (1.0 + e2)
    rt = jnp.where(lane == 0, i1 - N_GROUPS,
                   jnp.where(lane == 1, i2 - N_GROUPS,
                             jnp.where(lane == 2, w1, jnp.where(lane == 3, w2, 0.0))))
    rt_ref[...] = rt


def _mix(o_a, o_b, x, mod, g_out_a, g_out_b, g_ffn, w_out_bf16, w_router, b_router):
    b, s, d = x.shape
    half = o_a.shape[2]
    tm = 256
    nt = s // tm
    vec = lambda n: pl.BlockSpec((1, n), lambda bi, i: (0, 0))
    return pl.pallas_call(
        _mix_kernel,
        out_shape=(jax.ShapeDtypeStruct((b, s, d), F32),
                   jax.ShapeDtypeStruct((b, s, d), F32),
                   jax.ShapeDtypeStruct((b * s, LANES), F32)),
        grid=(b, nt),
        in_specs=[
            pl.BlockSpec((1, tm, half), lambda bi, i: (bi, i, 0)),
            pl.BlockSpec((1, tm, half), lambda bi, i: (bi, i, 0)),
            pl.BlockSpec((1, tm, d), lambda bi, i: (bi, i, 0)),
            pl.BlockSpec((1, 6, d), lambda bi, i: (bi, 0, 0)),
            vec(half), vec(half), vec(d),
            pl.BlockSpec((d, d), lambda bi, i: (0, 0), pipeline_mode=pl.Buffered(1)),
            pl.BlockSpec((d, LANES), lambda bi, i: (0, 0)),
            vec(LANES),
        ],
        out_specs=(pl.BlockSpec((1, tm, d), lambda bi, i: (bi, i, 0)),
                   pl.BlockSpec((1, tm, d), lambda bi, i: (bi, i, 0)),
                   pl.BlockSpec((tm, LANES), lambda bi, i: (bi * nt + i, 0))),
        compiler_params=pltpu.CompilerParams(
            dimension_semantics=("arbitrary", "arbitrary"), vmem_limit_bytes=VMEM_LIMIT),
        name="mix_out",
    )(o_a, o_b, x, mod, g_out_a.reshape(1, half), g_out_b.reshape(1, half), g_ffn.reshape(1, d),
      w_out_bf16, w_router, b_router.reshape(1, LANES))


MOE_TM = 256


ROW_BUFS = 3
DISPATCH_TM = 256
DISPATCH_STEPS_MIN = ROW_BUFS


def _dispatch_kernel(slot_ref, zf_ref, h_ref, xs_hbm, buf, zbuf, sem, zsem):
    i = pl.program_id(0)
    n_steps = pl.num_programs(0)
    td = h_ref.shape[0]
    tm = zbuf.shape[0]
    n_tiles_max = xs_hbm.shape[0] // tm

    @pl.when(i == 0)
    def _():
        zbuf[...] = jnp.zeros(zbuf.shape, zbuf.dtype)

        def fill(t, carry):
            @pl.when(zf_ref[t] > 0)
            def _():
                row0 = pl.multiple_of(t * tm, tm)
                pltpu.make_async_copy(zbuf, xs_hbm.at[pl.ds(row0, tm)], zsem.at[0]).start()
            return carry
        lax.fori_loop(0, n_tiles_max, fill, 0)

        def drain(t, carry):
            @pl.when(zf_ref[t] > 0)
            def _():
                pltpu.make_async_copy(zbuf, xs_hbm.at[pl.ds(0, tm)], zsem.at[0]).wait()
            return carry
        lax.fori_loop(0, n_tiles_max, drain, 0)

    def wait_rows(s):
        for _ in range(2):
            pltpu.make_async_copy(buf.at[s], xs_hbm.at[pl.ds(0, td)], sem.at[s]).wait()

    def step(s):
        @pl.when(i >= ROW_BUFS)
        def _():
            wait_rows(s)

        buf[s] = h_ref[...]
        for r in range(td):
            for k in range(2):
                dst = slot_ref[2 * (i * td + r) + k]
                pltpu.make_async_copy(buf.at[s, pl.ds(r, 1)], xs_hbm.at[pl.ds(dst, 1)],
                                      sem.at[s]).start(priority=k)

        @pl.when(i == n_steps - 1)
        def _():
            for t in range(min(ROW_BUFS, DISPATCH_STEPS_MIN)):
                wait_rows((s + ROW_BUFS - t) % ROW_BUFS)

    for s in range(ROW_BUFS):
        pl.when(i % ROW_BUFS == s)(functools.partial(step, s))


def _dispatch(h2, slot, zero_fill, n_pad):
    t, d = h2.shape
    tm = MOE_TM
    assert t // DISPATCH_TM >= DISPATCH_STEPS_MIN
    grid_spec = pltpu.PrefetchScalarGridSpec(
        num_scalar_prefetch=2,
        grid=(t // DISPATCH_TM,),
        in_specs=[pl.BlockSpec((DISPATCH_TM, d), lambda i, *_: (i, 0))],
        out_specs=pl.BlockSpec(memory_space=pl.ANY),
        scratch_shapes=[
            pltpu.VMEM((ROW_BUFS, DISPATCH_TM, d), F32),
            pltpu.VMEM((tm, d), F32),
            pltpu.SemaphoreType.DMA((ROW_BUFS,)),
            pltpu.SemaphoreType.DMA((1,)),
        ],
    )
    return pl.pallas_call(
        _dispatch_kernel,
        out_shape=jax.ShapeDtypeStruct((n_pad, d), F32),
        grid_spec=grid_spec,
        compiler_params=pltpu.CompilerParams(
            dimension_semantics=("arbitrary",), vmem_limit_bytes=VMEM_LIMIT),
        name="dispatch",
    )(slot, zero_fill, h2)


def _moe_kernel(te_ref, nxe_ref, nt_ref, x_ref, wg_hbm, wu_hbm, wd_hbm, o_ref,
                wg32, wu32, wd32, wgb, wub, wdb, wsem):
    i = pl.program_id(0)
    n_tiles = nt_ref[0]

    def weight_copies(e):
        return [pltpu.make_async_copy(src.at[e], dst, wsem.at[0])
                for src, dst in ((wg_hbm, wg32), (wu_hbm, wu32), (wd_hbm, wd32))]

    @pl.when(i == 0)
    def _():
        for cp in weight_copies(te_ref[0]):
            cp.start()

    @pl.when(i < n_tiles)
    def _():
        @pl.when(jnp.logical_or(i == 0, te_ref[i] != te_ref[jnp.maximum(i - 1, 0)]))
        def _():
            for cp in weight_copies(0):
                cp.wait()
            wgb[...] = wg32[...].astype(BF16)
            wub[...] = wu32[...].astype(BF16)
            wdb[...] = wd32[...].astype(BF16)

            @pl.when(nxe_ref[i] >= 0)
            def _():
                for cp in weight_copies(nxe_ref[i]):
                    cp.start()

        xb = x_ref[...].astype(BF16)
        g = jnp.dot(xb, wgb[...], preferred_element_type=F32)
        u = jnp.dot(xb, wub[...], preferred_element_type=F32)
        a = (g * jax.nn.sigmoid(g) * u).astype(BF16)
        o_ref[...] = jnp.dot(a, wdb[...], preferred_element_type=F32)

    @pl.when(i >= n_tiles)
    def _():
        o_ref[...] = jnp.zeros(o_ref.shape, o_ref.dtype)


def _moe(xs, w_gate, w_up, w_down, tile_expert, next_expert, n_tiles):
    n_pad, d = xs.shape
    tm = MOE_TM
    max_tiles = n_pad // tm
    de = w_gate.shape[2]
    grid_spec = pltpu.PrefetchScalarGridSpec(
        num_scalar_prefetch=3,
        grid=(max_tiles,),
        in_specs=[pl.BlockSpec((tm, d), lambda i, te, nxe, nt: (jnp.minimum(i, nt[0] - 1), 0))]
        + [pl.BlockSpec(memory_space=pl.ANY)] * 3,
        out_specs=pl.BlockSpec((tm, d), lambda i, *_: (i, 0)),
        scratch_shapes=[
            pltpu.VMEM((d, de), F32),
            pltpu.VMEM((d, de), F32),
            pltpu.VMEM((de, d), F32),
            pltpu.VMEM((d, de), BF16),
            pltpu.VMEM((d, de), BF16),
            pltpu.VMEM((de, d), BF16),
            pltpu.SemaphoreType.DMA((1,)),
        ],
    )
    return pl.pallas_call(
        _moe_kernel,
        out_shape=jax.ShapeDtypeStruct((n_pad, d), F32),
        grid_spec=grid_spec,
        compiler_params=pltpu.CompilerParams(
            dimension_semantics=("arbitrary",), vmem_limit_bytes=VMEM_LIMIT),
        name="moe",
    )(tile_expert, next_expert, n_tiles, xs, w_gate, w_up, w_down)


def _route_plan(rt, n_tok):
    tm = MOE_TM
    n_asg = 2 * n_tok
    n_pad = n_asg + N_EXPERTS * tm
    max_tiles = n_pad // tm
    e_flat = rt[:, :2].astype(jnp.int32).reshape(-1)
    onehot = (e_flat[:, None] == jnp.arange(N_EXPERTS)[None, :]).astype(jnp.int32)
    csum = jnp.cumsum(onehot, axis=0)
    counts = csum[-1]
    rank = jnp.sum(csum * onehot, axis=1) - 1
    tiles_e = (counts + tm - 1) // tm
    tile_end = jnp.cumsum(tiles_e)
    base = (tile_end - tiles_e) * tm
    slot = jnp.sum(onehot * base[None, :], axis=1) + rank
    n_tiles = tile_end[-1]
    tile_id = jnp.arange(max_tiles)
    te = jnp.sum((tile_id[:, None] >= tile_end[None, :]).astype(jnp.int32), axis=1)
    te_last = jnp.sum(((n_tiles - 1) >= tile_end).astype(jnp.int32))
    tile_expert = jnp.where(tile_id < n_tiles, te, te_last).astype(jnp.int32)
    ex = jnp.arange(N_EXPERTS)
    later = (ex[None, :] > ex[:, None]) & (counts[None, :] > 0)
    nxt_e = jnp.min(jnp.where(later, ex[None, :], N_EXPERTS), axis=1)
    nxt_e = jnp.where(nxt_e < N_EXPERTS, nxt_e, -1)
    sel = (tile_expert[:, None] == ex[None, :]).astype(jnp.int32)
    next_expert = jnp.sum(sel * nxt_e[None, :], axis=1).astype(jnp.int32)
    last_tile = jnp.sum(sel * (tile_end - 1)[None, :], axis=1)
    zero_fill = ((tile_id >= n_tiles) | (tile_id == last_tile)).astype(jnp.int32)
    return (tile_expert, next_expert, n_tiles.reshape(1).astype(jnp.int32), zero_fill,
            slot.astype(jnp.int32), n_pad)


FINAL_TM = 256


def _final_kernel(slot_ref, x1_ref, rt_ref, mod_ref, modf_ref, g_ref, y_hbm, o_ref, ybuf, gsem):
    i = pl.program_id(0)
    n_steps = pl.num_programs(0)
    tm = x1_ref.shape[0]

    def row_in(tile, r, k, s):
        idx = slot_ref[2 * (tile * tm + r) + k]
        pltpu.make_async_copy(y_hbm.at[pl.ds(idx, 1)], ybuf.at[s, pl.ds(k * tm + r, 1)],
                              gsem.at[s]).start(priority=k)

    def wait_in(s):
        pltpu.make_async_copy(y_hbm.at[pl.ds(0, 2 * tm)], ybuf.at[s], gsem.at[s]).wait()

    @pl.when(i == 0)
    def _():
        for t in range(ROW_BUFS - 1):
            tile = jnp.minimum(t, n_steps - 1)

            def row(r, carry):
                row_in(tile, r, 0, t)
                row_in(tile, r, 1, t)
                return carry
            lax.fori_loop(0, tm, row, 0, unroll=8)

    def step(s):
        wait_in(s)
        rt = rt_ref[...]
        y = rt[:, 2:3] * ybuf[s, :tm, :] + rt[:, 3:4] * ybuf[s, tm:, :]
        nxt = jnp.minimum(i + ROW_BUFS - 1, n_steps - 1)
        for r in range(tm):
            row_in(nxt, r, 0, (s + ROW_BUFS - 1) % ROW_BUFS)
            row_in(nxt, r, 1, (s + ROW_BUFS - 1) % ROW_BUFS)
        x2 = x1_ref[...] + mod_ref[0, 5:6, :] * y
        o_ref[...] = _rms(x2, g_ref[...]) * (1.0 + modf_ref[0, 1:2, :]) + modf_ref[0, 0:1, :]

        @pl.when(i == n_steps - 1)
        def _():
            for t in range(1, ROW_BUFS):
                wait_in((s + t) % ROW_BUFS)

    for s in range(ROW_BUFS):
        pl.when(i % ROW_BUFS == s)(functools.partial(step, s))


def _final(x1, y_rows, slot, rt, mod, modf, g_final):
    b, s, d = x1.shape
    tm = FINAL_TM
    nt = s // tm
    grid_spec = pltpu.PrefetchScalarGridSpec(
        num_scalar_prefetch=1,
        grid=(b * nt,),
        in_specs=[
            pl.BlockSpec((tm, d), lambda i, *_: (i, 0)),
            pl.BlockSpec((tm, LANES), lambda i, *_: (i, 0)),
            pl.BlockSpec((1, 6, d), lambda i, *_: (i // nt, 0, 0)),
            pl.BlockSpec((1, 2, d), lambda i, *_: (i // nt, 0, 0)),
            pl.BlockSpec((1, d), lambda i, *_: (0, 0)),
            pl.BlockSpec(memory_space=pl.ANY),
        ],
        out_specs=pl.BlockSpec((tm, d), lambda i, *_: (i, 0)),
        scratch_shapes=[
            pltpu.VMEM((ROW_BUFS, 2 * tm, d), F32),
            pltpu.SemaphoreType.DMA((ROW_BUFS,)),
        ],
    )
    out = pl.pallas_call(
        _final_kernel,
        out_shape=jax.ShapeDtypeStruct((b * s, d), F32),
        grid_spec=grid_spec,
        compiler_params=pltpu.CompilerParams(
            dimension_semantics=("arbitrary",), vmem_limit_bytes=VMEM_LIMIT),
        name="final",
    )(slot, x1.reshape(b * s, d), rt, mod, modf, g_final.reshape(1, d), y_rows)
    return out.reshape(b, s, d)


def kernel(x, c, w_ada, b_ada, g_mix, w_in, sinks_a, rel_bias_b, g_out_a, g_out_b, w_out, g_ffn,
           w_router_group, b_router_group, w_router_expert, b_router_expert, w_gate, w_up, w_down,
           w_ada_final, b_ada_final, g_final):
    b, s, d = x.shape
    assert w_ada.shape[0] == 1, "one layer"
    n_tok = b * s

    c_act = jax.nn.silu(c)
    a_rep = jnp.broadcast_to(c_act[:, :, None], (b, d, LANES))
    mod = _ada(a_rep, w_ada[0], b_ada[0]).reshape(b, 6, d)
    modf = _ada(a_rep, w_ada_final, b_ada_final).reshape(b, 2, d)

    proj = _proj(x, mod, g_mix[0], w_in[0].astype(BF16))

    bias_a, nw_a, bias_b, nw_b = _band_tables(rel_bias_b[0])
    kv_a0 = DA_Q // LANES
    o_a = _attention(proj, bias_a, sinks_a[0].astype(F32), n_groups=N_KV_A, ncol=2, nw=nw_a,
                     q_col0=0, k_col0=kv_a0, v_col0=kv_a0 + DA_KV // LANES, kv_share=2, dup=True)
    qb0 = (DA_Q + 2 * DA_KV) // LANES
    o_b = _attention(proj, bias_b, None, n_groups=N_HEADS_B // 2, ncol=1, nw=nw_b,
                     q_col0=qb0, k_col0=qb0 + DB // LANES, v_col0=qb0 + 2 * DB // LANES,
                     kv_share=1, dup=False)

    n_r = N_GROUPS + N_EXPERTS
    assert n_r <= ROUTER_LO_LANE
    w_r = jnp.concatenate([w_router_group[0], w_router_expert[0]], axis=1)
    w_r_hi = w_r.astype(BF16)
    w_r_lo = (w_r - w_r_hi.astype(F32)).astype(BF16)
    w_router = (jnp.zeros((d, LANES), BF16).at[:, :n_r].set(w_r_hi)
                .at[:, ROUTER_LO_LANE:ROUTER_LO_LANE + n_r].set(w_r_lo))
    b_router = jnp.zeros((LANES,), F32).at[:n_r].set(
        jnp.concatenate([b_router_group[0], b_router_expert[0]]))
    x1, h2, rt = _mix(o_a, o_b, x, mod, g_out_a[0], g_out_b[0], g_ffn[0], w_out[0].astype(BF16),
                      w_router, b_router)

    tile_expert, next_expert, n_tiles, zero_fill, slot, n_pad = _route_plan(rt, n_tok)
    xs = _dispatch(h2.reshape(n_tok, d), slot, zero_fill, n_pad)
    y = _moe(xs, w_gate[0], w_up[0], w_down[0], tile_expert, next_expert, n_tiles)

    return _final(x1, y, slot, rt, mod, modf, g_final)
```

```python
import functools

import jax
import jax.numpy as jnp
import numpy as np
from jax import lax
from jax.experimental import pallas as pl
from jax.experimental.pallas import tpu as pltpu

D_MODEL = 2048
CHUNK = 64
HEAD_DIM = 64
N_HEADS_A = 16
N_KV_A = 4
N_PREV_A = 2
N_HEADS_B = 16
N_PREV_B = 8
REL_CLIP = 128
DA_Q = N_HEADS_A * HEAD_DIM
DA_KV = N_KV_A * HEAD_DIM
DB = N_HEADS_B * HEAD_DIM
D_IN = DA_Q + 2 * DA_KV + 3 * DB
N_GROUPS = 4
EXPERTS_PER_GROUP = 8
N_EXPERTS = N_GROUPS * EXPERTS_PER_GROUP
D_EXPERT = D_MODEL // 4
EPS = 1e-6
NEG_INF = -1e30
LOG2E = 1.4426950408889634

LANES = 128
ROUTER_LO_LANE = 64
QBLK = 2 * CHUNK
VMEM_LIMIT = 56 * 1024 * 1024

F32 = jnp.float32
BF16 = jnp.bfloat16


def _rms(x, g):
    return x * lax.rsqrt(jnp.mean(x * x, axis=-1, keepdims=True) + EPS) * g


ADA_JB = 8


def _ada_kernel(a_ref, w_ref, b_ref, o_ref):
    n_b, k, _ = a_ref.shape
    tn = w_ref.shape[1]
    sub = 8
    for jb in range(tn // (ADA_JB * LANES)):
        col0 = jb * ADA_JB * LANES

        def body(kc, accs):
            k0 = pl.multiple_of(kc * sub, sub)
            a_rows = [a_ref[b, pl.ds(k0, sub), :] for b in range(n_b)]
            out = []
            for j in range(ADA_JB):
                w = w_ref[pl.ds(k0, sub), col0 + j * LANES:col0 + (j + 1) * LANES]
                out.append([accs[j][b] + a_rows[b] * w for b in range(n_b)])
            return out

        zero = jnp.zeros((sub, LANES), F32)
        accs = lax.fori_loop(0, k // sub, body, [[zero] * n_b for _ in range(ADA_JB)], unroll=4)
        for j in range(ADA_JB):
            cols = slice(col0 + j * LANES, col0 + (j + 1) * LANES)
            for b in range(n_b):
                o_ref[b:b + 1, cols] = jnp.sum(accs[j][b], axis=0, keepdims=True) + b_ref[:, cols]


def _ada(a_rep, w, bias):
    n_b, k, _ = a_rep.shape
    n = w.shape[1]
    tn = 2048
    return pl.pallas_call(
        _ada_kernel,
        out_shape=jax.ShapeDtypeStruct((n_b, n), F32),
        grid=(n // tn,),
        in_specs=[
            pl.BlockSpec((n_b, k, LANES), lambda j: (0, 0, 0)),
            pl.BlockSpec((k, tn), lambda j: (0, j)),
            pl.BlockSpec((1, tn), lambda j: (0, j)),
        ],
        out_specs=pl.BlockSpec((n_b, tn), lambda j: (0, j)),
        compiler_params=pltpu.CompilerParams(
            dimension_semantics=("arbitrary",), vmem_limit_bytes=VMEM_LIMIT),
        name="ada",
    )(a_rep, w, bias.reshape(1, n))


def _proj_kernel(x_ref, mod_ref, g_ref, w_ref, o_ref, *, n_chunk):
    h = _rms(x_ref[0], g_ref[...])
    h = h * (1.0 + mod_ref[0, 1:2, :]) + mod_ref[0, 0:1, :]
    hb = h.astype(BF16)
    for n0 in range(0, o_ref.shape[2], n_chunk):
        cols = slice(n0, n0 + n_chunk)
        o_ref[0, :, cols] = jnp.dot(hb, w_ref[:, cols], preferred_element_type=F32).astype(BF16)


def _proj(x, mod, g_mix, w_in_bf16):
    b, s, d = x.shape
    n = w_in_bf16.shape[1]
    tm = 512
    return pl.pallas_call(
        functools.partial(_proj_kernel, n_chunk=512),
        out_shape=jax.ShapeDtypeStruct((b, s, n), BF16),
        grid=(b, s // tm),
        in_specs=[
            pl.BlockSpec((1, tm, d), lambda bi, i: (bi, i, 0)),
            pl.BlockSpec((1, 6, d), lambda bi, i: (bi, 0, 0)),
            pl.BlockSpec((1, d), lambda bi, i: (0, 0)),
            pl.BlockSpec((d, n), lambda bi, i: (0, 0), pipeline_mode=pl.Buffered(1)),
        ],
        out_specs=pl.BlockSpec((1, tm, n), lambda bi, i: (bi, i, 0)),
        compiler_params=pltpu.CompilerParams(
            dimension_semantics=("arbitrary", "arbitrary"), vmem_limit_bytes=VMEM_LIMIT),
        name="proj",
    )(x, mod, g_mix.reshape(1, d), w_in_bf16)


def _attn_kernel(*refs, ncol, nw, dup, has_sink):
    if has_sink:
        sink_ref, q_ref, k_ref, v_ref, bias_ref, o_ref = refs[:6]
        scratch = refs[6:]
    else:
        q_ref, k_ref, v_ref, bias_ref, o_ref = refs[:5]
        scratch = refs[5:]
        sink_ref = None
    s_len = q_ref.shape[1]
    n_blk = s_len // QBLK
    n_stack = 2 * ncol
    grp = pl.program_id(1)

    rows = 512
    lane = lax.broadcasted_iota(jnp.int32, (QBLK, LANES), 1)
    low = lane < HEAD_DIM
    if dup:
        kd_ref, vt_ref = scratch[:2]
        half = grp % 2
        keep = (lax.broadcasted_iota(jnp.int32, (rows, LANES), 1) // HEAD_DIM) == half

        def spread(t):
            return jnp.where(keep, t, pltpu.roll(t, HEAD_DIM, 1))

        def dup_body(c, carry):
            r0 = pl.multiple_of(c * rows, rows)
            kd_ref[pl.ds(r0, rows), :] = spread(k_ref[0, pl.ds(r0, rows), :].astype(F32)).astype(BF16)
            return carry

        lax.fori_loop(0, s_len // rows, dup_body, 0)
        k_src = kd_ref
    else:
        vt_ref = scratch[0]
        k_src = k_ref.at[0]

        def spread(t):
            return t

    for c in range(s_len // rows):
        t = spread(v_ref[0, c * rows:(c + 1) * rows, :].astype(F32))
        vt_ref[:, c * rows:(c + 1) * rows] = t.T.astype(BF16)

    top = lax.broadcasted_iota(jnp.int32, (LANES, QBLK), 0) < HEAD_DIM

    if has_sink:
        sink = jnp.concatenate(
            [jnp.full((1, QBLK), sink_ref[grp * n_stack + h] * LOG2E, F32) for h in range(n_stack)],
            axis=1)

    def rows_of(j):
        return j * QBLK if isinstance(j, int) else pl.multiple_of(j * QBLK, QBLK)

    def logits(j, nvb):
        qf = q_ref[0, pl.ds(rows_of(j), QBLK), :].astype(F32) * (HEAD_DIM ** -0.5 * LOG2E)
        parts = []
        for c in range(ncol):
            qc = qf[:, c * LANES:(c + 1) * LANES]
            parts.append(jnp.where(low, qc, 0.0))
            parts.append(jnp.where(low, 0.0, qc))
        lhs = jnp.concatenate(parts, axis=0).astype(BF16)
        kw = k_src[pl.ds(rows_of(j - (nvb - 1)), nvb * QBLK), :]
        s = lax.dot_general(kw, lhs, (((1,), (1,)), ((), ())), preferred_element_type=F32)
        return s + bias_ref[0, (nw - nvb) * QBLK:, :]

    def softmax(s):
        m = jnp.max(s, axis=0, keepdims=True)
        if has_sink:
            m = jnp.maximum(m, sink)
        p = jnp.exp2(s - m)
        denom = jnp.sum(p, axis=0, keepdims=True)
        if has_sink:
            denom = denom + jnp.exp2(sink - m)
        return p.astype(BF16), denom

    def emit(j, nvb, p, denom):
        vw = vt_ref[:, pl.ds(rows_of(j - (nvb - 1)), nvb * QBLK)]
        o = jnp.dot(vw, p, preferred_element_type=F32) / denom
        for c in range(ncol):
            o0 = o[:, (2 * c) * QBLK:(2 * c + 1) * QBLK]
            o1 = o[:, (2 * c + 1) * QBLK:(2 * c + 2) * QBLK]
            o_ref[0, pl.ds(rows_of(j), QBLK), c * LANES:(c + 1) * LANES] = (
                jnp.where(top, o0, o1).T.astype(BF16))

    first = nw - 1
    for j in range(first + (n_blk - first) % 2):
        nvb = min(j + 1, nw)
        emit(j, nvb, *softmax(logits(j, nvb)))
    first += (n_blk - first) % 2

    s_scr, p_scr = scratch[-2:]
    last = n_blk - 1
    s_scr[0] = logits(first, nw)
    p0, d0 = softmax(s_scr[0])
    p_scr[0] = p0
    s_scr[1] = logits(first + 1, nw)

    def body(t, d_even):
        j = first + 2 * t
        emit(j, nw, p_scr[0], d_even)
        p1, d_odd = softmax(s_scr[1])
        p_scr[1] = p1
        s_scr[0] = logits(jnp.minimum(j + 2, last), nw)
        emit(j + 1, nw, p_scr[1], d_odd)
        p0, d_next = softmax(s_scr[0])
        p_scr[0] = p0
        s_scr[1] = logits(jnp.minimum(j + 3, last), nw)
        return d_next

    lax.fori_loop(0, (n_blk - first) // 2, body, d0)


def _attention(proj, bias, sinks, *, n_groups, ncol, nw, q_col0, k_col0, v_col0, kv_share, dup):
    b, s, _ = proj.shape
    qw = ncol * LANES
    has_sink = sinks is not None
    kernel = functools.partial(_attn_kernel, ncol=ncol, nw=nw, dup=dup, has_sink=has_sink)
    _, n_keys, n_q = bias.shape
    in_specs = [
        pl.BlockSpec((1, s, qw), lambda bi, g, *_: (bi, 0, q_col0 // ncol + g)),
        pl.BlockSpec((1, s, LANES), lambda bi, g, *_: (bi, 0, k_col0 + g // kv_share)),
        pl.BlockSpec((1, s, LANES), lambda bi, g, *_: (bi, 0, v_col0 + g // kv_share)),
        pl.BlockSpec((1, n_keys, n_q), lambda bi, g, *_: (g, 0, 0)),
    ]
    out_spec = pl.BlockSpec((1, s, qw), lambda bi, g, *_: (bi, 0, g))
    scratch = [pltpu.VMEM((s, LANES), BF16)] if dup else []
    scratch += [pltpu.VMEM((LANES, s), BF16),
                pltpu.VMEM((2, n_keys, n_q), F32), pltpu.VMEM((2, n_keys, n_q), BF16)]
    grid_spec = pltpu.PrefetchScalarGridSpec(
        num_scalar_prefetch=1 if has_sink else 0,
        grid=(b, n_groups),
        in_specs=in_specs,
        out_specs=out_spec,
        scratch_shapes=scratch,
    )
    args = ((sinks,) if has_sink else ()) + (proj, proj, proj, bias)
    return pl.pallas_call(
        kernel,
        out_shape=jax.ShapeDtypeStruct((b, s, n_groups * qw), BF16),
        grid_spec=grid_spec,
        compiler_params=pltpu.CompilerParams(
            dimension_semantics=("arbitrary", "arbitrary"), vmem_limit_bytes=VMEM_LIMIT),
        name="attn_a" if dup else "attn_b",
    )(*args)


def _band_tables(rel_bias_b):
    qi = jnp.arange(QBLK)[:, None]
    r = qi // CHUNK

    def table(nw, n_prev, fn):
        kj = jnp.arange(nw * QBLK)[None, :]
        rel = (nw - 1) * QBLK + qi - kj
        inband = (kj >= r * CHUNK) & (kj < (r + n_prev + 1) * CHUNK)
        return jnp.where(inband[None], fn(rel), NEG_INF)

    slopes = jnp.exp2(-8.0 * jnp.arange(1, N_HEADS_A + 1, dtype=F32) / N_HEADS_A)
    nw_a = (N_PREV_A * CHUNK) // QBLK + 1
    nw_b = (N_PREV_B * CHUNK) // QBLK + 1
    bias_a = table(nw_a, N_PREV_A, lambda rel: -slopes[:, None, None] * jnp.abs(rel).astype(F32)[None])
    w_b = nw_b * QBLK
    p = QBLK + w_b - 1
    m = (np.arange(p) + QBLK - 1) % p - (QBLK - 1)
    dist = np.clip((nw_b - 1) * QBLK - m, -REL_CLIP, REL_CLIP) + REL_CLIP
    vec = rel_bias_b[:, dist].astype(F32)
    reps = -(-(QBLK * (p - 1)) // p)
    rel_b = jnp.tile(vec, (1, reps))[:, :QBLK * (p - 1)].reshape(N_HEADS_B, QBLK, p - 1)[:, :, :w_b]
    bias_b = table(nw_b, N_PREV_B, lambda rel: rel_b)
    rep = N_HEADS_A // N_KV_A
    bias_a = bias_a.reshape(N_KV_A, rep * QBLK, nw_a * QBLK).transpose(0, 2, 1)
    bias_b = bias_b.reshape(N_HEADS_B // 2, 2 * QBLK, nw_b * QBLK).transpose(0, 2, 1)
    return bias_a * LOG2E, nw_a, bias_b * LOG2E, nw_b


def _mix_kernel(oa_ref, ob_ref, x_ref, mod_ref, ga_ref, gb_ref, gf_ref, wo_ref, wr_ref, br_ref,
                x1_ref, h2_ref, rt_ref):
    half = oa_ref.shape[2]
    na = _rms(oa_ref[0].astype(F32), ga_ref[...]).astype(BF16)
    nb = _rms(ob_ref[0].astype(F32), gb_ref[...]).astype(BF16)
    acc = jnp.dot(na, wo_ref[:half, :], preferred_element_type=F32)
    acc = acc + jnp.dot(nb, wo_ref[half:, :], preferred_element_type=F32)
    x1 = x_ref[0] + mod_ref[0, 2:3, :] * acc
    x1_ref[0] = x1
    h2 = _rms(x1, gf_ref[...]) * (1.0 + mod_ref[0, 4:5, :]) + mod_ref[0, 3:4, :]
    h2_ref[0] = h2

    tm = h2.shape[0]
    h_hi = h2.astype(BF16)
    h_lo = (h2 - h_hi.astype(F32)).astype(BF16)
    r = jnp.dot(jnp.concatenate([h_hi, h_lo], axis=0), wr_ref[...], preferred_element_type=F32)
    r = r[:tm] + r[tm:]
    logits = r + pltpu.roll(r, LANES - ROUTER_LO_LANE, 1) + br_ref[...]

    lane = lax.broadcasted_iota(jnp.int32, (tm, LANES), 1)
    lane_f = lane.astype(F32)
    big = float(LANES)
    ninf = -jnp.inf

    def first_max(vals):
        top = jnp.max(vals, axis=-1, keepdims=True)
        idx = jnp.min(jnp.where(vals == top, lane_f, big), axis=-1, keepdims=True)
        return top, idx

    is_g = lane < N_GROUPS
    g_top, g_idx = first_max(jnp.where(is_g, logits, ninf))
    p_g = 1.0 / jnp.sum(jnp.where(is_g, jnp.exp(logits - g_top), 0.0), axis=-1, keepdims=True)
    lo = N_GROUPS + g_idx * EXPERTS_PER_GROUP
    e_vals = jnp.where((lane_f >= lo) & (lane_f < lo + EXPERTS_PER_GROUP), logits, ninf)
    v1, i1 = first_max(e_vals)
    v2, i2 = first_max(jnp.where(lane_f == i1, ninf, e_vals))
    e2 = jnp.exp(v2 - v1)
    w1 = p_g / (1.0 + e2)
    w2 = p_g * e2 / (1.0 + e2)
    rt = jnp.where(lane == 0, i1 - N_GROUPS,
                   jnp.where(lane == 1, i2 - N_GROUPS,
                             jnp.where(lane == 2, w1, jnp.where(lane == 3, w2, 0.0))))
    rt_ref[...] = rt


def _mix(o_a, o_b, x, mod, g_out_a, g_out_b, g_ffn, w_out_bf16, w_router, b_router):
    b, s, d = x.shape
    half = o_a.shape[2]
    tm = 256
    nt = s // tm
    vec = lambda n: pl.BlockSpec((1, n), lambda bi, i: (0, 0))
    return pl.pallas_call(
        _mix_kernel,
        out_shape=(jax.ShapeDtypeStruct((b, s, d), F32),
                   jax.ShapeDtypeStruct((b, s, d), F32),
                   jax.ShapeDtypeStruct((b * s, LANES), F32)),
        grid=(b, nt),
        in_specs=[
            pl.BlockSpec((1, tm, half), lambda bi, i: (bi, i, 0)),
            pl.BlockSpec((1, tm, half), lambda bi, i: (bi, i, 0)),
            pl.BlockSpec((1, tm, d), lambda bi, i: (bi, i, 0)),
            pl.BlockSpec((1, 6, d), lambda bi, i: (bi, 0, 0)),
            vec(half), vec(half), vec(d),
            pl.BlockSpec((d, d), lambda bi, i: (0, 0), pipeline_mode=pl.Buffered(1)),
            pl.BlockSpec((d, LANES), lambda bi, i: (0, 0)),
            vec(LANES),
        ],
        out_specs=(pl.BlockSpec((1, tm, d), lambda bi, i: (bi, i, 0)),
                   pl.BlockSpec((1, tm, d), lambda bi, i: (bi, i, 0)),
                   pl.BlockSpec((tm, LANES), lambda bi, i: (bi * nt + i, 0))),
        compiler_params=pltpu.CompilerParams(
            dimension_semantics=("arbitrary", "arbitrary"), vmem_limit_bytes=VMEM_LIMIT),
        name="mix_out",
    )(o_a, o_b, x, mod, g_out_a.reshape(1, half), g_out_b.reshape(1, half), g_ffn.reshape(1, d),
      w_out_bf16, w_router, b_router.reshape(1, LANES))


MOE_TM = 256


ROW_BUFS = 3
DISPATCH_TM = 256
DISPATCH_STEPS_MIN = ROW_BUFS


def _dispatch_kernel(slot_ref, zf_ref, h_ref, xs_hbm, buf, zbuf, sem, zsem):
    i = pl.program_id(0)
    n_steps = pl.num_programs(0)
    td = h_ref.shape[0]
    tm = zbuf.shape[0]
    n_tiles_max = xs_hbm.shape[0] // tm

    @pl.when(i == 0)
    def _():
        zbuf[...] = jnp.zeros(zbuf.shape, zbuf.dtype)

        def fill(t, carry):
            @pl.when(zf_ref[t] > 0)
            def _():
                row0 = pl.multiple_of(t * tm, tm)
                pltpu.make_async_copy(zbuf, xs_hbm.at[pl.ds(row0, tm)], zsem.at[0]).start()
            return carry
        lax.fori_loop(0, n_tiles_max, fill, 0)

        def drain(t, carry):
            @pl.when(zf_ref[t] > 0)
            def _():
                pltpu.make_async_copy(zbuf, xs_hbm.at[pl.ds(0, tm)], zsem.at[0]).wait()
            return carry
        lax.fori_loop(0, n_tiles_max, drain, 0)

    def wait_rows(s):
        for _ in range(2):
            pltpu.make_async_copy(buf.at[s], xs_hbm.at[pl.ds(0, td)], sem.at[s]).wait()

    def step(s):
        @pl.when(i >= ROW_BUFS)
        def _():
            wait_rows(s)

        buf[s] = h_ref[...]
        for r in range(td):
            for k in range(2):
                dst = slot_ref[2 * (i * td + r) + k]
                pltpu.make_async_copy(buf.at[s, pl.ds(r, 1)], xs_hbm.at[pl.ds(dst, 1)],
                                      sem.at[s]).start(priority=k)

        @pl.when(i == n_steps - 1)
        def _():
            for t in range(min(ROW_BUFS, DISPATCH_STEPS_MIN)):
                wait_rows((s + ROW_BUFS - t) % ROW_BUFS)

    for s in range(ROW_BUFS):
        pl.when(i % ROW_BUFS == s)(functools.partial(step, s))


def _dispatch(h2, slot, zero_fill, n_pad):
    t, d = h2.shape
    tm = MOE_TM
    assert t // DISPATCH_TM >= DISPATCH_STEPS_MIN
    grid_spec = pltpu.PrefetchScalarGridSpec(
        num_scalar_prefetch=2,
        grid=(t // DISPATCH_TM,),
        in_specs=[pl.BlockSpec((DISPATCH_TM, d), lambda i, *_: (i, 0))],
        out_specs=pl.BlockSpec(memory_space=pl.ANY),
        scratch_shapes=[
            pltpu.VMEM((ROW_BUFS, DISPATCH_TM, d), F32),
            pltpu.VMEM((tm, d), F32),
            pltpu.SemaphoreType.DMA((ROW_BUFS,)),
            pltpu.SemaphoreType.DMA((1,)),
        ],
    )
    return pl.pallas_call(
        _dispatch_kernel,
        out_shape=jax.ShapeDtypeStruct((n_pad, d), F32),
        grid_spec=grid_spec,
        compiler_params=pltpu.CompilerParams(
            dimension_semantics=("arbitrary",), vmem_limit_bytes=VMEM_LIMIT),
        name="dispatch",
    )(slot, zero_fill, h2)


def _moe_kernel(te_ref, nxe_ref, nt_ref, x_ref, wg_hbm, wu_hbm, wd_hbm, o_ref,
                wg32, wu32, wd32, wgb, wub, wdb, wsem):
    i = pl.program_id(0)
    n_tiles = nt_ref[0]

    def weight_copies(e):
        return [pltpu.make_async_copy(src.at[e], dst, wsem.at[0])
                for src, dst in ((wg_hbm, wg32), (wu_hbm, wu32), (wd_hbm, wd32))]

    @pl.when(i == 0)
    def _():
        for cp in weight_copies(te_ref[0]):
            cp.start()

    @pl.when(i < n_tiles)
    def _():
        @pl.when(jnp.logical_or(i == 0, te_ref[i] != te_ref[jnp.maximum(i - 1, 0)]))
        def _():
            for cp in weight_copies(0):
                cp.wait()
            wgb[...] = wg32[...].astype(BF16)
            wub[...] = wu32[...].astype(BF16)
            wdb[...] = wd32[...].astype(BF16)

            @pl.when(nxe_ref[i] >= 0)
            def _():
                for cp in weight_copies(nxe_ref[i]):
                    cp.start()

        xb = x_ref[...].astype(BF16)
        g = jnp.dot(xb, wgb[...], preferred_element_type=F32)
        u = jnp.dot(xb, wub[...], preferred_element_type=F32)
        a = (g * jax.nn.sigmoid(g) * u).astype(BF16)
        o_ref[...] = jnp.dot(a, wdb[...], preferred_element_type=F32)

    @pl.when(i >= n_tiles)
    def _():
        o_ref[...] = jnp.zeros(o_ref.shape, o_ref.dtype)


def _moe(xs, w_gate, w_up, w_down, tile_expert, next_expert, n_tiles):
    n_pad, d = xs.shape
    tm = MOE_TM
    max_tiles = n_pad // tm
    de = w_gate.shape[2]
    grid_spec = pltpu.PrefetchScalarGridSpec(
        num_scalar_prefetch=3,
        grid=(max_tiles,),
        in_specs=[pl.BlockSpec((tm, d), lambda i, te, nxe, nt: (jnp.minimum(i, nt[0] - 1), 0))]
        + [pl.BlockSpec(memory_space=pl.ANY)] * 3,
        out_specs=pl.BlockSpec((tm, d), lambda i, *_: (i, 0)),
        scratch_shapes=[
            pltpu.VMEM((d, de), F32),
            pltpu.VMEM((d, de), F32),
            pltpu.VMEM((de, d), F32),
            pltpu.VMEM((d, de), BF16),
            pltpu.VMEM((d, de), BF16),
            pltpu.VMEM((de, d), BF16),
            pltpu.SemaphoreType.DMA((1,)),
        ],
    )
    return pl.pallas_call(
        _moe_kernel,
        out_shape=jax.ShapeDtypeStruct((n_pad, d), F32),
        grid_spec=grid_spec,
        compiler_params=pltpu.CompilerParams(
            dimension_semantics=("arbitrary",), vmem_limit_bytes=VMEM_LIMIT),
        name="moe",
    )(tile_expert, next_expert, n_tiles, xs, w_gate, w_up, w_down)


def _route_plan(rt, n_tok):
    tm = MOE_TM
    n_asg = 2 * n_tok
    n_pad = n_asg + N_EXPERTS * tm
    max_tiles = n_pad // tm
    e_flat = rt[:, :2].astype(jnp.int32).reshape(-1)
    onehot = (e_flat[:, None] == jnp.arange(N_EXPERTS)[None, :]).astype(jnp.int32)
    csum = jnp.cumsum(onehot, axis=0)
    counts = csum[-1]
    rank = jnp.sum(csum * onehot, axis=1) - 1
    tiles_e = (counts + tm - 1) // tm
    tile_end = jnp.cumsum(tiles_e)
    base = (tile_end - tiles_e) * tm
    slot = jnp.sum(onehot * base[None, :], axis=1) + rank
    n_tiles = tile_end[-1]
    tile_id = jnp.arange(max_tiles)
    te = jnp.sum((tile_id[:, None] >= tile_end[None, :]).astype(jnp.int32), axis=1)
    te_last = jnp.sum(((n_tiles - 1) >= tile_end).astype(jnp.int32))
    tile_expert = jnp.where(tile_id < n_tiles, te, te_last).astype(jnp.int32)
    ex = jnp.arange(N_EXPERTS)
    later = (ex[None, :] > ex[:, None]) & (counts[None, :] > 0)
    nxt_e = jnp.min(jnp.where(later, ex[None, :], N_EXPERTS), axis=1)
    nxt_e = jnp.where(nxt_e < N_EXPERTS, nxt_e, -1)
    sel = (tile_expert[:, None] == ex[None, :]).astype(jnp.int32)
    next_expert = jnp.sum(sel * nxt_e[None, :], axis=1).astype(jnp.int32)
    last_tile = jnp.sum(sel * (tile_end - 1)[None, :], axis=1)
    zero_fill = ((tile_id >= n_tiles) | (tile_id == last_tile)).astype(jnp.int32)
    return (tile_expert, next_expert, n_tiles.reshape(1).astype(jnp.int32), zero_fill,
            slot.astype(jnp.int32), n_pad)


FINAL_TM = 256


def _final_kernel(slot_ref, x1_ref, rt_ref, mod_ref, modf_ref, g_ref, y_hbm, o_ref, ybuf, gsem):
    i = pl.program_id(0)
    n_steps = pl.num_programs(0)
    tm = x1_ref.shape[0]

    def row_in(tile, r, k, s):
        idx = slot_ref[2 * (tile * tm + r) + k]
        pltpu.make_async_copy(y_hbm.at[pl.ds(idx, 1)], ybuf.at[s, pl.ds(k * tm + r, 1)],
                              gsem.at[s]).start(priority=k)

    def wait_in(s):
        pltpu.make_async_copy(y_hbm.at[pl.ds(0, 2 * tm)], ybuf.at[s], gsem.at[s]).wait()

    @pl.when(i == 0)
    def _():
        for t in range(ROW_BUFS - 1):
            tile = jnp.minimum(t, n_steps - 1)

            def row(r, carry):
                row_in(tile, r, 0, t)
                row_in(tile, r, 1, t)
                return carry
            lax.fori_loop(0, tm, row, 0, unroll=8)

    def step(s):
        wait_in(s)
        rt = rt_ref[...]
        y = rt[:, 2:3] * ybuf[s, :tm, :] + rt[:, 3:4] * ybuf[s, tm:, :]
        nxt = jnp.minimum(i + ROW_BUFS - 1, n_steps - 1)
        for r in range(tm):
            row_in(nxt, r, 0, (s + ROW_BUFS - 1) % ROW_BUFS)
            row_in(nxt, r, 1, (s + ROW_BUFS - 1) % ROW_BUFS)
        x2 = x1_ref[...] + mod_ref[0, 5:6, :] * y
        o_ref[...] = _rms(x2, g_ref[...]) * (1.0 + modf_ref[0, 1:2, :]) + modf_ref[0, 0:1, :]

        @pl.when(i == n_steps - 1)
        def _():
            for t in range(1, ROW_BUFS):
                wait_in((s + t) % ROW_BUFS)

    for s in range(ROW_BUFS):
        pl.when(i % ROW_BUFS == s)(functools.partial(step, s))


def _final(x1, y_rows, slot, rt, mod, modf, g_final):
    b, s, d = x1.shape
    tm = FINAL_TM
    nt = s // tm
    grid_spec = pltpu.PrefetchScalarGridSpec(
        num_scalar_prefetch=1,
        grid=(b * nt,),
        in_specs=[
            pl.BlockSpec((tm, d), lambda i, *_: (i, 0)),
            pl.BlockSpec((tm, LANES), lambda i, *_: (i, 0)),
            pl.BlockSpec((1, 6, d), lambda i, *_: (i // nt, 0, 0)),
            pl.BlockSpec((1, 2, d), lambda i, *_: (i // nt, 0, 0)),
            pl.BlockSpec((1, d), lambda i, *_: (0, 0)),
            pl.BlockSpec(memory_space=pl.ANY),
        ],
        out_specs=pl.BlockSpec((tm, d), lambda i, *_: (i, 0)),
        scratch_shapes=[
            pltpu.VMEM((ROW_BUFS, 2 * tm, d), F32),
            pltpu.SemaphoreType.DMA((ROW_BUFS,)),
        ],
    )
    out = pl.pallas_call(
        _final_kernel,
        out_shape=jax.ShapeDtypeStruct((b * s, d), F32),
        grid_spec=grid_spec,
        compiler_params=pltpu.CompilerParams(
            dimension_semantics=("arbitrary",), vmem_limit_bytes=VMEM_LIMIT),
        name="final",
    )(slot, x1.reshape(b * s, d), rt, mod, modf, g_final.reshape(1, d), y_rows)
    return out.reshape(b, s, d)


def kernel(x, c, w_ada, b_ada, g_mix, w_in, sinks_a, rel_bias_b, g_out_a, g_out_b, w_out, g_ffn,
           w_router_group, b_router_group, w_router_expert, b_router_expert, w_gate, w_up, w_down,
           w_ada_final, b_ada_final, g_final):
    b, s, d = x.shape
    assert w_ada.shape[0] == 1, "one layer"
    n_tok = b * s

    c_act = jax.nn.silu(c)
    a_rep = jnp.broadcast_to(c_act[:, :, None], (b, d, LANES))
    mod = _ada(a_rep, w_ada[0], b_ada[0]).reshape(b, 6, d)
    modf = _ada(a_rep, w_ada_final, b_ada_final).reshape(b, 2, d)

    proj = _proj(x, mod, g_mix[0], w_in[0].astype(BF16))

    bias_a, nw_a, bias_b, nw_b = _band_tables(rel_bias_b[0])
    kv_a0 = DA_Q // LANES
    o_a = _attention(proj, bias_a, sinks_a[0].astype(F32), n_groups=N_KV_A, ncol=2, nw=nw_a,
                     q_col0=0, k_col0=kv_a0, v_col0=kv_a0 + DA_KV // LANES, kv_share=2, dup=True)
    qb0 = (DA_Q + 2 * DA_KV) // LANES
    o_b = _attention(proj, bias_b, None, n_groups=N_HEADS_B // 2, ncol=1, nw=nw_b,
                     q_col0=qb0, k_col0=qb0 + DB // LANES, v_col0=qb0 + 2 * DB // LANES,
                     kv_share=1, dup=False)

    n_r = N_GROUPS + N_EXPERTS
    assert n_r <= ROUTER_LO_LANE
    w_r = jnp.concatenate([w_router_group[0], w_router_expert[0]], axis=1)
    w_r_hi = w_r.astype(BF16)
    w_r_lo = (w_r - w_r_hi.astype(F32)).astype(BF16)
    w_router = (jnp.zeros((d, LANES), BF16).at[:, :n_r].set(w_r_hi)
                .at[:, ROUTER_LO_LANE:ROUTER_LO_LANE + n_r].set(w_r_lo))
    b_router = jnp.zeros((LANES,), F32).at[:n_r].set(
        jnp.concatenate([b_router_group[0], b_router_expert[0]]))
    x1, h2, rt = _mix(o_a, o_b, x, mod, g_out_a[0], g_out_b[0], g_ffn[0], w_out[0].astype(BF16),
                      w_router, b_router)

    tile_expert, next_expert, n_tiles, zero_fill, slot, n_pad = _route_plan(rt, n_tok)
    xs = _dispatch(h2.reshape(n_tok, d), slot, zero_fill, n_pad)
    y = _moe(xs, w_gate[0], w_up[0], w_down[0], tile_expert, next_expert, n_tiles)

    return _final(x1, y, slot, rt, mod, modf, g_final)
```

```python
import functools

import jax
import jax.numpy as jnp
import numpy as np
from jax import lax
from jax.experimental import pallas as pl
from jax.experimental.pallas import tpu as pltpu

D_MODEL = 2048
CHUNK = 64
HEAD_DIM = 64
N_HEADS_A = 16
N_KV_A = 4
N_PREV_A = 2
N_HEADS_B = 16
N_PREV_B = 8
REL_CLIP = 128
DA_Q = N_HEADS_A * HEAD_DIM
DA_KV = N_KV_A * HEAD_DIM
DB = N_HEADS_B * HEAD_DIM
D_IN = DA_Q + 2 * DA_KV + 3 * DB
N_GROUPS = 4
EXPERTS_PER_GROUP = 8
N_EXPERTS = N_GROUPS * EXPERTS_PER_GROUP
D_EXPERT = D_MODEL // 4
EPS = 1e-6
NEG_INF = -1e30
LOG2E = 1.4426950408889634

LANES = 128
ROUTER_LO_LANE = 64
QBLK = 2 * CHUNK
VMEM_LIMIT = 56 * 1024 * 1024

F32 = jnp.float32
BF16 = jnp.bfloat16


def _rms(x, g):
    return x * lax.rsqrt(jnp.mean(x * x, axis=-1, keepdims=True) + EPS) * g


ADA_JB = 8


def _ada_kernel(a_ref, w_ref, b_ref, o_ref):
    n_b, k, _ = a_ref.shape
    tn = w_ref.shape[1]
    sub = 8
    for jb in range(tn // (ADA_JB * LANES)):
        col0 = jb * ADA_JB * LANES

        def body(kc, accs):
            k0 = pl.multiple_of(kc * sub, sub)
            a_rows = [a_ref[b, pl.ds(k0, sub), :] for b in range(n_b)]
            out = []
            for j in range(ADA_JB):
                w = w_ref[pl.ds(k0, sub), col0 + j * LANES:col0 + (j + 1) * LANES]
                out.append([accs[j][b] + a_rows[b] * w for b in range(n_b)])
            return out

        zero = jnp.zeros((sub, LANES), F32)
        accs = lax.fori_loop(0, k // sub, body, [[zero] * n_b for _ in range(ADA_JB)], unroll=4)
        for j in range(ADA_JB):
            cols = slice(col0 + j * LANES, col0 + (j + 1) * LANES)
            for b in range(n_b):
                o_ref[b:b + 1, cols] = jnp.sum(accs[j][b], axis=0, keepdims=True) + b_ref[:, cols]


def _ada(a_rep, w, bias):
    n_b, k, _ = a_rep.shape
    n = w.shape[1]
    tn = 2048
    return pl.pallas_call(
        _ada_kernel,
        out_shape=jax.ShapeDtypeStruct((n_b, n), F32),
        grid=(n // tn,),
        in_specs=[
            pl.BlockSpec((n_b, k, LANES), lambda j: (0, 0, 0)),
            pl.BlockSpec((k, tn), lambda j: (0, j)),
            pl.BlockSpec((1, tn), lambda j: (0, j)),
        ],
        out_specs=pl.BlockSpec((n_b, tn), lambda j: (0, j)),
        compiler_params=pltpu.CompilerParams(
            dimension_semantics=("arbitrary",), vmem_limit_bytes=VMEM_LIMIT),
        name="ada",
    )(a_rep, w, bias.reshape(1, n))


def _proj_kernel(x_ref, mod_ref, g_ref, w_ref, o_ref, *, n_chunk):
    h = _rms(x_ref[0], g_ref[...])
    h = h * (1.0 + mod_ref[0, 1:2, :]) + mod_ref[0, 0:1, :]
    hb = h.astype(BF16)
    for n0 in range(0, o_ref.shape[2], n_chunk):
        cols = slice(n0, n0 + n_chunk)
        o_ref[0, :, cols] = jnp.dot(hb, w_ref[:, cols], preferred_element_type=F32).astype(BF16)


def _proj(x, mod, g_mix, w_in_bf16):
    b, s, d = x.shape
    n = w_in_bf16.shape[1]
    tm = 512
    return pl.pallas_call(
        functools.partial(_proj_kernel, n_chunk=512),
        out_shape=jax.ShapeDtypeStruct((b, s, n), BF16),
        grid=(b, s // tm),
        in_specs=[
            pl.BlockSpec((1, tm, d), lambda bi, i: (bi, i, 0)),
            pl.BlockSpec((1, 6, d), lambda bi, i: (bi, 0, 0)),
            pl.BlockSpec((1, d), lambda bi, i: (0, 0)),
            pl.BlockSpec((d, n), lambda bi, i: (0, 0), pipeline_mode=pl.Buffered(1)),
        ],
        out_specs=pl.BlockSpec((1, tm, n), lambda bi, i: (bi, i, 0)),
        compiler_params=pltpu.CompilerParams(
            dimension_semantics=("arbitrary", "arbitrary"), vmem_limit_bytes=VMEM_LIMIT),
        name="proj",
    )(x, mod, g_mix.reshape(1, d), w_in_bf16)


def _attn_kernel(*refs, ncol, nw, dup, has_sink):
    if has_sink:
        sink_ref, q_ref, k_ref, v_ref, bias_ref, o_ref = refs[:6]
        scratch = refs[6:]
    else:
        q_ref, k_ref, v_ref, bias_ref, o_ref = refs[:5]
        scratch = refs[5:]
        sink_ref = None
    s_len = q_ref.shape[1]
    n_blk = s_len // QBLK
    n_stack = 2 * ncol
    grp = pl.program_id(1)

    rows = 512
    lane = lax.broadcasted_iota(jnp.int32, (QBLK, LANES), 1)
    low = lane < HEAD_DIM
    if dup:
        kd_ref, vt_ref = scratch[:2]
        half = grp % 2
        keep = (lax.broadcasted_iota(jnp.int32, (rows, LANES), 1) // HEAD_DIM) == half

        def spread(t):
            return jnp.where(keep, t, pltpu.roll(t, HEAD_DIM, 1))

        def dup_body(c, carry):
            r0 = pl.multiple_of(c * rows, rows)
            kd_ref[pl.ds(r0, rows), :] = spread(k_ref[0, pl.ds(r0, rows), :].astype(F32)).astype(BF16)
            return carry

        lax.fori_loop(0, s_len // rows, dup_body, 0)
        k_src = kd_ref
    else:
        vt_ref = scratch[0]
        k_src = k_ref.at[0]

        def spread(t):
            return t

    for c in range(s_len // rows):
        t = spread(v_ref[0, c * rows:(c + 1) * rows, :].astype(F32))
        vt_ref[:, c * rows:(c + 1) * rows] = t.T.astype(BF16)

    top = lax.broadcasted_iota(jnp.int32, (LANES, QBLK), 0) < HEAD_DIM

    if has_sink:
        sink = jnp.concatenate(
            [jnp.full((1, QBLK), sink_ref[grp * n_stack + h] * LOG2E, F32) for h in range(n_stack)],
            axis=1)

    def rows_of(j):
        return j * QBLK if isinstance(j, int) else pl.multiple_of(j * QBLK, QBLK)

    def logits(j, nvb):
        qf = q_ref[0, pl.ds(rows_of(j), QBLK), :].astype(F32) * (HEAD_DIM ** -0.5 * LOG2E)
        parts = []
        for c in range(ncol):
            qc = qf[:, c * LANES:(c + 1) * LANES]
            parts.append(jnp.where(low, qc, 0.0))
            parts.append(jnp.where(low, 0.0, qc))
        lhs = jnp.concatenate(parts, axis=0).astype(BF16)
        kw = k_src[pl.ds(rows_of(j - (nvb - 1)), nvb * QBLK), :]
        s = lax.dot_general(kw, lhs, (((1,), (1,)), ((), ())), preferred_element_type=F32)
        return s + bias_ref[0, (nw - nvb) * QBLK:, :]

    def softmax(s):
        m = jnp.max(s, axis=0, keepdims=True)
        if has_sink:
            m = jnp.maximum(m, sink)
        p = jnp.exp2(s - m)
        denom = jnp.sum(p, axis=0, keepdims=True)
        if has_sink:
            denom = denom + jnp.exp2(sink - m)
        return p.astype(BF16), denom

    def emit(j, nvb, p, denom):
        vw = vt_ref[:, pl.ds(rows_of(j - (nvb - 1)), nvb * QBLK)]
        o = jnp.dot(vw, p, preferred_element_type=F32) / denom
        for c in range(ncol):
            o0 = o[:, (2 * c) * QBLK:(2 * c + 1) * QBLK]
            o1 = o[:, (2 * c + 1) * QBLK:(2 * c + 2) * QBLK]
            o_ref[0, pl.ds(rows_of(j), QBLK), c * LANES:(c + 1) * LANES] = (
                jnp.where(top, o0, o1).T.astype(BF16))

    first = nw - 1
    for j in range(first + (n_blk - first) % 2):
        nvb = min(j + 1, nw)
        emit(j, nvb, *softmax(logits(j, nvb)))
    first += (n_blk - first) % 2

    s_scr, p_scr = scratch[-2:]
    last = n_blk - 1
    s_scr[0] = logits(first, nw)
    p0, d0 = softmax(s_scr[0])
    p_scr[0] = p0
    s_scr[1] = logits(first + 1, nw)

    def body(t, d_even):
        j = first + 2 * t
        emit(j, nw, p_scr[0], d_even)
        p1, d_odd = softmax(s_scr[1])
        p_scr[1] = p1
        s_scr[0] = logits(jnp.minimum(j + 2, last), nw)
        emit(j + 1, nw, p_scr[1], d_odd)
        p0, d_next = softmax(s_scr[0])
        p_scr[0] = p0
        s_scr[1] = logits(jnp.minimum(j + 3, last), nw)
        return d_next

    lax.fori_loop(0, (n_blk - first) // 2, body, d0)


def _attention(proj, bias, sinks, *, n_groups, ncol, nw, q_col0, k_col0, v_col0, kv_share, dup):
    b, s, _ = proj.shape
    qw = ncol * LANES
    has_sink = sinks is not None
    kernel = functools.partial(_attn_kernel, ncol=ncol, nw=nw, dup=dup, has_sink=has_sink)
    _, n_keys, n_q = bias.shape
    in_specs = [
        pl.BlockSpec((1, s, qw), lambda bi, g, *_: (bi, 0, q_col0 // ncol + g)),
        pl.BlockSpec((1, s, LANES), lambda bi, g, *_: (bi, 0, k_col0 + g // kv_share)),
        pl.BlockSpec((1, s, LANES), lambda bi, g, *_: (bi, 0, v_col0 + g // kv_share)),
        pl.BlockSpec((1, n_keys, n_q), lambda bi, g, *_: (g, 0, 0)),
    ]
    out_spec = pl.BlockSpec((1, s, qw), lambda bi, g, *_: (bi, 0, g))
    scratch = [pltpu.VMEM((s, LANES), BF16)] if dup else []
    scratch += [pltpu.VMEM((LANES, s), BF16),
                pltpu.VMEM((2, n_keys, n_q), F32), pltpu.VMEM((2, n_keys, n_q), BF16)]
    grid_spec = pltpu.PrefetchScalarGridSpec(
        num_scalar_prefetch=1 if has_sink else 0,
        grid=(b, n_groups),
        in_specs=in_specs,
        out_specs=out_spec,
        scratch_shapes=scratch,
    )
    args = ((sinks,) if has_sink else ()) + (proj, proj, proj, bias)
    return pl.pallas_call(
        kernel,
        out_shape=jax.ShapeDtypeStruct((b, s, n_groups * qw), BF16),
        grid_spec=grid_spec,
        compiler_params=pltpu.CompilerParams(
            dimension_semantics=("arbitrary", "arbitrary"), vmem_limit_bytes=VMEM_LIMIT),
        name="attn_a" if dup else "attn_b",
    )(*args)


def _band_tables(rel_bias_b):
    qi = jnp.arange(QBLK)[:, None]
    r = qi // CHUNK

    def table(nw, n_prev, fn):
        kj = jnp.arange(nw * QBLK)[None, :]
        rel = (nw - 1) * QBLK + qi - kj
        inband = (kj >= r * CHUNK) & (kj < (r + n_prev + 1) * CHUNK)
        return jnp.where(inband[None], fn(rel), NEG_INF)

    slopes = jnp.exp2(-8.0 * jnp.arange(1, N_HEADS_A + 1, dtype=F32) / N_HEADS_A)
    nw_a = (N_PREV_A * CHUNK) // QBLK + 1
    nw_b = (N_PREV_B * CHUNK) // QBLK + 1
    bias_a = table(nw_a, N_PREV_A, lambda rel: -slopes[:, None, None] * jnp.abs(rel).astype(F32)[None])
    w_b = nw_b * QBLK
    row = -(-(QBLK + w_b - 2) // LANES) * LANES
    p = row + 1
    t = np.arange(p)
    m = np.where(t < w_b, t, t - p)
    dist = np.clip((nw_b - 1) * QBLK - m, -REL_CLIP, REL_CLIP) + REL_CLIP
    vec = rel_bias_b[:, dist].astype(F32)
    reps = -(-(QBLK * row) // p)
    rel_b = jnp.tile(vec, (1, reps))[:, :QBLK * row].reshape(N_HEADS_B, QBLK, row)[:, :, :w_b]
    bias_b = table(nw_b, N_PREV_B, lambda rel: rel_b)
    rep = N_HEADS_A // N_KV_A
    bias_a = bias_a.reshape(N_KV_A, rep * QBLK, nw_a * QBLK).transpose(0, 2, 1)
    bias_b = bias_b.reshape(N_HEADS_B // 2, 2 * QBLK, nw_b * QBLK).transpose(0, 2, 1)
    return bias_a * LOG2E, nw_a, bias_b * LOG2E, nw_b


def _mix_kernel(oa_ref, ob_ref, x_ref, mod_ref, ga_ref, gb_ref, gf_ref, wo_ref, wr_ref, br_ref,
                x1_ref, h2_ref, rt_ref, ids_ref):
    half = oa_ref.shape[2]
    na = _rms(oa_ref[0].astype(F32), ga_ref[...]).astype(BF16)
    nb = _rms(ob_ref[0].astype(F32), gb_ref[...]).astype(BF16)
    acc = jnp.dot(na, wo_ref[:half, :], preferred_element_type=F32)
    acc = acc + jnp.dot(nb, wo_ref[half:, :], preferred_element_type=F32)
    x1 = x_ref[0] + mod_ref[0, 2:3, :] * acc
    x1_ref[0] = x1
    h2 = _rms(x1, gf_ref[...]) * (1.0 + mod_ref[0, 4:5, :]) + mod_ref[0, 3:4, :]
    h2_ref[0] = h2

    tm = h2.shape[0]
    h_hi = h2.astype(BF16)
    h_lo = (h2 - h_hi.astype(F32)).astype(BF16)
    r = jnp.dot(jnp.concatenate([h_hi, h_lo], axis=0), wr_ref[...], preferred_element_type=F32)
    r = r[:tm] + r[tm:]
    logits = r + pltpu.roll(r, LANES - ROUTER_LO_LANE, 1) + br_ref[...]

    lane = lax.broadcasted_iota(jnp.int32, (tm, LANES), 1)
    lane_f = lane.astype(F32)
    big = float(LANES)
    ninf = -jnp.inf

    def first_max(vals):
        top = jnp.max(vals, axis=-1, keepdims=True)
        idx = jnp.min(jnp.where(vals == top, lane_f, big), axis=-1, keepdims=True)
        return top, idx

    is_g = lane < N_GROUPS
    g_top, g_idx = first_max(jnp.where(is_g, logits, ninf))
    p_g = 1.0 / jnp.sum(jnp.where(is_g, jnp.exp(logits - g_top), 0.0), axis=-1, keepdims=True)
    lo = N_GROUPS + g_idx * EXPERTS_PER_GROUP
    e_vals = jnp.where((lane_f >= lo) & (lane_f < lo + EXPERTS_PER_GROUP), logits, ninf)
    v1, i1 = first_max(e_vals)
    v2, i2 = first_max(jnp.where(lane_f == i1, ninf, e_vals))
    e2 = jnp.exp(v2 - v1)
    w1 = p_g / (1.0 + e2)
    w2 = p_g * e2 / (1.0 + e2)
    rt = jnp.where(lane == 0, i1 - N_GROUPS,
                   jnp.where(lane == 1, i2 - N_GROUPS,
                             jnp.where(lane == 2, w1, jnp.where(lane == 3, w2, 0.0))))
    rt_ref[...] = rt
    pick = (lax.broadcasted_iota(jnp.int32, (8, LANES), 0)
            == lax.broadcasted_iota(jnp.int32, (8, LANES), 1)).astype(BF16)
    ids_ref[...] = lax.dot_general(pick, rt.astype(BF16), (((1,), (1,)), ((), ())),
                                   preferred_element_type=F32)


def _mix(o_a, o_b, x, mod, g_out_a, g_out_b, g_ffn, w_out_bf16, w_router, b_router):
    b, s, d = x.shape
    half = o_a.shape[2]
    tm = 512
    nt = s // tm
    vec = lambda n: pl.BlockSpec((1, n), lambda bi, i: (0, 0))
    return pl.pallas_call(
        _mix_kernel,
        out_shape=(jax.ShapeDtypeStruct((b, s, d), F32),
                   jax.ShapeDtypeStruct((b, s, d), F32),
                   jax.ShapeDtypeStruct((b * s, LANES), F32),
                   jax.ShapeDtypeStruct((8, b * s), F32)),
        grid=(b, nt),
        in_specs=[
            pl.BlockSpec((1, tm, half), lambda bi, i: (bi, i, 0)),
            pl.BlockSpec((1, tm, half), lambda bi, i: (bi, i, 0)),
            pl.BlockSpec((1, tm, d), lambda bi, i: (bi, i, 0)),
            pl.BlockSpec((1, 6, d), lambda bi, i: (bi, 0, 0)),
            vec(half), vec(half), vec(d),
            pl.BlockSpec((d, d), lambda bi, i: (0, 0), pipeline_mode=pl.Buffered(1)),
            pl.BlockSpec((d, LANES), lambda bi, i: (0, 0)),
            vec(LANES),
        ],
        out_specs=(pl.BlockSpec((1, tm, d), lambda bi, i: (bi, i, 0)),
                   pl.BlockSpec((1, tm, d), lambda bi, i: (bi, i, 0)),
                   pl.BlockSpec((tm, LANES), lambda bi, i: (bi * nt + i, 0)),
                   pl.BlockSpec((8, tm), lambda bi, i: (0, bi * nt + i))),
        compiler_params=pltpu.CompilerParams(
            dimension_semantics=("arbitrary", "arbitrary"), vmem_limit_bytes=VMEM_LIMIT),
        name="mix_out",
    )(o_a, o_b, x, mod, g_out_a.reshape(1, half), g_out_b.reshape(1, half), g_ffn.reshape(1, d),
      w_out_bf16, w_router, b_router.reshape(1, LANES))


MOE_TM = 256


ROW_BUFS = 3
DISPATCH_TM = 256
DISPATCH_STEPS_MIN = ROW_BUFS


def _dispatch_kernel(slot_ref, zf_ref, h_ref, xs_hbm, buf, zbuf, sem, zsem):
    i = pl.program_id(0)
    n_steps = pl.num_programs(0)
    td = h_ref.shape[0]
    tm = zbuf.shape[0]
    n_tiles_max = xs_hbm.shape[0] // tm

    @pl.when(i == 0)
    def _():
        zbuf[...] = jnp.zeros(zbuf.shape, zbuf.dtype)

        def fill(t, carry):
            @pl.when(zf_ref[t] > 0)
            def _():
                row0 = pl.multiple_of(t * tm, tm)
                pltpu.make_async_copy(zbuf, xs_hbm.at[pl.ds(row0, tm)], zsem.at[0]).start()
            return carry
        lax.fori_loop(0, n_tiles_max, fill, 0)

        def drain(t, carry):
            @pl.when(zf_ref[t] > 0)
            def _():
                pltpu.make_async_copy(zbuf, xs_hbm.at[pl.ds(0, tm)], zsem.at[0]).wait()
            return carry
        lax.fori_loop(0, n_tiles_max, drain, 0)

    def wait_rows(s):
        for _ in range(2):
            pltpu.make_async_copy(buf.at[s], xs_hbm.at[pl.ds(0, td)], sem.at[s]).wait()

    def step(s):
        @pl.when(i >= ROW_BUFS)
        def _():
            wait_rows(s)

        buf[s] = h_ref[...]
        for r in range(td):
            for k in range(2):
                dst = slot_ref[k * (n_steps * td) + i * td + r]
                pltpu.make_async_copy(buf.at[s, pl.ds(r, 1)], xs_hbm.at[pl.ds(dst, 1)],
                                      sem.at[s]).start(priority=k)

        @pl.when(i == n_steps - 1)
        def _():
            for t in range(min(ROW_BUFS, DISPATCH_STEPS_MIN)):
                wait_rows((s + ROW_BUFS - t) % ROW_BUFS)

    for s in range(ROW_BUFS):
        pl.when(i % ROW_BUFS == s)(functools.partial(step, s))


def _dispatch(h2, slot, zero_fill, n_pad):
    t, d = h2.shape
    tm = MOE_TM
    assert t // DISPATCH_TM >= DISPATCH_STEPS_MIN
    grid_spec = pltpu.PrefetchScalarGridSpec(
        num_scalar_prefetch=2,
        grid=(t // DISPATCH_TM,),
        in_specs=[pl.BlockSpec((DISPATCH_TM, d), lambda i, *_: (i, 0))],
        out_specs=pl.BlockSpec(memory_space=pl.ANY),
        scratch_shapes=[
            pltpu.VMEM((ROW_BUFS, DISPATCH_TM, d), F32),
            pltpu.VMEM((tm, d), F32),
            pltpu.SemaphoreType.DMA((ROW_BUFS,)),
            pltpu.SemaphoreType.DMA((1,)),
        ],
    )
    return pl.pallas_call(
        _dispatch_kernel,
        out_shape=jax.ShapeDtypeStruct((n_pad, d), F32),
        grid_spec=grid_spec,
        compiler_params=pltpu.CompilerParams(
            dimension_semantics=("arbitrary",), vmem_limit_bytes=VMEM_LIMIT),
        name="dispatch",
    )(slot, zero_fill, h2)


def _moe_kernel(te_ref, nxe_ref, nt_ref, x_ref, wg_hbm, wu_hbm, wd_hbm, o_ref,
                wg32, wu32, wd32, wgb, wub, wdb, wsem):
    i = pl.program_id(0)
    n_tiles = nt_ref[0]

    def weight_copies(e):
        return [pltpu.make_async_copy(src.at[e], dst, wsem.at[0])
                for src, dst in ((wg_hbm, wg32), (wu_hbm, wu32), (wd_hbm, wd32))]

    @pl.when(i == 0)
    def _():
        for cp in weight_copies(te_ref[0]):
            cp.start()

    @pl.when(i < n_tiles)
    def _():
        @pl.when(jnp.logical_or(i == 0, te_ref[i] != te_ref[jnp.maximum(i - 1, 0)]))
        def _():
            for cp in weight_copies(0):
                cp.wait()
            wgb[...] = wg32[...].astype(BF16)
            wub[...] = wu32[...].astype(BF16)
            wdb[...] = wd32[...].astype(BF16)

            @pl.when(nxe_ref[i] >= 0)
            def _():
                for cp in weight_copies(nxe_ref[i]):
                    cp.start()

        xb = x_ref[...].astype(BF16)
        g = jnp.dot(xb, wgb[...], preferred_element_type=F32)
        u = jnp.dot(xb, wub[...], preferred_element_type=F32)
        a = (g * jax.nn.sigmoid(g) * u).astype(BF16)
        o_ref[...] = jnp.dot(a, wdb[...], preferred_element_type=F32)

    @pl.when(i >= n_tiles)
    def _():
        o_ref[...] = jnp.zeros(o_ref.shape, o_ref.dtype)


def _moe(xs, w_gate, w_up, w_down, tile_expert, next_expert, n_tiles):
    n_pad, d = xs.shape
    tm = MOE_TM
    max_tiles = n_pad // tm
    de = w_gate.shape[2]
    grid_spec = pltpu.PrefetchScalarGridSpec(
        num_scalar_prefetch=3,
        grid=(max_tiles,),
        in_specs=[pl.BlockSpec((tm, d), lambda i, te, nxe, nt: (jnp.minimum(i, nt[0] - 1), 0))]
        + [pl.BlockSpec(memory_space=pl.ANY)] * 3,
        out_specs=pl.BlockSpec((tm, d), lambda i, *_: (i, 0)),
        scratch_shapes=[
            pltpu.VMEM((d, de), F32),
            pltpu.VMEM((d, de), F32),
            pltpu.VMEM((de, d), F32),
            pltpu.VMEM((d, de), BF16),
            pltpu.VMEM((d, de), BF16),
            pltpu.VMEM((de, d), BF16),
            pltpu.SemaphoreType.DMA((1,)),
        ],
    )
    return pl.pallas_call(
        _moe_kernel,
        out_shape=jax.ShapeDtypeStruct((n_pad, d), F32),
        grid_spec=grid_spec,
        compiler_params=pltpu.CompilerParams(
            dimension_semantics=("arbitrary",), vmem_limit_bytes=VMEM_LIMIT),
        name="moe",
    )(tile_expert, next_expert, n_tiles, xs, w_gate, w_up, w_down)


def _route_plan(ids, n_tok):
    tm = MOE_TM
    n_asg = 2 * n_tok
    n_pad = n_asg + N_EXPERTS * tm
    max_tiles = n_pad // tm
    e_flat = ids[:2].astype(jnp.int32).reshape(-1)
    onehot = e_flat[None, :] == jnp.arange(N_EXPERTS)[:, None]
    blk = LANES
    oh3 = onehot.reshape(N_EXPERTS, n_asg // blk, blk).astype(BF16)
    upper = (jnp.arange(blk)[:, None] <= jnp.arange(blk)[None, :]).astype(BF16)
    within = jnp.einsum("ebj,jk->ebk", oh3, upper, preferred_element_type=F32).astype(jnp.int32)
    blk_tot = within[:, :, -1]
    offs = jnp.cumsum(blk_tot, axis=1) - blk_tot
    csum = (within + offs[:, :, None]).reshape(N_EXPERTS, n_asg)
    counts = jnp.sum(blk_tot, axis=1)
    oh_i = onehot.astype(jnp.int32)
    rank = jnp.sum(csum * oh_i, axis=0) - 1
    tiles_e = (counts + tm - 1) // tm
    tile_end = jnp.cumsum(tiles_e)
    base = (tile_end - tiles_e) * tm
    slot = jnp.sum(oh_i * base[:, None], axis=0) + rank
    n_tiles = tile_end[-1]
    tile_id = jnp.arange(max_tiles)
    te = jnp.sum((tile_id[:, None] >= tile_end[None, :]).astype(jnp.int32), axis=1)
    te_last = jnp.sum(((n_tiles - 1) >= tile_end).astype(jnp.int32))
    tile_expert = jnp.where(tile_id < n_tiles, te, te_last).astype(jnp.int32)
    ex = jnp.arange(N_EXPERTS)
    later = (ex[None, :] > ex[:, None]) & (counts[None, :] > 0)
    nxt_e = jnp.min(jnp.where(later, ex[None, :], N_EXPERTS), axis=1)
    nxt_e = jnp.where(nxt_e < N_EXPERTS, nxt_e, -1)
    sel = (tile_expert[:, None] == ex[None, :]).astype(jnp.int32)
    next_expert = jnp.sum(sel * nxt_e[None, :], axis=1).astype(jnp.int32)
    last_tile = jnp.sum(sel * (tile_end - 1)[None, :], axis=1)
    zero_fill = ((tile_id >= n_tiles) | (tile_id == last_tile)).astype(jnp.int32)
    return (tile_expert, next_expert, n_tiles.reshape(1).astype(jnp.int32), zero_fill,
            slot.astype(jnp.int32), n_pad)


FINAL_TM = 256


def _final_kernel(slot_ref, x1_ref, rt_ref, mod_ref, modf_ref, g_ref, y_hbm, o_ref, ybuf, gsem):
    i = pl.program_id(0)
    n_steps = pl.num_programs(0)
    tm = x1_ref.shape[0]

    def row_in(tile, r, k, s):
        idx = slot_ref[k * (n_steps * tm) + tile * tm + r]
        pltpu.make_async_copy(y_hbm.at[pl.ds(idx, 1)], ybuf.at[s, pl.ds(k * tm + r, 1)],
                              gsem.at[s]).start(priority=k)

    def wait_in(s):
        pltpu.make_async_copy(y_hbm.at[pl.ds(0, 2 * tm)], ybuf.at[s], gsem.at[s]).wait()

    @pl.when(i == 0)
    def _():
        for t in range(ROW_BUFS - 1):
            tile = jnp.minimum(t, n_steps - 1)

            def row(r, carry):
                row_in(tile, r, 0, t)
                row_in(tile, r, 1, t)
                return carry
            lax.fori_loop(0, tm, row, 0, unroll=8)

    def step(s):
        wait_in(s)
        rt = rt_ref[...]
        y = rt[:, 2:3] * ybuf[s, :tm, :] + rt[:, 3:4] * ybuf[s, tm:, :]
        nxt = jnp.minimum(i + ROW_BUFS - 1, n_steps - 1)
        for r in range(tm):
            row_in(nxt, r, 0, (s + ROW_BUFS - 1) % ROW_BUFS)
            row_in(nxt, r, 1, (s + ROW_BUFS - 1) % ROW_BUFS)
        x2 = x1_ref[...] + mod_ref[0, 5:6, :] * y
        o_ref[...] = _rms(x2, g_ref[...]) * (1.0 + modf_ref[0, 1:2, :]) + modf_ref[0, 0:1, :]

        @pl.when(i == n_steps - 1)
        def _():
            for t in range(1, ROW_BUFS):
                wait_in((s + t) % ROW_BUFS)

    for s in range(ROW_BUFS):
        pl.when(i % ROW_BUFS == s)(functools.partial(step, s))


def _final(x1, y_rows, slot, rt, mod, modf, g_final):
    b, s, d = x1.shape
    tm = FINAL_TM
    nt = s // tm
    grid_spec = pltpu.PrefetchScalarGridSpec(
        num_scalar_prefetch=1,
        grid=(b * nt,),
        in_specs=[
            pl.BlockSpec((tm, d), lambda i, *_: (i, 0)),
            pl.BlockSpec((tm, LANES), lambda i, *_: (i, 0)),
            pl.BlockSpec((1, 6, d), lambda i, *_: (i // nt, 0, 0)),
            pl.BlockSpec((1, 2, d), lambda i, *_: (i // nt, 0, 0)),
            pl.BlockSpec((1, d), lambda i, *_: (0, 0)),
            pl.BlockSpec(memory_space=pl.ANY),
        ],
        out_specs=pl.BlockSpec((tm, d), lambda i, *_: (i, 0)),
        scratch_shapes=[
            pltpu.VMEM((ROW_BUFS, 2 * tm, d), F32),
            pltpu.SemaphoreType.DMA((ROW_BUFS,)),
        ],
    )
    out = pl.pallas_call(
        _final_kernel,
        out_shape=jax.ShapeDtypeStruct((b * s, d), F32),
        grid_spec=grid_spec,
        compiler_params=pltpu.CompilerParams(
            dimension_semantics=("arbitrary",), vmem_limit_bytes=VMEM_LIMIT),
        name="final",
    )(slot, x1.reshape(b * s, d), rt, mod, modf, g_final.reshape(1, d), y_rows)
    return out.reshape(b, s, d)


def kernel(x, c, w_ada, b_ada, g_mix, w_in, sinks_a, rel_bias_b, g_out_a, g_out_b, w_out, g_ffn,
           w_router_group, b_router_group, w_router_expert, b_router_expert, w_gate, w_up, w_down,
           w_ada_final, b_ada_final, g_final):
    b, s, d = x.shape
    assert w_ada.shape[0] == 1, "one layer"
    n_tok = b * s

    c_act = jax.nn.silu(c)
    a_rep = jnp.broadcast_to(c_act[:, :, None], (b, d, LANES))
    mod = _ada(a_rep, w_ada[0], b_ada[0]).reshape(b, 6, d)
    modf = _ada(a_rep, w_ada_final, b_ada_final).reshape(b, 2, d)

    proj = _proj(x, mod, g_mix[0], w_in[0].astype(BF16))

    bias_a, nw_a, bias_b, nw_b = _band_tables(rel_bias_b[0])
    kv_a0 = DA_Q // LANES
    o_a = _attention(proj, bias_a, sinks_a[0].astype(F32), n_groups=N_KV_A, ncol=2, nw=nw_a,
                     q_col0=0, k_col0=kv_a0, v_col0=kv_a0 + DA_KV // LANES, kv_share=2, dup=True)
    qb0 = (DA_Q + 2 * DA_KV) // LANES
    o_b = _attention(proj, bias_b, None, n_groups=N_HEADS_B // 2, ncol=1, nw=nw_b,
                     q_col0=qb0, k_col0=qb0 + DB // LANES, v_col0=qb0 + 2 * DB // LANES,
                     kv_share=1, dup=False)

    n_r = N_GROUPS + N_EXPERTS
    assert n_r <= ROUTER_LO_LANE
    w_r = jnp.concatenate([w_router_group[0], w_router_expert[0]], axis=1)
    w_r_hi = w_r.astype(BF16)
    w_r_lo = (w_r - w_r_hi.astype(F32)).astype(BF16)
    w_router = (jnp.zeros((d, LANES), BF16).at[:, :n_r].set(w_r_hi)
                .at[:, ROUTER_LO_LANE:ROUTER_LO_LANE + n_r].set(w_r_lo))
    b_router = jnp.zeros((LANES,), F32).at[:n_r].set(
        jnp.concatenate([b_router_group[0], b_router_expert[0]]))
    x1, h2, rt, ids = _mix(o_a, o_b, x, mod, g_out_a[0], g_out_b[0], g_ffn[0], w_out[0].astype(BF16),
                      w_router, b_router)

    tile_expert, next_expert, n_tiles, zero_fill, slot, n_pad = _route_plan(ids, n_tok)
    xs = _dispatch(h2.reshape(n_tok, d), slot, zero_fill, n_pad)
    y = _moe(xs, w_gate[0], w_up[0], w_down[0], tile_expert, next_expert, n_tiles)

    return _final(x1, y, slot, rt, mod, modf, g_final)
```

```python
import functools

import jax
import jax.numpy as jnp
import numpy as np
from jax import lax
from jax.experimental import pallas as pl
from jax.experimental.pallas import tpu as pltpu

D_MODEL = 2048
CHUNK = 64
HEAD_DIM = 64
N_HEADS_A = 16
N_KV_A = 4
N_PREV_A = 2
N_HEADS_B = 16
N_PREV_B = 8
REL_CLIP = 128
DA_Q = N_HEADS_A * HEAD_DIM
DA_KV = N_KV_A * HEAD_DIM
DB = N_HEADS_B * HEAD_DIM
D_IN = DA_Q + 2 * DA_KV + 3 * DB
N_GROUPS = 4
EXPERTS_PER_GROUP = 8
N_EXPERTS = N_GROUPS * EXPERTS_PER_GROUP
D_EXPERT = D_MODEL // 4
EPS = 1e-6
NEG_INF = -1e30
LOG2E = 1.4426950408889634

LANES = 128
ROUTER_LO_LANE = 64
QBLK = 2 * CHUNK
VMEM_LIMIT = 56 * 1024 * 1024

F32 = jnp.float32
BF16 = jnp.bfloat16


def _rms(x, g):
    return x * lax.rsqrt(jnp.mean(x * x, axis=-1, keepdims=True) + EPS) * g


ADA_JB = 8


def _ada_kernel(a_ref, w_ref, b_ref, o_ref):
    n_b, k, _ = a_ref.shape
    tn = w_ref.shape[1]
    sub = 8
    for jb in range(tn // (ADA_JB * LANES)):
        col0 = jb * ADA_JB * LANES

        def body(kc, accs):
            k0 = pl.multiple_of(kc * sub, sub)
            a_rows = [a_ref[b, pl.ds(k0, sub), :] for b in range(n_b)]
            out = []
            for j in range(ADA_JB):
                w = w_ref[pl.ds(k0, sub), col0 + j * LANES:col0 + (j + 1) * LANES]
                out.append([accs[j][b] + a_rows[b] * w for b in range(n_b)])
            return out

        zero = jnp.zeros((sub, LANES), F32)
        accs = lax.fori_loop(0, k // sub, body, [[zero] * n_b for _ in range(ADA_JB)], unroll=4)
        for j in range(ADA_JB):
            cols = slice(col0 + j * LANES, col0 + (j + 1) * LANES)
            for b in range(n_b):
                o_ref[b:b + 1, cols] = jnp.sum(accs[j][b], axis=0, keepdims=True) + b_ref[:, cols]


def _ada(a_rep, w, bias):
    n_b, k, _ = a_rep.shape
    n = w.shape[1]
    tn = 2048
    return pl.pallas_call(
        _ada_kernel,
        out_shape=jax.ShapeDtypeStruct((n_b, n), F32),
        grid=(n // tn,),
        in_specs=[
            pl.BlockSpec((n_b, k, LANES), lambda j: (0, 0, 0)),
            pl.BlockSpec((k, tn), lambda j: (0, j)),
            pl.BlockSpec((1, tn), lambda j: (0, j)),
        ],
        out_specs=pl.BlockSpec((n_b, tn), lambda j: (0, j)),
        compiler_params=pltpu.CompilerParams(
            dimension_semantics=("arbitrary",), vmem_limit_bytes=VMEM_LIMIT),
        name="ada",
    )(a_rep, w, bias.reshape(1, n))


def _proj_kernel(x_ref, mod_ref, g_ref, w_ref, o_ref, *, n_chunk):
    h = _rms(x_ref[0], g_ref[...])
    h = h * (1.0 + mod_ref[0, 1:2, :]) + mod_ref[0, 0:1, :]
    hb = h.astype(BF16)
    for n0 in range(0, o_ref.shape[2], n_chunk):
        cols = slice(n0, n0 + n_chunk)
        o_ref[0, :, cols] = jnp.dot(hb, w_ref[:, cols], preferred_element_type=F32).astype(BF16)


def _proj(x, mod, g_mix, w_in_bf16):
    b, s, d = x.shape
    n = w_in_bf16.shape[1]
    tm = 512
    return pl.pallas_call(
        functools.partial(_proj_kernel, n_chunk=512),
        out_shape=jax.ShapeDtypeStruct((b, s, n), BF16),
        grid=(b, s // tm),
        in_specs=[
            pl.BlockSpec((1, tm, d), lambda bi, i: (bi, i, 0)),
            pl.BlockSpec((1, 6, d), lambda bi, i: (bi, 0, 0)),
            pl.BlockSpec((1, d), lambda bi, i: (0, 0)),
            pl.BlockSpec((d, n), lambda bi, i: (0, 0), pipeline_mode=pl.Buffered(1)),
        ],
        out_specs=pl.BlockSpec((1, tm, n), lambda bi, i: (bi, i, 0)),
        compiler_params=pltpu.CompilerParams(
            dimension_semantics=("arbitrary", "arbitrary"), vmem_limit_bytes=VMEM_LIMIT),
        name="proj",
    )(x, mod, g_mix.reshape(1, d), w_in_bf16)


def _attn_kernel(*refs, ncol, nw, dup, has_sink):
    if has_sink:
        sink_ref, q_ref, k_ref, v_ref, bias_ref, o_ref = refs[:6]
        scratch = refs[6:]
    else:
        q_ref, k_ref, v_ref, bias_ref, o_ref = refs[:5]
        scratch = refs[5:]
        sink_ref = None
    s_len = q_ref.shape[1]
    n_blk = s_len // QBLK
    n_stack = 2 * ncol
    grp = pl.program_id(1)

    rows = 512
    lane = lax.broadcasted_iota(jnp.int32, (QBLK, LANES), 1)
    low = lane < HEAD_DIM
    if dup:
        kd_ref, vt_ref = scratch[:2]
        half = grp % 2
        keep = (lax.broadcasted_iota(jnp.int32, (rows, LANES), 1) // HEAD_DIM) == half

        def spread(t):
            return jnp.where(keep, t, pltpu.roll(t, HEAD_DIM, 1))

        def dup_body(c, carry):
            r0 = pl.multiple_of(c * rows, rows)
            kd_ref[pl.ds(r0, rows), :] = spread(k_ref[0, pl.ds(r0, rows), :].astype(F32)).astype(BF16)
            return carry

        lax.fori_loop(0, s_len // rows, dup_body, 0)
        k_src = kd_ref
    else:
        vt_ref = scratch[0]
        k_src = k_ref.at[0]

        def spread(t):
            return t

    for c in range(s_len // rows):
        t = spread(v_ref[0, c * rows:(c + 1) * rows, :].astype(F32))
        vt_ref[:, c * rows:(c + 1) * rows] = t.T.astype(BF16)

    top = lax.broadcasted_iota(jnp.int32, (LANES, QBLK), 0) < HEAD_DIM

    if has_sink:
        sink = jnp.concatenate(
            [jnp.full((1, QBLK), sink_ref[grp * n_stack + h] * LOG2E, F32) for h in range(n_stack)],
            axis=1)

    def rows_of(j):
        return j * QBLK if isinstance(j, int) else pl.multiple_of(j * QBLK, QBLK)

    def logits(j, nvb):
        qf = q_ref[0, pl.ds(rows_of(j), QBLK), :].astype(F32) * (HEAD_DIM ** -0.5 * LOG2E)
        parts = []
        for c in range(ncol):
            qc = qf[:, c * LANES:(c + 1) * LANES]
            parts.append(jnp.where(low, qc, 0.0))
            parts.append(jnp.where(low, 0.0, qc))
        lhs = jnp.concatenate(parts, axis=0).astype(BF16)
        kw = k_src[pl.ds(rows_of(j - (nvb - 1)), nvb * QBLK), :]
        s = lax.dot_general(kw, lhs, (((1,), (1,)), ((), ())), preferred_element_type=F32)
        return s + bias_ref[0, (nw - nvb) * QBLK:, :]

    def softmax(s):
        m = jnp.max(s, axis=0, keepdims=True)
        if has_sink:
            m = jnp.maximum(m, sink)
        p = jnp.exp2(s - m)
        denom = jnp.sum(p, axis=0, keepdims=True)
        if has_sink:
            denom = denom + jnp.exp2(sink - m)
        return p.astype(BF16), denom

    def emit(j, nvb, p, denom):
        vw = vt_ref[:, pl.ds(rows_of(j - (nvb - 1)), nvb * QBLK)]
        o = jnp.dot(vw, p, preferred_element_type=F32) / denom
        for c in range(ncol):
            o0 = o[:, (2 * c) * QBLK:(2 * c + 1) * QBLK]
            o1 = o[:, (2 * c + 1) * QBLK:(2 * c + 2) * QBLK]
            o_ref[0, pl.ds(rows_of(j), QBLK), c * LANES:(c + 1) * LANES] = (
                jnp.where(top, o0, o1).T.astype(BF16))

    s_scr, p_scr = scratch[-2:]
    last = n_blk - 1

    def n_valid(j):
        return min(j + 1, nw) if isinstance(j, int) else nw

    def put_logits(u, j):
        s_scr[u, :n_valid(j) * QBLK, :] = logits(j, n_valid(j))

    def put_probs(u, j):
        rows_n = n_valid(j) * QBLK
        p, den = softmax(s_scr[u, :rows_n, :])
        p_scr[u, :rows_n, :] = p
        return den

    def pair(j, d_even, clamp):
        emit(j, n_valid(j), p_scr[0, :n_valid(j) * QBLK, :], d_even)
        d_odd = put_probs(1, j + 1)
        put_logits(0, clamp(j + 2))
        emit(j + 1, n_valid(j + 1), p_scr[1, :n_valid(j + 1) * QBLK, :], d_odd)
        d_next = put_probs(0, clamp(j + 2))
        put_logits(1, clamp(j + 3))
        return d_next

    first = nw - 1 + (nw - 1) % 2
    assert n_blk % 2 == 0 and first + 4 <= n_blk
    put_logits(0, 0)
    den = put_probs(0, 0)
    put_logits(1, 1)
    for j in range(0, first, 2):
        den = pair(j, den, lambda k: k)

    lax.fori_loop(0, (n_blk - first) // 2,
                  lambda t, d: pair(first + 2 * t, d, lambda k: jnp.minimum(k, last)), den)


def _attention(proj, bias, sinks, *, n_groups, ncol, nw, q_col0, k_col0, v_col0, kv_share, dup):
    b, s, _ = proj.shape
    qw = ncol * LANES
    has_sink = sinks is not None
    kernel = functools.partial(_attn_kernel, ncol=ncol, nw=nw, dup=dup, has_sink=has_sink)
    _, n_keys, n_q = bias.shape
    in_specs = [
        pl.BlockSpec((1, s, qw), lambda bi, g, *_: (bi, 0, q_col0 // ncol + g)),
        pl.BlockSpec((1, s, LANES), lambda bi, g, *_: (bi, 0, k_col0 + g // kv_share)),
        pl.BlockSpec((1, s, LANES), lambda bi, g, *_: (bi, 0, v_col0 + g // kv_share)),
        pl.BlockSpec((1, n_keys, n_q), lambda bi, g, *_: (g, 0, 0)),
    ]
    out_spec = pl.BlockSpec((1, s, qw), lambda bi, g, *_: (bi, 0, g))
    scratch = [pltpu.VMEM((s, LANES), BF16)] if dup else []
    scratch += [pltpu.VMEM((LANES, s), BF16),
                pltpu.VMEM((2, n_keys, n_q), F32), pltpu.VMEM((2, n_keys, n_q), BF16)]
    grid_spec = pltpu.PrefetchScalarGridSpec(
        num_scalar_prefetch=1 if has_sink else 0,
        grid=(b, n_groups),
        in_specs=in_specs,
        out_specs=out_spec,
        scratch_shapes=scratch,
    )
    args = ((sinks,) if has_sink else ()) + (proj, proj, proj, bias)
    return pl.pallas_call(
        kernel,
        out_shape=jax.ShapeDtypeStruct((b, s, n_groups * qw), BF16),
        grid_spec=grid_spec,
        compiler_params=pltpu.CompilerParams(
            dimension_semantics=("arbitrary", "arbitrary"), vmem_limit_bytes=VMEM_LIMIT),
        name="attn_a" if dup else "attn_b",
    )(*args)


def _band_tables(rel_bias_b):
    qi = jnp.arange(QBLK)[:, None]
    r = qi // CHUNK

    def table(nw, n_prev, fn):
        kj = jnp.arange(nw * QBLK)[None, :]
        rel = (nw - 1) * QBLK + qi - kj
        inband = (kj >= r * CHUNK) & (kj < (r + n_prev + 1) * CHUNK)
        return jnp.where(inband[None], fn(rel), NEG_INF)

    slopes = jnp.exp2(-8.0 * jnp.arange(1, N_HEADS_A + 1, dtype=F32) / N_HEADS_A)
    nw_a = (N_PREV_A * CHUNK) // QBLK + 1
    nw_b = (N_PREV_B * CHUNK) // QBLK + 1
    bias_a = table(nw_a, N_PREV_A, lambda rel: -slopes[:, None, None] * jnp.abs(rel).astype(F32)[None])
    w_b = nw_b * QBLK
    row = -(-(QBLK + w_b - 2) // LANES) * LANES
    p = row + 1
    t = np.arange(p)
    m = np.where(t < w_b, t, t - p)
    dist = np.clip((nw_b - 1) * QBLK - m, -REL_CLIP, REL_CLIP) + REL_CLIP
    vec = rel_bias_b[:, dist].astype(F32)
    reps = -(-(QBLK * row) // p)
    rel_b = jnp.tile(vec, (1, reps))[:, :QBLK * row].reshape(N_HEADS_B, QBLK, row)[:, :, :w_b]
    bias_b = table(nw_b, N_PREV_B, lambda rel: rel_b)
    rep = N_HEADS_A // N_KV_A
    bias_a = bias_a.reshape(N_KV_A, rep * QBLK, nw_a * QBLK).transpose(0, 2, 1)
    bias_b = bias_b.reshape(N_HEADS_B // 2, 2 * QBLK, nw_b * QBLK).transpose(0, 2, 1)
    return bias_a * LOG2E, nw_a, bias_b * LOG2E, nw_b


def _mix_kernel(oa_ref, ob_ref, x_ref, mod_ref, ga_ref, gb_ref, gf_ref, wo_ref, wr_ref, br_ref,
                x1_ref, h2_ref, rt_ref, ids_ref):
    half = oa_ref.shape[2]
    na = _rms(oa_ref[0].astype(F32), ga_ref[...]).astype(BF16)
    nb = _rms(ob_ref[0].astype(F32), gb_ref[...]).astype(BF16)
    acc = jnp.dot(na, wo_ref[:half, :], preferred_element_type=F32)
    acc = acc + jnp.dot(nb, wo_ref[half:, :], preferred_element_type=F32)
    x1 = x_ref[0] + mod_ref[0, 2:3, :] * acc
    x1_ref[0] = x1
    h2 = _rms(x1, gf_ref[...]) * (1.0 + mod_ref[0, 4:5, :]) + mod_ref[0, 3:4, :]
    h2_ref[0] = h2

    tm = h2.shape[0]
    h_hi = h2.astype(BF16)
    h_lo = (h2 - h_hi.astype(F32)).astype(BF16)
    r = jnp.dot(jnp.concatenate([h_hi, h_lo], axis=0), wr_ref[...], preferred_element_type=F32)
    r = r[:tm] + r[tm:]
    logits = r + pltpu.roll(r, LANES - ROUTER_LO_LANE, 1) + br_ref[...]

    lane = lax.broadcasted_iota(jnp.int32, (tm, LANES), 1)
    lane_f = lane.astype(F32)
    big = float(LANES)
    ninf = -jnp.inf

    def first_max(vals):
        top = jnp.max(vals, axis=-1, keepdims=True)
        idx = jnp.min(jnp.where(vals == top, lane_f, big), axis=-1, keepdims=True)
        return top, idx

    is_g = lane < N_GROUPS
    g_top, g_idx = first_max(jnp.where(is_g, logits, ninf))
    p_g = 1.0 / jnp.sum(jnp.where(is_g, jnp.exp(logits - g_top), 0.0), axis=-1, keepdims=True)
    lo = N_GROUPS + g_idx * EXPERTS_PER_GROUP
    e_vals = jnp.where((lane_f >= lo) & (lane_f < lo + EXPERTS_PER_GROUP), logits, ninf)
    v1, i1 = first_max(e_vals)
    v2, i2 = first_max(jnp.where(lane_f == i1, ninf, e_vals))
    e2 = jnp.exp(v2 - v1)
    w1 = p_g / (1.0 + e2)
    w2 = p_g * e2 / (1.0 + e2)
    rt = jnp.where(lane == 0, i1 - N_GROUPS,
                   jnp.where(lane == 1, i2 - N_GROUPS,
                             jnp.where(lane == 2, w1, jnp.where(lane == 3, w2, 0.0))))
    rt_ref[...] = rt
    pick = (lax.broadcasted_iota(jnp.int32, (8, LANES), 0)
            == lax.broadcasted_iota(jnp.int32, (8, LANES), 1)).astype(BF16)
    ids_ref[...] = lax.dot_general(pick, rt.astype(BF16), (((1,), (1,)), ((), ())),
                                   preferred_element_type=F32)


def _mix(o_a, o_b, x, mod, g_out_a, g_out_b, g_ffn, w_out_bf16, w_router, b_router):
    b, s, d = x.shape
    half = o_a.shape[2]
    tm = 512
    nt = s // tm
    vec = lambda n: pl.BlockSpec((1, n), lambda bi, i: (0, 0))
    return pl.pallas_call(
        _mix_kernel,
        out_shape=(jax.ShapeDtypeStruct((b, s, d), F32),
                   jax.ShapeDtypeStruct((b, s, d), F32),
                   jax.ShapeDtypeStruct((b * s, LANES), F32),
                   jax.ShapeDtypeStruct((8, b * s), F32)),
        grid=(b, nt),
        in_specs=[
            pl.BlockSpec((1, tm, half), lambda bi, i: (bi, i, 0)),
            pl.BlockSpec((1, tm, half), lambda bi, i: (bi, i, 0)),
            pl.BlockSpec((1, tm, d), lambda bi, i: (bi, i, 0)),
            pl.BlockSpec((1, 6, d), lambda bi, i: (bi, 0, 0)),
            vec(half), vec(half), vec(d),
            pl.BlockSpec((d, d), lambda bi, i: (0, 0), pipeline_mode=pl.Buffered(1)),
            pl.BlockSpec((d, LANES), lambda bi, i: (0, 0)),
            vec(LANES),
        ],
        out_specs=(pl.BlockSpec((1, tm, d), lambda bi, i: (bi, i, 0)),
                   pl.BlockSpec((1, tm, d), lambda bi, i: (bi, i, 0)),
                   pl.BlockSpec((tm, LANES), lambda bi, i: (bi * nt + i, 0)),
                   pl.BlockSpec((8, tm), lambda bi, i: (0, bi * nt + i))),
        compiler_params=pltpu.CompilerParams(
            dimension_semantics=("arbitrary", "arbitrary"), vmem_limit_bytes=VMEM_LIMIT),
        name="mix_out",
    )(o_a, o_b, x, mod, g_out_a.reshape(1, half), g_out_b.reshape(1, half), g_ffn.reshape(1, d),
      w_out_bf16, w_router, b_router.reshape(1, LANES))


MOE_TM = 256


ROW_BUFS = 3
DISPATCH_TM = 256
DISPATCH_STEPS_MIN = ROW_BUFS


def _dispatch_kernel(slot_ref, zf_ref, h_ref, xs_hbm, buf, zbuf, sem, zsem):
    i = pl.program_id(0)
    n_steps = pl.num_programs(0)
    td = h_ref.shape[0]
    tm = zbuf.shape[0]
    n_tiles_max = xs_hbm.shape[0] // tm

    @pl.when(i == 0)
    def _():
        zbuf[...] = jnp.zeros(zbuf.shape, zbuf.dtype)

        def fill_copy(t):
            z0 = pl.multiple_of(zf_ref[t], 8)
            n = pl.multiple_of(tm - z0, 8)
            row0 = pl.multiple_of(t * tm + z0, 8)
            return pltpu.make_async_copy(zbuf.at[pl.ds(0, n)], xs_hbm.at[pl.ds(row0, n)], zsem.at[0])

        def fill(t, carry):
            @pl.when(zf_ref[t] < tm)
            def _():
                fill_copy(t).start()
            return carry
        lax.fori_loop(0, n_tiles_max, fill, 0)

        def drain(t, carry):
            @pl.when(zf_ref[t] < tm)
            def _():
                fill_copy(t).wait()
            return carry
        lax.fori_loop(0, n_tiles_max, drain, 0)

    def wait_rows(s):
        for _ in range(2):
            pltpu.make_async_copy(buf.at[s], xs_hbm.at[pl.ds(0, td)], sem.at[s]).wait()

    def step(s):
        @pl.when(i >= ROW_BUFS)
        def _():
            wait_rows(s)

        buf[s] = h_ref[...]
        for r in range(td):
            for k in range(2):
                dst = slot_ref[k * (n_steps * td) + i * td + r]
                pltpu.make_async_copy(buf.at[s, pl.ds(r, 1)], xs_hbm.at[pl.ds(dst, 1)],
                                      sem.at[s]).start(priority=k)

        @pl.when(i == n_steps - 1)
        def _():
            for t in range(min(ROW_BUFS, DISPATCH_STEPS_MIN)):
                wait_rows((s + ROW_BUFS - t) % ROW_BUFS)

    for s in range(ROW_BUFS):
        pl.when(i % ROW_BUFS == s)(functools.partial(step, s))


def _dispatch(h2, slot, zero_fill, n_pad):
    t, d = h2.shape
    tm = MOE_TM
    assert t // DISPATCH_TM >= DISPATCH_STEPS_MIN
    grid_spec = pltpu.PrefetchScalarGridSpec(
        num_scalar_prefetch=2,
        grid=(t // DISPATCH_TM,),
        in_specs=[pl.BlockSpec((DISPATCH_TM, d), lambda i, *_: (i, 0))],
        out_specs=pl.BlockSpec(memory_space=pl.ANY),
        scratch_shapes=[
            pltpu.VMEM((ROW_BUFS, DISPATCH_TM, d), F32),
            pltpu.VMEM((tm, d), F32),
            pltpu.SemaphoreType.DMA((ROW_BUFS,)),
            pltpu.SemaphoreType.DMA((1,)),
        ],
    )
    return pl.pallas_call(
        _dispatch_kernel,
        out_shape=jax.ShapeDtypeStruct((n_pad, d), F32),
        grid_spec=grid_spec,
        compiler_params=pltpu.CompilerParams(
            dimension_semantics=("arbitrary",), vmem_limit_bytes=VMEM_LIMIT),
        name="dispatch",
    )(slot, zero_fill, h2)


def _moe_kernel(te_ref, nxe_ref, nt_ref, x_ref, wg_hbm, wu_hbm, wd_hbm, o_ref,
                wg32, wu32, wd32, wgb, wub, wdb, wsem):
    i = pl.program_id(0)
    n_tiles = nt_ref[0]

    def weight_copies(e):
        return [pltpu.make_async_copy(src.at[e], dst, wsem.at[0])
                for src, dst in ((wg_hbm, wg32), (wu_hbm, wu32), (wd_hbm, wd32))]

    @pl.when(i == 0)
    def _():
        for cp in weight_copies(te_ref[0]):
            cp.start()

    @pl.when(i < n_tiles)
    def _():
        @pl.when(jnp.logical_or(i == 0, te_ref[i] != te_ref[jnp.maximum(i - 1, 0)]))
        def _():
            for cp in weight_copies(0):
                cp.wait()
            wgb[...] = wg32[...].astype(BF16)
            wub[...] = wu32[...].astype(BF16)
            wdb[...] = wd32[...].astype(BF16)

            @pl.when(nxe_ref[i] >= 0)
            def _():
                for cp in weight_copies(nxe_ref[i]):
                    cp.start()

        xb = x_ref[...].astype(BF16)
        g = jnp.dot(xb, wgb[...], preferred_element_type=F32)
        u = jnp.dot(xb, wub[...], preferred_element_type=F32)
        a = (g * jax.nn.sigmoid(g) * u).astype(BF16)
        o_ref[...] = jnp.dot(a, wdb[...], preferred_element_type=F32)

    @pl.when(i >= n_tiles)
    def _():
        o_ref[...] = jnp.zeros(o_ref.shape, o_ref.dtype)


def _moe(xs, w_gate, w_up, w_down, tile_expert, next_expert, n_tiles):
    n_pad, d = xs.shape
    tm = MOE_TM
    max_tiles = n_pad // tm
    de = w_gate.shape[2]
    grid_spec = pltpu.PrefetchScalarGridSpec(
        num_scalar_prefetch=3,
        grid=(max_tiles,),
        in_specs=[pl.BlockSpec((tm, d), lambda i, te, nxe, nt: (jnp.minimum(i, nt[0] - 1), 0))]
        + [pl.BlockSpec(memory_space=pl.ANY)] * 3,
        out_specs=pl.BlockSpec((tm, d), lambda i, *_: (i, 0)),
        scratch_shapes=[
            pltpu.VMEM((d, de), F32),
            pltpu.VMEM((d, de), F32),
            pltpu.VMEM((de, d), F32),
            pltpu.VMEM((d, de), BF16),
            pltpu.VMEM((d, de), BF16),
            pltpu.VMEM((de, d), BF16),
            pltpu.SemaphoreType.DMA((1,)),
        ],
    )
    return pl.pallas_call(
        _moe_kernel,
        out_shape=jax.ShapeDtypeStruct((n_pad, d), F32),
        grid_spec=grid_spec,
        compiler_params=pltpu.CompilerParams(
            dimension_semantics=("arbitrary",), vmem_limit_bytes=VMEM_LIMIT),
        name="moe",
    )(tile_expert, next_expert, n_tiles, xs, w_gate, w_up, w_down)


def _route_plan(ids, n_tok):
    tm = MOE_TM
    n_asg = 2 * n_tok
    n_pad = n_asg + N_EXPERTS * tm
    max_tiles = n_pad // tm
    e_flat = ids[:2].astype(jnp.int32).reshape(-1)
    onehot = e_flat[None, :] == jnp.arange(N_EXPERTS)[:, None]
    blk = LANES
    oh3 = onehot.reshape(N_EXPERTS, n_asg // blk, blk).astype(BF16)
    upper = (jnp.arange(blk)[:, None] <= jnp.arange(blk)[None, :]).astype(BF16)
    within = jnp.einsum("ebj,jk->ebk", oh3, upper, preferred_element_type=F32).astype(jnp.int32)
    blk_tot = within[:, :, -1]
    offs = jnp.cumsum(blk_tot, axis=1) - blk_tot
    csum = (within + offs[:, :, None]).reshape(N_EXPERTS, n_asg)
    counts = jnp.sum(blk_tot, axis=1)
    oh_i = onehot.astype(jnp.int32)
    rank = jnp.sum(csum * oh_i, axis=0) - 1
    tiles_e = (counts + tm - 1) // tm
    tile_end = jnp.cumsum(tiles_e)
    base = (tile_end - tiles_e) * tm
    slot = jnp.sum(oh_i * base[:, None], axis=0) + rank
    n_tiles = tile_end[-1]
    tile_id = jnp.arange(max_tiles)
    te = jnp.sum((tile_id[:, None] >= tile_end[None, :]).astype(jnp.int32), axis=1)
    te_last = jnp.sum(((n_tiles - 1) >= tile_end).astype(jnp.int32))
    tile_expert = jnp.where(tile_id < n_tiles, te, te_last).astype(jnp.int32)
    ex = jnp.arange(N_EXPERTS)
    later = (ex[None, :] > ex[:, None]) & (counts[None, :] > 0)
    nxt_e = jnp.min(jnp.where(later, ex[None, :], N_EXPERTS), axis=1)
    nxt_e = jnp.where(nxt_e < N_EXPERTS, nxt_e, -1)
    sel = (tile_expert[:, None] == ex[None, :]).astype(jnp.int32)
    next_expert = jnp.sum(sel * nxt_e[None, :], axis=1).astype(jnp.int32)
    last_tile = jnp.sum(sel * (tile_end - 1)[None, :], axis=1)
    used_last = jnp.sum(sel * (counts - (tiles_e - 1) * tm)[None, :], axis=1)
    zero_fill = jnp.where(tile_id >= n_tiles, 0,
                          jnp.where(tile_id == last_tile, (used_last // 8) * 8, tm)).astype(jnp.int32)
    return (tile_expert, next_expert, n_tiles.reshape(1).astype(jnp.int32), zero_fill,
            slot.astype(jnp.int32), n_pad)


FINAL_TM = 256


def _final_kernel(slot_ref, x1_ref, rt_ref, mod_ref, modf_ref, g_ref, y_hbm, o_ref, ybuf, gsem):
    i = pl.program_id(0)
    n_steps = pl.num_programs(0)
    tm = x1_ref.shape[0]

    def row_in(tile, r, k, s):
        idx = slot_ref[k * (n_steps * tm) + tile * tm + r]
        pltpu.make_async_copy(y_hbm.at[pl.ds(idx, 1)], ybuf.at[s, pl.ds(k * tm + r, 1)],
                              gsem.at[s]).start(priority=k)

    def wait_in(s):
        pltpu.make_async_copy(y_hbm.at[pl.ds(0, 2 * tm)], ybuf.at[s], gsem.at[s]).wait()

    @pl.when(i == 0)
    def _():
        for t in range(ROW_BUFS - 1):
            tile = jnp.minimum(t, n_steps - 1)

            def row(r, carry):
                row_in(tile, r, 0, t)
                row_in(tile, r, 1, t)
                return carry
            lax.fori_loop(0, tm, row, 0, unroll=8)

    def step(s):
        wait_in(s)
        rt = rt_ref[...]
        y = rt[:, 2:3] * ybuf[s, :tm, :] + rt[:, 3:4] * ybuf[s, tm:, :]
        nxt = jnp.minimum(i + ROW_BUFS - 1, n_steps - 1)
        for r in range(tm):
            row_in(nxt, r, 0, (s + ROW_BUFS - 1) % ROW_BUFS)
            row_in(nxt, r, 1, (s + ROW_BUFS - 1) % ROW_BUFS)
        x2 = x1_ref[...] + mod_ref[0, 5:6, :] * y
        o_ref[...] = _rms(x2, g_ref[...]) * (1.0 + modf_ref[0, 1:2, :]) + modf_ref[0, 0:1, :]

        @pl.when(i == n_steps - 1)
        def _():
            for t in range(1, ROW_BUFS):
                wait_in((s + t) % ROW_BUFS)

    for s in range(ROW_BUFS):
        pl.when(i % ROW_BUFS == s)(functools.partial(step, s))


def _final(x1, y_rows, slot, rt, mod, modf, g_final):
    b, s, d = x1.shape
    tm = FINAL_TM
    nt = s // tm
    grid_spec = pltpu.PrefetchScalarGridSpec(
        num_scalar_prefetch=1,
        grid=(b * nt,),
        in_specs=[
            pl.BlockSpec((tm, d), lambda i, *_: (i, 0)),
            pl.BlockSpec((tm, LANES), lambda i, *_: (i, 0)),
            pl.BlockSpec((1, 6, d), lambda i, *_: (i // nt, 0, 0)),
            pl.BlockSpec((1, 2, d), lambda i, *_: (i // nt, 0, 0)),
            pl.BlockSpec((1, d), lambda i, *_: (0, 0)),
            pl.BlockSpec(memory_space=pl.ANY),
        ],
        out_specs=pl.BlockSpec((tm, d), lambda i, *_: (i, 0)),
        scratch_shapes=[
            pltpu.VMEM((ROW_BUFS, 2 * tm, d), F32),
            pltpu.SemaphoreType.DMA((ROW_BUFS,)),
        ],
    )
    out = pl.pallas_call(
        _final_kernel,
        out_shape=jax.ShapeDtypeStruct((b * s, d), F32),
        grid_spec=grid_spec,
        compiler_params=pltpu.CompilerParams(
            dimension_semantics=("arbitrary",), vmem_limit_bytes=VMEM_LIMIT),
        name="final",
    )(slot, x1.reshape(b * s, d), rt, mod, modf, g_final.reshape(1, d), y_rows)
    return out.reshape(b, s, d)


def kernel(x, c, w_ada, b_ada, g_mix, w_in, sinks_a, rel_bias_b, g_out_a, g_out_b, w_out, g_ffn,
           w_router_group, b_router_group, w_router_expert, b_router_expert, w_gate, w_up, w_down,
           w_ada_final, b_ada_final, g_final):
    b, s, d = x.shape
    assert w_ada.shape[0] == 1, "one layer"
    n_tok = b * s

    c_act = jax.nn.silu(c)
    a_rep = jnp.broadcast_to(c_act[:, :, None], (b, d, LANES))
    mod = _ada(a_rep, w_ada[0], b_ada[0]).reshape(b, 6, d)
    modf = _ada(a_rep, w_ada_final, b_ada_final).reshape(b, 2, d)

    proj = _proj(x, mod, g_mix[0], w_in[0].astype(BF16))

    bias_a, nw_a, bias_b, nw_b = _band_tables(rel_bias_b[0])
    kv_a0 = DA_Q // LANES
    o_a = _attention(proj, bias_a, sinks_a[0].astype(F32), n_groups=N_KV_A, ncol=2, nw=nw_a,
                     q_col0=0, k_col0=kv_a0, v_col0=kv_a0 + DA_KV // LANES, kv_share=2, dup=True)
    qb0 = (DA_Q + 2 * DA_KV) // LANES
    o_b = _attention(proj, bias_b, None, n_groups=N_HEADS_B // 2, ncol=1, nw=nw_b,
                     q_col0=qb0, k_col0=qb0 + DB // LANES, v_col0=qb0 + 2 * DB // LANES,
                     kv_share=1, dup=False)

    n_r = N_GROUPS + N_EXPERTS
    assert n_r <= ROUTER_LO_LANE
    w_r = jnp.concatenate([w_router_group[0], w_router_expert[0]], axis=1)
    w_r_hi = w_r.astype(BF16)
    w_r_lo = (w_r - w_r_hi.astype(F32)).astype(BF16)
    w_router = (jnp.zeros((d, LANES), BF16).at[:, :n_r].set(w_r_hi)
                .at[:, ROUTER_LO_LANE:ROUTER_LO_LANE + n_r].set(w_r_lo))
    b_router = jnp.zeros((LANES,), F32).at[:n_r].set(
        jnp.concatenate([b_router_group[0], b_router_expert[0]]))
    x1, h2, rt, ids = _mix(o_a, o_b, x, mod, g_out_a[0], g_out_b[0], g_ffn[0], w_out[0].astype(BF16),
                      w_router, b_router)

    tile_expert, next_expert, n_tiles, zero_fill, slot, n_pad = _route_plan(ids, n_tok)
    xs = _dispatch(h2.reshape(n_tok, d), slot, zero_fill, n_pad)
    y = _moe(xs, w_gate[0], w_up[0], w_down[0], tile_expert, next_expert, n_tiles)

    return _final(x1, y, slot, rt, mod, modf, g_final)
```

```python
import functools

import jax
import jax.numpy as jnp
import numpy as np
from jax import lax
from jax.experimental import pallas as pl
from jax.experimental.pallas import tpu as pltpu

D_MODEL = 2048
CHUNK = 64
HEAD_DIM = 64
N_HEADS_A = 16
N_KV_A = 4
N_PREV_A = 2
N_HEADS_B = 16
N_PREV_B = 8
REL_CLIP = 128
DA_Q = N_HEADS_A * HEAD_DIM
DA_KV = N_KV_A * HEAD_DIM
DB = N_HEADS_B * HEAD_DIM
D_IN = DA_Q + 2 * DA_KV + 3 * DB
N_GROUPS = 4
EXPERTS_PER_GROUP = 8
N_EXPERTS = N_GROUPS * EXPERTS_PER_GROUP
D_EXPERT = D_MODEL // 4
EPS = 1e-6
NEG_INF = -1e30
LOG2E = 1.4426950408889634

LANES = 128
ROUTER_LO_LANE = 64
QBLK = 2 * CHUNK
VMEM_LIMIT = 56 * 1024 * 1024

F32 = jnp.float32
BF16 = jnp.bfloat16


def _rms(x, g):
    return x * lax.rsqrt(jnp.mean(x * x, axis=-1, keepdims=True) + EPS) * g


ADA_JB = 8


def _ada_kernel(a_ref, w_ref, b_ref, o_ref):
    n_b, k, _ = a_ref.shape
    tn = w_ref.shape[1]
    sub = 8
    for jb in range(tn // (ADA_JB * LANES)):
        col0 = jb * ADA_JB * LANES

        def body(kc, accs):
            k0 = pl.multiple_of(kc * sub, sub)
            a_rows = [a_ref[b, pl.ds(k0, sub), :] for b in range(n_b)]
            out = []
            for j in range(ADA_JB):
                w = w_ref[pl.ds(k0, sub), col0 + j * LANES:col0 + (j + 1) * LANES]
                out.append([accs[j][b] + a_rows[b] * w for b in range(n_b)])
            return out

        zero = jnp.zeros((sub, LANES), F32)
        accs = lax.fori_loop(0, k // sub, body, [[zero] * n_b for _ in range(ADA_JB)], unroll=4)
        for j in range(ADA_JB):
            cols = slice(col0 + j * LANES, col0 + (j + 1) * LANES)
            for b in range(n_b):
                o_ref[b:b + 1, cols] = jnp.sum(accs[j][b], axis=0, keepdims=True) + b_ref[:, cols]


def _ada(a_rep, w, bias):
    n_b, k, _ = a_rep.shape
    n = w.shape[1]
    tn = 2048
    return pl.pallas_call(
        _ada_kernel,
        out_shape=jax.ShapeDtypeStruct((n_b, n), F32),
        grid=(n // tn,),
        in_specs=[
            pl.BlockSpec((n_b, k, LANES), lambda j: (0, 0, 0)),
            pl.BlockSpec((k, tn), lambda j: (0, j)),
            pl.BlockSpec((1, tn), lambda j: (0, j)),
        ],
        out_specs=pl.BlockSpec((n_b, tn), lambda j: (0, j)),
        compiler_params=pltpu.CompilerParams(
            dimension_semantics=("arbitrary",), vmem_limit_bytes=VMEM_LIMIT),
        name="ada",
    )(a_rep, w, bias.reshape(1, n))


def _proj_kernel(x_ref, mod_ref, g_ref, w_ref, o_ref, *, n_chunk):
    h = _rms(x_ref[0], g_ref[...])
    h = h * (1.0 + mod_ref[0, 1:2, :]) + mod_ref[0, 0:1, :]
    hb = h.astype(BF16)
    for n0 in range(0, o_ref.shape[2], n_chunk):
        cols = slice(n0, n0 + n_chunk)
        o_ref[0, :, cols] = jnp.dot(hb, w_ref[:, cols], preferred_element_type=F32).astype(BF16)


def _proj(x, mod, g_mix, w_in_bf16):
    b, s, d = x.shape
    n = w_in_bf16.shape[1]
    tm = 512
    return pl.pallas_call(
        functools.partial(_proj_kernel, n_chunk=512),
        out_shape=jax.ShapeDtypeStruct((b, s, n), BF16),
        grid=(b, s // tm),
        in_specs=[
            pl.BlockSpec((1, tm, d), lambda bi, i: (bi, i, 0)),
            pl.BlockSpec((1, 6, d), lambda bi, i: (bi, 0, 0)),
            pl.BlockSpec((1, d), lambda bi, i: (0, 0)),
            pl.BlockSpec((d, n), lambda bi, i: (0, 0), pipeline_mode=pl.Buffered(1)),
        ],
        out_specs=pl.BlockSpec((1, tm, n), lambda bi, i: (bi, i, 0)),
        compiler_params=pltpu.CompilerParams(
            dimension_semantics=("arbitrary", "arbitrary"), vmem_limit_bytes=VMEM_LIMIT),
        name="proj",
    )(x, mod, g_mix.reshape(1, d), w_in_bf16)


def _attn_kernel(*refs, ncol, nw, dup, has_sink):
    if has_sink:
        sink_ref, q_ref, k_ref, v_ref, bias_ref, o_ref = refs[:6]
        scratch = refs[6:]
    else:
        q_ref, k_ref, v_ref, bias_ref, o_ref = refs[:5]
        scratch = refs[5:]
        sink_ref = None
    s_len = q_ref.shape[1]
    n_blk = s_len // QBLK
    n_stack = 2 * ncol
    grp = pl.program_id(1)

    rows = 512
    lane = lax.broadcasted_iota(jnp.int32, (QBLK, LANES), 1)
    low = lane < HEAD_DIM
    if dup:
        kd_ref, vt_ref = scratch[:2]
        half = grp % 2
        keep = (lax.broadcasted_iota(jnp.int32, (rows, LANES), 1) // HEAD_DIM) == half

        def spread(t):
            return jnp.where(keep, t, pltpu.roll(t, HEAD_DIM, 1))

        def dup_body(c, carry):
            r0 = pl.multiple_of(c * rows, rows)
            kd_ref[pl.ds(r0, rows), :] = spread(k_ref[0, pl.ds(r0, rows), :].astype(F32)).astype(BF16)
            return carry

        lax.fori_loop(0, s_len // rows, dup_body, 0)
        k_src = kd_ref
    else:
        vt_ref = scratch[0]
        k_src = k_ref.at[0]

        def spread(t):
            return t

    for c in range(s_len // rows):
        t = spread(v_ref[0, c * rows:(c + 1) * rows, :].astype(F32))
        vt_ref[:, c * rows:(c + 1) * rows] = t.T.astype(BF16)

    top = lax.broadcasted_iota(jnp.int32, (LANES, QBLK), 0) < HEAD_DIM

    if has_sink:
        sink = jnp.concatenate(
            [jnp.full((1, QBLK), sink_ref[grp * n_stack + h] * LOG2E, F32) for h in range(n_stack)],
            axis=1)

    def rows_of(j):
        return j * QBLK if isinstance(j, int) else pl.multiple_of(j * QBLK, QBLK)

    def logits(j, nvb):
        qf = q_ref[0, pl.ds(rows_of(j), QBLK), :].astype(F32) * (HEAD_DIM ** -0.5 * LOG2E)
        parts = []
        for c in range(ncol):
            qc = qf[:, c * LANES:(c + 1) * LANES]
            parts.append(jnp.where(low, qc, 0.0))
            parts.append(jnp.where(low, 0.0, qc))
        lhs = jnp.concatenate(parts, axis=0).astype(BF16)
        kw = k_src[pl.ds(rows_of(j - (nvb - 1)), nvb * QBLK), :]
        s = lax.dot_general(kw, lhs, (((1,), (1,)), ((), ())), preferred_element_type=F32)
        return s + bias_ref[0, (nw - nvb) * QBLK:, :]

    def softmax(s):
        m = jnp.max(s, axis=0, keepdims=True)
        if has_sink:
            m = jnp.maximum(m, sink)
        p = jnp.exp2(s - m)
        denom = jnp.sum(p, axis=0, keepdims=True)
        if has_sink:
            denom = denom + jnp.exp2(sink - m)
        return p.astype(BF16), denom

    def emit(j, nvb, p, denom):
        vw = vt_ref[:, pl.ds(rows_of(j - (nvb - 1)), nvb * QBLK)]
        o = jnp.dot(vw, p, preferred_element_type=F32) / denom
        for c in range(ncol):
            o0 = o[:, (2 * c) * QBLK:(2 * c + 1) * QBLK]
            o1 = o[:, (2 * c + 1) * QBLK:(2 * c + 2) * QBLK]
            o_ref[0, pl.ds(rows_of(j), QBLK), c * LANES:(c + 1) * LANES] = (
                jnp.where(top, o0, o1).T.astype(BF16))

    s_scr, p_scr = scratch[-2:]
    last = n_blk - 1

    def n_valid(j):
        return min(j + 1, nw) if isinstance(j, int) else nw

    def put_logits(u, j):
        s_scr[u, :n_valid(j) * QBLK, :] = logits(j, n_valid(j))

    def put_probs(u, j):
        rows_n = n_valid(j) * QBLK
        p, den = softmax(s_scr[u, :rows_n, :])
        p_scr[u, :rows_n, :] = p
        return den

    def pair(j, d_even, clamp):
        emit(j, n_valid(j), p_scr[0, :n_valid(j) * QBLK, :], d_even)
        d_odd = put_probs(1, j + 1)
        put_logits(0, clamp(j + 2))
        emit(j + 1, n_valid(j + 1), p_scr[1, :n_valid(j + 1) * QBLK, :], d_odd)
        d_next = put_probs(0, clamp(j + 2))
        put_logits(1, clamp(j + 3))
        return d_next

    first = nw - 1 + (nw - 1) % 2
    assert n_blk % 2 == 0 and first + 4 <= n_blk
    put_logits(0, 0)
    den = put_probs(0, 0)
    put_logits(1, 1)
    for j in range(0, first, 2):
        den = pair(j, den, lambda k: k)

    lax.fori_loop(0, (n_blk - first) // 2,
                  lambda t, d: pair(first + 2 * t, d, lambda k: jnp.minimum(k, last)), den)


def _attention(proj, bias, sinks, *, n_groups, ncol, nw, q_col0, k_col0, v_col0, kv_share, dup):
    b, s, _ = proj.shape
    qw = ncol * LANES
    has_sink = sinks is not None
    kernel = functools.partial(_attn_kernel, ncol=ncol, nw=nw, dup=dup, has_sink=has_sink)
    _, n_keys, n_q = bias.shape
    in_specs = [
        pl.BlockSpec((1, s, qw), lambda bi, g, *_: (bi, 0, q_col0 // ncol + g)),
        pl.BlockSpec((1, s, LANES), lambda bi, g, *_: (bi, 0, k_col0 + g // kv_share)),
        pl.BlockSpec((1, s, LANES), lambda bi, g, *_: (bi, 0, v_col0 + g // kv_share)),
        pl.BlockSpec((1, n_keys, n_q), lambda bi, g, *_: (g, 0, 0)),
    ]
    out_spec = pl.BlockSpec((1, s, qw), lambda bi, g, *_: (bi, 0, g))
    scratch = [pltpu.VMEM((s, LANES), BF16)] if dup else []
    scratch += [pltpu.VMEM((LANES, s), BF16),
                pltpu.VMEM((2, n_keys, n_q), F32), pltpu.VMEM((2, n_keys, n_q), BF16)]
    grid_spec = pltpu.PrefetchScalarGridSpec(
        num_scalar_prefetch=1 if has_sink else 0,
        grid=(b, n_groups),
        in_specs=in_specs,
        out_specs=out_spec,
        scratch_shapes=scratch,
    )
    args = ((sinks,) if has_sink else ()) + (proj, proj, proj, bias)
    return pl.pallas_call(
        kernel,
        out_shape=jax.ShapeDtypeStruct((b, s, n_groups * qw), BF16),
        grid_spec=grid_spec,
        compiler_params=pltpu.CompilerParams(
            dimension_semantics=("arbitrary", "arbitrary"), vmem_limit_bytes=VMEM_LIMIT),
        name="attn_a" if dup else "attn_b",
    )(*args)


def _band_tables(rel_bias_b):
    qi = jnp.arange(QBLK)[:, None]
    r = qi // CHUNK

    def table(nw, n_prev, fn):
        kj = jnp.arange(nw * QBLK)[None, :]
        rel = (nw - 1) * QBLK + qi - kj
        inband = (kj >= r * CHUNK) & (kj < (r + n_prev + 1) * CHUNK)
        return jnp.where(inband[None], fn(rel), NEG_INF)

    slopes = jnp.exp2(-8.0 * jnp.arange(1, N_HEADS_A + 1, dtype=F32) / N_HEADS_A)
    nw_a = (N_PREV_A * CHUNK) // QBLK + 1
    nw_b = (N_PREV_B * CHUNK) // QBLK + 1
    bias_a = table(nw_a, N_PREV_A, lambda rel: -slopes[:, None, None] * jnp.abs(rel).astype(F32)[None])
    w_b = nw_b * QBLK
    row = -(-(QBLK + w_b - 2) // LANES) * LANES
    p = row + 1
    t = np.arange(p)
    m = np.where(t < w_b, t, t - p)
    dist = np.clip((nw_b - 1) * QBLK - m, -REL_CLIP, REL_CLIP) + REL_CLIP
    vec = rel_bias_b[:, dist].astype(F32)
    reps = -(-(QBLK * row) // p)
    rel_b = jnp.tile(vec, (1, reps))[:, :QBLK * row].reshape(N_HEADS_B, QBLK, row)[:, :, :w_b]
    bias_b = table(nw_b, N_PREV_B, lambda rel: rel_b)
    rep = N_HEADS_A // N_KV_A
    bias_a = bias_a.reshape(N_KV_A, rep * QBLK, nw_a * QBLK).transpose(0, 2, 1)
    bias_b = bias_b.reshape(N_HEADS_B // 2, 2 * QBLK, nw_b * QBLK).transpose(0, 2, 1)
    return bias_a * LOG2E, nw_a, bias_b * LOG2E, nw_b


def _mix_kernel(oa_ref, ob_ref, x_ref, mod_ref, ga_ref, gb_ref, gf_ref, wo_ref, wr_ref, br_ref,
                x1_ref, h2_ref, rt_ref, ids_ref):
    half = oa_ref.shape[2]
    na = _rms(oa_ref[0].astype(F32), ga_ref[...]).astype(BF16)
    nb = _rms(ob_ref[0].astype(F32), gb_ref[...]).astype(BF16)
    acc = jnp.dot(na, wo_ref[:half, :], preferred_element_type=F32)
    acc = acc + jnp.dot(nb, wo_ref[half:, :], preferred_element_type=F32)
    x1 = x_ref[0] + mod_ref[0, 2:3, :] * acc
    x1_ref[0] = x1
    h2 = _rms(x1, gf_ref[...]) * (1.0 + mod_ref[0, 4:5, :]) + mod_ref[0, 3:4, :]
    h2_ref[0] = h2

    tm = h2.shape[0]
    h_hi = h2.astype(BF16)
    h_lo = (h2 - h_hi.astype(F32)).astype(BF16)
    r = jnp.dot(jnp.concatenate([h_hi, h_lo], axis=0), wr_ref[...], preferred_element_type=F32)
    r = r[:tm] + r[tm:]
    logits = r + pltpu.roll(r, LANES - ROUTER_LO_LANE, 1) + br_ref[...]

    lane = lax.broadcasted_iota(jnp.int32, (tm, LANES), 1)
    lane_f = lane.astype(F32)
    big = float(LANES)
    ninf = -jnp.inf

    def first_max(vals):
        top = jnp.max(vals, axis=-1, keepdims=True)
        idx = jnp.min(jnp.where(vals == top, lane_f, big), axis=-1, keepdims=True)
        return top, idx

    is_g = lane < N_GROUPS
    g_top, g_idx = first_max(jnp.where(is_g, logits, ninf))
    p_g = 1.0 / jnp.sum(jnp.where(is_g, jnp.exp(logits - g_top), 0.0), axis=-1, keepdims=True)
    lo = N_GROUPS + g_idx * EXPERTS_PER_GROUP
    e_vals = jnp.where((lane_f >= lo) & (lane_f < lo + EXPERTS_PER_GROUP), logits, ninf)
    v1, i1 = first_max(e_vals)
    v2, i2 = first_max(jnp.where(lane_f == i1, ninf, e_vals))
    e2 = jnp.exp(v2 - v1)
    w1 = p_g / (1.0 + e2)
    w2 = p_g * e2 / (1.0 + e2)
    rt = jnp.where(lane == 0, i1 - N_GROUPS,
                   jnp.where(lane == 1, i2 - N_GROUPS,
                             jnp.where(lane == 2, w1, jnp.where(lane == 3, w2, 0.0))))
    rt_ref[...] = rt
    pick = (lax.broadcasted_iota(jnp.int32, (8, LANES), 0)
            == lax.broadcasted_iota(jnp.int32, (8, LANES), 1)).astype(BF16)
    ids_ref[...] = lax.dot_general(pick, rt.astype(BF16), (((1,), (1,)), ((), ())),
                                   preferred_element_type=F32)


def _mix(o_a, o_b, x, mod, g_out_a, g_out_b, g_ffn, w_out_bf16, w_router, b_router):
    b, s, d = x.shape
    half = o_a.shape[2]
    tm = 512
    nt = s // tm
    vec = lambda n: pl.BlockSpec((1, n), lambda bi, i: (0, 0))
    return pl.pallas_call(
        _mix_kernel,
        out_shape=(jax.ShapeDtypeStruct((b, s, d), F32),
                   jax.ShapeDtypeStruct((b, s, d), F32),
                   jax.ShapeDtypeStruct((b * s, LANES), F32),
                   jax.ShapeDtypeStruct((8, b * s), F32)),
        grid=(b, nt),
        in_specs=[
            pl.BlockSpec((1, tm, half), lambda bi, i: (bi, i, 0)),
            pl.BlockSpec((1, tm, half), lambda bi, i: (bi, i, 0)),
            pl.BlockSpec((1, tm, d), lambda bi, i: (bi, i, 0)),
            pl.BlockSpec((1, 6, d), lambda bi, i: (bi, 0, 0)),
            vec(half), vec(half), vec(d),
            pl.BlockSpec((d, d), lambda bi, i: (0, 0), pipeline_mode=pl.Buffered(1)),
            pl.BlockSpec((d, LANES), lambda bi, i: (0, 0)),
            vec(LANES),
        ],
        out_specs=(pl.BlockSpec((1, tm, d), lambda bi, i: (bi, i, 0)),
                   pl.BlockSpec((1, tm, d), lambda bi, i: (bi, i, 0)),
                   pl.BlockSpec((tm, LANES), lambda bi, i: (bi * nt + i, 0)),
                   pl.BlockSpec((8, tm), lambda bi, i: (0, bi * nt + i))),
        compiler_params=pltpu.CompilerParams(
            dimension_semantics=("arbitrary", "arbitrary"), vmem_limit_bytes=VMEM_LIMIT),
        name="mix_out",
    )(o_a, o_b, x, mod, g_out_a.reshape(1, half), g_out_b.reshape(1, half), g_ffn.reshape(1, d),
      w_out_bf16, w_router, b_router.reshape(1, LANES))


MOE_TM = 512


ROW_BUFS = 3
DISPATCH_TM = 256
DISPATCH_STEPS_MIN = ROW_BUFS


def _dispatch_kernel(slot_ref, zf_ref, h_ref, xs_hbm, buf, zbuf, sem, zsem):
    i = pl.program_id(0)
    n_steps = pl.num_programs(0)
    td = h_ref.shape[0]
    tm = zbuf.shape[0]
    n_tiles_max = xs_hbm.shape[0] // tm

    @pl.when(i == 0)
    def _():
        zbuf[...] = jnp.zeros(zbuf.shape, zbuf.dtype)

        def fill_copy(t):
            z0 = pl.multiple_of(zf_ref[t], 8)
            n = pl.multiple_of(tm - z0, 8)
            row0 = pl.multiple_of(t * tm + z0, 8)
            return pltpu.make_async_copy(zbuf.at[pl.ds(0, n)], xs_hbm.at[pl.ds(row0, n)], zsem.at[0])

        def fill(t, carry):
            @pl.when(zf_ref[t] < tm)
            def _():
                fill_copy(t).start()
            return carry
        lax.fori_loop(0, n_tiles_max, fill, 0)

        def drain(t, carry):
            @pl.when(zf_ref[t] < tm)
            def _():
                fill_copy(t).wait()
            return carry
        lax.fori_loop(0, n_tiles_max, drain, 0)

    def wait_rows(s):
        for _ in range(2):
            pltpu.make_async_copy(buf.at[s], xs_hbm.at[pl.ds(0, td)], sem.at[s]).wait()

    def step(s):
        @pl.when(i >= ROW_BUFS)
        def _():
            wait_rows(s)

        buf[s] = h_ref[...]
        for r in range(td):
            for k in range(2):
                dst = slot_ref[k * (n_steps * td) + i * td + r]
                pltpu.make_async_copy(buf.at[s, pl.ds(r, 1)], xs_hbm.at[pl.ds(dst, 1)],
                                      sem.at[s]).start(priority=k)

        @pl.when(i == n_steps - 1)
        def _():
            for t in range(min(ROW_BUFS, DISPATCH_STEPS_MIN)):
                wait_rows((s + ROW_BUFS - t) % ROW_BUFS)

    for s in range(ROW_BUFS):
        pl.when(i % ROW_BUFS == s)(functools.partial(step, s))


def _dispatch(h2, slot, zero_fill, n_pad):
    t, d = h2.shape
    tm = MOE_TM
    assert t // DISPATCH_TM >= DISPATCH_STEPS_MIN
    grid_spec = pltpu.PrefetchScalarGridSpec(
        num_scalar_prefetch=2,
        grid=(t // DISPATCH_TM,),
        in_specs=[pl.BlockSpec((DISPATCH_TM, d), lambda i, *_: (i, 0))],
        out_specs=pl.BlockSpec(memory_space=pl.ANY),
        scratch_shapes=[
            pltpu.VMEM((ROW_BUFS, DISPATCH_TM, d), F32),
            pltpu.VMEM((tm, d), F32),
            pltpu.SemaphoreType.DMA((ROW_BUFS,)),
            pltpu.SemaphoreType.DMA((1,)),
        ],
    )
    return pl.pallas_call(
        _dispatch_kernel,
        out_shape=jax.ShapeDtypeStruct((n_pad, d), F32),
        grid_spec=grid_spec,
        compiler_params=pltpu.CompilerParams(
            dimension_semantics=("arbitrary",), vmem_limit_bytes=VMEM_LIMIT),
        name="dispatch",
    )(slot, zero_fill, h2)


def _moe_kernel(te_ref, nxe_ref, nt_ref, x_ref, wg_hbm, wu_hbm, wd_hbm, o_ref,
                wg32, wu32, wd32, wgb, wub, wdb, wsem):
    i = pl.program_id(0)
    n_tiles = nt_ref[0]

    def weight_copies(e):
        return [pltpu.make_async_copy(src.at[e], dst, wsem.at[0])
                for src, dst in ((wg_hbm, wg32), (wu_hbm, wu32), (wd_hbm, wd32))]

    @pl.when(i == 0)
    def _():
        for cp in weight_copies(te_ref[0]):
            cp.start()

    @pl.when(i < n_tiles)
    def _():
        @pl.when(jnp.logical_or(i == 0, te_ref[i] != te_ref[jnp.maximum(i - 1, 0)]))
        def _():
            for cp in weight_copies(0):
                cp.wait()
            wgb[...] = wg32[...].astype(BF16)
            wub[...] = wu32[...].astype(BF16)
            wdb[...] = wd32[...].astype(BF16)

            @pl.when(nxe_ref[i] >= 0)
            def _():
                for cp in weight_copies(nxe_ref[i]):
                    cp.start()

        xb = x_ref[...].astype(BF16)
        g = jnp.dot(xb, wgb[...], preferred_element_type=F32)
        u = jnp.dot(xb, wub[...], preferred_element_type=F32)
        a = (g * jax.nn.sigmoid(g) * u).astype(BF16)
        o_ref[...] = jnp.dot(a, wdb[...], preferred_element_type=F32)

    @pl.when(i >= n_tiles)
    def _():
        o_ref[...] = jnp.zeros(o_ref.shape, o_ref.dtype)


def _moe(xs, w_gate, w_up, w_down, tile_expert, next_expert, n_tiles):
    n_pad, d = xs.shape
    tm = MOE_TM
    max_tiles = n_pad // tm
    de = w_gate.shape[2]
    grid_spec = pltpu.PrefetchScalarGridSpec(
        num_scalar_prefetch=3,
        grid=(max_tiles,),
        in_specs=[pl.BlockSpec((tm, d), lambda i, te, nxe, nt: (jnp.minimum(i, nt[0] - 1), 0))]
        + [pl.BlockSpec(memory_space=pl.ANY)] * 3,
        out_specs=pl.BlockSpec((tm, d), lambda i, *_: (i, 0)),
        scratch_shapes=[
            pltpu.VMEM((d, de), F32),
            pltpu.VMEM((d, de), F32),
            pltpu.VMEM((de, d), F32),
            pltpu.VMEM((d, de), BF16),
            pltpu.VMEM((d, de), BF16),
            pltpu.VMEM((de, d), BF16),
            pltpu.SemaphoreType.DMA((1,)),
        ],
    )
    return pl.pallas_call(
        _moe_kernel,
        out_shape=jax.ShapeDtypeStruct((n_pad, d), F32),
        grid_spec=grid_spec,
        compiler_params=pltpu.CompilerParams(
            dimension_semantics=("arbitrary",), vmem_limit_bytes=VMEM_LIMIT),
        name="moe",
    )(tile_expert, next_expert, n_tiles, xs, w_gate, w_up, w_down)


def _route_plan(ids, n_tok):
    tm = MOE_TM
    n_asg = 2 * n_tok
    n_pad = n_asg + N_EXPERTS * tm
    max_tiles = n_pad // tm
    e_flat = ids[:2].astype(jnp.int32).reshape(-1)
    onehot = e_flat[None, :] == jnp.arange(N_EXPERTS)[:, None]
    blk = LANES
    oh3 = onehot.reshape(N_EXPERTS, n_asg // blk, blk).astype(BF16)
    upper = (jnp.arange(blk)[:, None] <= jnp.arange(blk)[None, :]).astype(BF16)
    within = jnp.einsum("ebj,jk->ebk", oh3, upper, preferred_element_type=F32).astype(jnp.int32)
    blk_tot = within[:, :, -1]
    offs = jnp.cumsum(blk_tot, axis=1) - blk_tot
    csum = (within + offs[:, :, None]).reshape(N_EXPERTS, n_asg)
    counts = jnp.sum(blk_tot, axis=1)
    oh_i = onehot.astype(jnp.int32)
    rank = jnp.sum(csum * oh_i, axis=0) - 1
    tiles_e = (counts + tm - 1) // tm
    tile_end = jnp.cumsum(tiles_e)
    base = (tile_end - tiles_e) * tm
    slot = jnp.sum(oh_i * base[:, None], axis=0) + rank
    n_tiles = tile_end[-1]
    tile_id = jnp.arange(max_tiles)
    te = jnp.sum((tile_id[:, None] >= tile_end[None, :]).astype(jnp.int32), axis=1)
    te_last = jnp.sum(((n_tiles - 1) >= tile_end).astype(jnp.int32))
    tile_expert = jnp.where(tile_id < n_tiles, te, te_last).astype(jnp.int32)
    ex = jnp.arange(N_EXPERTS)
    later = (ex[None, :] > ex[:, None]) & (counts[None, :] > 0)
    nxt_e = jnp.min(jnp.where(later, ex[None, :], N_EXPERTS), axis=1)
    nxt_e = jnp.where(nxt_e < N_EXPERTS, nxt_e, -1)
    sel = (tile_expert[:, None] == ex[None, :]).astype(jnp.int32)
    next_expert = jnp.sum(sel * nxt_e[None, :], axis=1).astype(jnp.int32)
    last_tile = jnp.sum(sel * (tile_end - 1)[None, :], axis=1)
    used_last = jnp.sum(sel * (counts - (tiles_e - 1) * tm)[None, :], axis=1)
    zero_fill = jnp.where(tile_id >= n_tiles, 0,
                          jnp.where(tile_id == last_tile, (used_last // 8) * 8, tm)).astype(jnp.int32)
    return (tile_expert, next_expert, n_tiles.reshape(1).astype(jnp.int32), zero_fill,
            slot.astype(jnp.int32), n_pad)


FINAL_TM = 256


def _final_kernel(slot_ref, x1_ref, rt_ref, mod_ref, modf_ref, g_ref, y_hbm, o_ref, ybuf, gsem):
    i = pl.program_id(0)
    n_steps = pl.num_programs(0)
    tm = x1_ref.shape[0]

    def row_in(tile, r, k, s):
        idx = slot_ref[k * (n_steps * tm) + tile * tm + r]
        pltpu.make_async_copy(y_hbm.at[pl.ds(idx, 1)], ybuf.at[s, pl.ds(k * tm + r, 1)],
                              gsem.at[s]).start(priority=k)

    def wait_in(s):
        pltpu.make_async_copy(y_hbm.at[pl.ds(0, 2 * tm)], ybuf.at[s], gsem.at[s]).wait()

    @pl.when(i == 0)
    def _():
        for t in range(ROW_BUFS - 1):
            tile = jnp.minimum(t, n_steps - 1)

            def row(r, carry):
                row_in(tile, r, 0, t)
                row_in(tile, r, 1, t)
                return carry
            lax.fori_loop(0, tm, row, 0, unroll=8)

    def step(s):
        wait_in(s)
        rt = rt_ref[...]
        y = rt[:, 2:3] * ybuf[s, :tm, :] + rt[:, 3:4] * ybuf[s, tm:, :]
        nxt = jnp.minimum(i + ROW_BUFS - 1, n_steps - 1)
        for r in range(tm):
            row_in(nxt, r, 0, (s + ROW_BUFS - 1) % ROW_BUFS)
            row_in(nxt, r, 1, (s + ROW_BUFS - 1) % ROW_BUFS)
        x2 = x1_ref[...] + mod_ref[0, 5:6, :] * y
        o_ref[...] = _rms(x2, g_ref[...]) * (1.0 + modf_ref[0, 1:2, :]) + modf_ref[0, 0:1, :]

        @pl.when(i == n_steps - 1)
        def _():
            for t in range(1, ROW_BUFS):
                wait_in((s + t) % ROW_BUFS)

    for s in range(ROW_BUFS):
        pl.when(i % ROW_BUFS == s)(functools.partial(step, s))


def _final(x1, y_rows, slot, rt, mod, modf, g_final):
    b, s, d = x1.shape
    tm = FINAL_TM
    nt = s // tm
    grid_spec = pltpu.PrefetchScalarGridSpec(
        num_scalar_prefetch=1,
        grid=(b * nt,),
        in_specs=[
            pl.BlockSpec((tm, d), lambda i, *_: (i, 0)),
            pl.BlockSpec((tm, LANES), lambda i, *_: (i, 0)),
            pl.BlockSpec((1, 6, d), lambda i, *_: (i // nt, 0, 0)),
            pl.BlockSpec((1, 2, d), lambda i, *_: (i // nt, 0, 0)),
            pl.BlockSpec((1, d), lambda i, *_: (0, 0)),
            pl.BlockSpec(memory_space=pl.ANY),
        ],
        out_specs=pl.BlockSpec((tm, d), lambda i, *_: (i, 0)),
        scratch_shapes=[
            pltpu.VMEM((ROW_BUFS, 2 * tm, d), F32),
            pltpu.SemaphoreType.DMA((ROW_BUFS,)),
        ],
    )
    out = pl.pallas_call(
        _final_kernel,
        out_shape=jax.ShapeDtypeStruct((b * s, d), F32),
        grid_spec=grid_spec,
        compiler_params=pltpu.CompilerParams(
            dimension_semantics=("arbitrary",), vmem_limit_bytes=VMEM_LIMIT),
        name="final",
    )(slot, x1.reshape(b * s, d), rt, mod, modf, g_final.reshape(1, d), y_rows)
    return out.reshape(b, s, d)


def kernel(x, c, w_ada, b_ada, g_mix, w_in, sinks_a, rel_bias_b, g_out_a, g_out_b, w_out, g_ffn,
           w_router_group, b_router_group, w_router_expert, b_router_expert, w_gate, w_up, w_down,
           w_ada_final, b_ada_final, g_final):
    b, s, d = x.shape
    assert w_ada.shape[0] == 1, "one layer"
    n_tok = b * s

    c_act = jax.nn.silu(c)
    a_rep = jnp.broadcast_to(c_act[:, :, None], (b, d, LANES))
    mod = _ada(a_rep, w_ada[0], b_ada[0]).reshape(b, 6, d)
    modf = _ada(a_rep, w_ada_final, b_ada_final).reshape(b, 2, d)

    proj = _proj(x, mod, g_mix[0], w_in[0].astype(BF16))

    bias_a, nw_a, bias_b, nw_b = _band_tables(rel_bias_b[0])
    kv_a0 = DA_Q // LANES
    o_a = _attention(proj, bias_a, sinks_a[0].astype(F32), n_groups=N_KV_A, ncol=2, nw=nw_a,
                     q_col0=0, k_col0=kv_a0, v_col0=kv_a0 + DA_KV // LANES, kv_share=2, dup=True)
    qb0 = (DA_Q + 2 * DA_KV) // LANES
    o_b = _attention(proj, bias_b, None, n_groups=N_HEADS_B // 2, ncol=1, nw=nw_b,
                     q_col0=qb0, k_col0=qb0 + DB // LANES, v_col0=qb0 + 2 * DB // LANES,
                     kv_share=1, dup=False)

    n_r = N_GROUPS + N_EXPERTS
    assert n_r <= ROUTER_LO_LANE
    w_r = jnp.concatenate([w_router_group[0], w_router_expert[0]], axis=1)
    w_r_hi = w_r.astype(BF16)
    w_r_lo = (w_r - w_r_hi.astype(F32)).astype(BF16)
    w_router = (jnp.zeros((d, LANES), BF16).at[:, :n_r].set(w_r_hi)
                .at[:, ROUTER_LO_LANE:ROUTER_LO_LANE + n_r].set(w_r_lo))
    b_router = jnp.zeros((LANES,), F32).at[:n_r].set(
        jnp.concatenate([b_router_group[0], b_router_expert[0]]))
    x1, h2, rt, ids = _mix(o_a, o_b, x, mod, g_out_a[0], g_out_b[0], g_ffn[0], w_out[0].astype(BF16),
                      w_router, b_router)

    tile_expert, next_expert, n_tiles, zero_fill, slot, n_pad = _route_plan(ids, n_tok)
    xs = _dispatch(h2.reshape(n_tok, d), slot, zero_fill, n_pad)
    y = _moe(xs, w_gate[0], w_up[0], w_down[0], tile_expert, next_expert, n_tiles)

    return _final(x1, y, slot, rt, mod, modf, g_final)
```

```python
import functools

import jax
import jax.numpy as jnp
import numpy as np
from jax import lax
from jax.experimental import pallas as pl
from jax.experimental.pallas import tpu as pltpu

D_MODEL = 2048
CHUNK = 64
HEAD_DIM = 64
N_HEADS_A = 16
N_KV_A = 4
N_PREV_A = 2
N_HEADS_B = 16
N_PREV_B = 8
REL_CLIP = 128
DA_Q = N_HEADS_A * HEAD_DIM
DA_KV = N_KV_A * HEAD_DIM
DB = N_HEADS_B * HEAD_DIM
D_IN = DA_Q + 2 * DA_KV + 3 * DB
N_GROUPS = 4
EXPERTS_PER_GROUP = 8
N_EXPERTS = N_GROUPS * EXPERTS_PER_GROUP
D_EXPERT = D_MODEL // 4
EPS = 1e-6
NEG_INF = -1e30
LOG2E = 1.4426950408889634

LANES = 128
ROUTER_LO_LANE = 64
QBLK = 2 * CHUNK
VMEM_LIMIT = 56 * 1024 * 1024

F32 = jnp.float32
BF16 = jnp.bfloat16


def _rms(x, g):
    return x * lax.rsqrt(jnp.mean(x * x, axis=-1, keepdims=True) + EPS) * g


ADA_JB = 8


def _ada_kernel(a_ref, w_ref, b_ref, o_ref):
    n_b, k, _ = a_ref.shape
    tn = w_ref.shape[1]
    sub = 8
    for jb in range(tn // (ADA_JB * LANES)):
        col0 = jb * ADA_JB * LANES

        def body(kc, accs):
            k0 = pl.multiple_of(kc * sub, sub)
            a_rows = [a_ref[b, pl.ds(k0, sub), :] for b in range(n_b)]
            out = []
            for j in range(ADA_JB):
                w = w_ref[pl.ds(k0, sub), col0 + j * LANES:col0 + (j + 1) * LANES]
                out.append([accs[j][b] + a_rows[b] * w for b in range(n_b)])
            return out

        zero = jnp.zeros((sub, LANES), F32)
        accs = lax.fori_loop(0, k // sub, body, [[zero] * n_b for _ in range(ADA_JB)], unroll=4)
        for j in range(ADA_JB):
            cols = slice(col0 + j * LANES, col0 + (j + 1) * LANES)
            for b in range(n_b):
                o_ref[b:b + 1, cols] = jnp.sum(accs[j][b], axis=0, keepdims=True) + b_ref[:, cols]


def _ada(a_rep, w, bias):
    n_b, k, _ = a_rep.shape
    n = w.shape[1]
    tn = 2048
    return pl.pallas_call(
        _ada_kernel,
        out_shape=jax.ShapeDtypeStruct((n_b, n), F32),
        grid=(n // tn,),
        in_specs=[
            pl.BlockSpec((n_b, k, LANES), lambda j: (0, 0, 0)),
            pl.BlockSpec((k, tn), lambda j: (0, j)),
            pl.BlockSpec((1, tn), lambda j: (0, j)),
        ],
        out_specs=pl.BlockSpec((n_b, tn), lambda j: (0, j)),
        compiler_params=pltpu.CompilerParams(
            dimension_semantics=("arbitrary",), vmem_limit_bytes=VMEM_LIMIT),
        name="ada",
    )(a_rep, w, bias.reshape(1, n))


def _proj_kernel(x_ref, mod_ref, g_ref, w_ref, o_ref, *, n_chunk):
    h = _rms(x_ref[0], g_ref[...])
    h = h * (1.0 + mod_ref[0, 1:2, :]) + mod_ref[0, 0:1, :]
    hb = h.astype(BF16)
    for n0 in range(0, o_ref.shape[2], n_chunk):
        cols = slice(n0, n0 + n_chunk)
        o_ref[0, :, cols] = jnp.dot(hb, w_ref[:, cols], preferred_element_type=F32).astype(BF16)


def _proj(x, mod, g_mix, w_in_bf16):
    b, s, d = x.shape
    n = w_in_bf16.shape[1]
    tm = 512
    return pl.pallas_call(
        functools.partial(_proj_kernel, n_chunk=512),
        out_shape=jax.ShapeDtypeStruct((b, s, n), BF16),
        grid=(b, s // tm),
        in_specs=[
            pl.BlockSpec((1, tm, d), lambda bi, i: (bi, i, 0)),
            pl.BlockSpec((1, 6, d), lambda bi, i: (bi, 0, 0)),
            pl.BlockSpec((1, d), lambda bi, i: (0, 0)),
            pl.BlockSpec((d, n), lambda bi, i: (0, 0), pipeline_mode=pl.Buffered(1)),
        ],
        out_specs=pl.BlockSpec((1, tm, n), lambda bi, i: (bi, i, 0)),
        compiler_params=pltpu.CompilerParams(
            dimension_semantics=("arbitrary", "arbitrary"), vmem_limit_bytes=VMEM_LIMIT),
        name="proj",
    )(x, mod, g_mix.reshape(1, d), w_in_bf16)


def _attn_kernel(*refs, ncol, nw, dup, has_sink):
    if has_sink:
        sink_ref, q_ref, k_ref, v_ref, bias_ref, o_ref = refs[:6]
        scratch = refs[6:]
    else:
        q_ref, k_ref, v_ref, bias_ref, o_ref = refs[:5]
        scratch = refs[5:]
        sink_ref = None
    s_len = q_ref.shape[1]
    n_blk = s_len // QBLK
    n_stack = 2 * ncol
    grp = pl.program_id(1)

    rows = 512
    lane = lax.broadcasted_iota(jnp.int32, (QBLK, LANES), 1)
    low = lane < HEAD_DIM
    if dup:
        kd_ref, vt_ref = scratch[:2]
        half = grp % 2
        keep = (lax.broadcasted_iota(jnp.int32, (rows, LANES), 1) // HEAD_DIM) == half

        def spread(t):
            return jnp.where(keep, t, pltpu.roll(t, HEAD_DIM, 1))

        def dup_body(c, carry):
            r0 = pl.multiple_of(c * rows, rows)
            kd_ref[pl.ds(r0, rows), :] = spread(k_ref[0, pl.ds(r0, rows), :].astype(F32)).astype(BF16)
            return carry

        lax.fori_loop(0, s_len // rows, dup_body, 0)
        k_src = kd_ref
    else:
        vt_ref = scratch[0]
        k_src = k_ref.at[0]

        def spread(t):
            return t

    for c in range(s_len // rows):
        t = spread(v_ref[0, c * rows:(c + 1) * rows, :].astype(F32))
        vt_ref[:, c * rows:(c + 1) * rows] = t.T.astype(BF16)

    top = lax.broadcasted_iota(jnp.int32, (LANES, QBLK), 0) < HEAD_DIM

    if has_sink:
        sink = jnp.concatenate(
            [jnp.full((1, QBLK), sink_ref[grp * n_stack + h] * LOG2E, F32) for h in range(n_stack)],
            axis=1)

    def rows_of(j):
        return j * QBLK if isinstance(j, int) else pl.multiple_of(j * QBLK, QBLK)

    def logits(j, nvb):
        qf = q_ref[0, pl.ds(rows_of(j), QBLK), :].astype(F32) * (HEAD_DIM ** -0.5 * LOG2E)
        parts = []
        for c in range(ncol):
            qc = qf[:, c * LANES:(c + 1) * LANES]
            parts.append(jnp.where(low, qc, 0.0))
            parts.append(jnp.where(low, 0.0, qc))
        lhs = jnp.concatenate(parts, axis=0).astype(BF16)
        kw = k_src[pl.ds(rows_of(j - (nvb - 1)), nvb * QBLK), :]
        s = lax.dot_general(kw, lhs, (((1,), (1,)), ((), ())), preferred_element_type=F32)
        return s + bias_ref[0, (nw - nvb) * QBLK:, :]

    def softmax(s):
        m = jnp.max(s, axis=0, keepdims=True)
        if has_sink:
            m = jnp.maximum(m, sink)
        p = jnp.exp2(s - m)
        denom = jnp.sum(p, axis=0, keepdims=True)
        if has_sink:
            denom = denom + jnp.exp2(sink - m)
        return p.astype(BF16), denom

    def emit(j, nvb, p, denom):
        vw = vt_ref[:, pl.ds(rows_of(j - (nvb - 1)), nvb * QBLK)]
        o = jnp.dot(vw, p, preferred_element_type=F32) / denom
        for c in range(ncol):
            o0 = o[:, (2 * c) * QBLK:(2 * c + 1) * QBLK]
            o1 = o[:, (2 * c + 1) * QBLK:(2 * c + 2) * QBLK]
            o_ref[0, pl.ds(rows_of(j), QBLK), c * LANES:(c + 1) * LANES] = (
                jnp.where(top, o0, o1).T.astype(BF16))

    s_scr, p_scr = scratch[-2:]
    last = n_blk - 1

    def n_valid(j):
        return min(j + 1, nw) if isinstance(j, int) else nw

    def put_logits(u, j):
        s_scr[u, :n_valid(j) * QBLK, :] = logits(j, n_valid(j))

    def put_probs(u, j):
        rows_n = n_valid(j) * QBLK
        p, den = softmax(s_scr[u, :rows_n, :])
        p_scr[u, :rows_n, :] = p
        return den

    def pair(j, d_even, clamp):
        emit(j, n_valid(j), p_scr[0, :n_valid(j) * QBLK, :], d_even)
        d_odd = put_probs(1, j + 1)
        put_logits(0, clamp(j + 2))
        emit(j + 1, n_valid(j + 1), p_scr[1, :n_valid(j + 1) * QBLK, :], d_odd)
        d_next = put_probs(0, clamp(j + 2))
        put_logits(1, clamp(j + 3))
        return d_next

    first = nw - 1 + (nw - 1) % 2
    assert n_blk % 2 == 0 and first + 4 <= n_blk
    put_logits(0, 0)
    den = put_probs(0, 0)
    put_logits(1, 1)
    for j in range(0, first, 2):
        den = pair(j, den, lambda k: k)

    lax.fori_loop(0, (n_blk - first) // 2,
                  lambda t, d: pair(first + 2 * t, d, lambda k: jnp.minimum(k, last)), den)


def _attention(proj, bias, sinks, *, n_groups, ncol, nw, q_col0, k_col0, v_col0, kv_share, dup):
    b, s, _ = proj.shape
    qw = ncol * LANES
    has_sink = sinks is not None
    kernel = functools.partial(_attn_kernel, ncol=ncol, nw=nw, dup=dup, has_sink=has_sink)
    _, n_keys, n_q = bias.shape
    in_specs = [
        pl.BlockSpec((1, s, qw), lambda bi, g, *_: (bi, 0, q_col0 // ncol + g)),
        pl.BlockSpec((1, s, LANES), lambda bi, g, *_: (bi, 0, k_col0 + g // kv_share)),
        pl.BlockSpec((1, s, LANES), lambda bi, g, *_: (bi, 0, v_col0 + g // kv_share)),
        pl.BlockSpec((1, n_keys, n_q), lambda bi, g, *_: (g, 0, 0)),
    ]
    out_spec = pl.BlockSpec((1, s, qw), lambda bi, g, *_: (bi, 0, g))
    scratch = [pltpu.VMEM((s, LANES), BF16)] if dup else []
    scratch += [pltpu.VMEM((LANES, s), BF16),
                pltpu.VMEM((2, n_keys, n_q), F32), pltpu.VMEM((2, n_keys, n_q), BF16)]
    grid_spec = pltpu.PrefetchScalarGridSpec(
        num_scalar_prefetch=1 if has_sink else 0,
        grid=(b, n_groups),
        in_specs=in_specs,
        out_specs=out_spec,
        scratch_shapes=scratch,
    )
    args = ((sinks,) if has_sink else ()) + (proj, proj, proj, bias)
    return pl.pallas_call(
        kernel,
        out_shape=jax.ShapeDtypeStruct((b, s, n_groups * qw), BF16),
        grid_spec=grid_spec,
        compiler_params=pltpu.CompilerParams(
            dimension_semantics=("arbitrary", "arbitrary"), vmem_limit_bytes=VMEM_LIMIT),
        name="attn_a" if dup else "attn_b",
    )(*args)


def _band_tables(rel_bias_b):
    qi = jnp.arange(QBLK)[:, None]
    r = qi // CHUNK

    def table(nw, n_prev, fn):
        kj = jnp.arange(nw * QBLK)[None, :]
        rel = (nw - 1) * QBLK + qi - kj
        inband = (kj >= r * CHUNK) & (kj < (r + n_prev + 1) * CHUNK)
        return jnp.where(inband[None], fn(rel), NEG_INF)

    slopes = jnp.exp2(-8.0 * jnp.arange(1, N_HEADS_A + 1, dtype=F32) / N_HEADS_A)
    nw_a = (N_PREV_A * CHUNK) // QBLK + 1
    nw_b = (N_PREV_B * CHUNK) // QBLK + 1
    bias_a = table(nw_a, N_PREV_A, lambda rel: -slopes[:, None, None] * jnp.abs(rel).astype(F32)[None])
    w_b = nw_b * QBLK
    row = -(-(QBLK + w_b - 2) // LANES) * LANES
    p = row + 1
    t = np.arange(p)
    m = np.where(t < w_b, t, t - p)
    dist = np.clip((nw_b - 1) * QBLK - m, -REL_CLIP, REL_CLIP) + REL_CLIP
    vec = rel_bias_b[:, dist].astype(F32)
    reps = -(-(QBLK * row) // p)
    rel_b = jnp.tile(vec, (1, reps))[:, :QBLK * row].reshape(N_HEADS_B, QBLK, row)[:, :, :w_b]
    bias_b = table(nw_b, N_PREV_B, lambda rel: rel_b)
    rep = N_HEADS_A // N_KV_A
    bias_a = bias_a.reshape(N_KV_A, rep * QBLK, nw_a * QBLK).transpose(0, 2, 1)
    bias_b = bias_b.reshape(N_HEADS_B // 2, 2 * QBLK, nw_b * QBLK).transpose(0, 2, 1)
    return bias_a * LOG2E, nw_a, bias_b * LOG2E, nw_b


def _mix_kernel(oa_ref, ob_ref, x_ref, mod_ref, ga_ref, gb_ref, gf_ref, wo_ref, wr_ref, br_ref,
                x1_ref, h2_ref, rt_ref, ids_ref):
    half = oa_ref.shape[2]
    na = _rms(oa_ref[0].astype(F32), ga_ref[...]).astype(BF16)
    nb = _rms(ob_ref[0].astype(F32), gb_ref[...]).astype(BF16)
    acc = jnp.dot(na, wo_ref[:half, :], preferred_element_type=F32)
    acc = acc + jnp.dot(nb, wo_ref[half:, :], preferred_element_type=F32)
    x1 = x_ref[0] + mod_ref[0, 2:3, :] * acc
    x1_ref[0] = x1
    h2 = _rms(x1, gf_ref[...]) * (1.0 + mod_ref[0, 4:5, :]) + mod_ref[0, 3:4, :]
    h2_ref[0] = h2

    tm = h2.shape[0]
    h_hi = h2.astype(BF16)
    h_lo = (h2 - h_hi.astype(F32)).astype(BF16)
    r = jnp.dot(jnp.concatenate([h_hi, h_lo], axis=0), wr_ref[...], preferred_element_type=F32)
    r = r[:tm] + r[tm:]
    logits = r + pltpu.roll(r, LANES - ROUTER_LO_LANE, 1) + br_ref[...]

    lane = lax.broadcasted_iota(jnp.int32, (tm, LANES), 1)
    lane_f = lane.astype(F32)
    big = float(LANES)
    ninf = -jnp.inf

    def first_max(vals):
        top = jnp.max(vals, axis=-1, keepdims=True)
        idx = jnp.min(jnp.where(vals == top, lane_f, big), axis=-1, keepdims=True)
        return top, idx

    is_g = lane < N_GROUPS
    g_top, g_idx = first_max(jnp.where(is_g, logits, ninf))
    p_g = 1.0 / jnp.sum(jnp.where(is_g, jnp.exp(logits - g_top), 0.0), axis=-1, keepdims=True)
    lo = N_GROUPS + g_idx * EXPERTS_PER_GROUP
    e_vals = jnp.where((lane_f >= lo) & (lane_f < lo + EXPERTS_PER_GROUP), logits, ninf)
    v1, i1 = first_max(e_vals)
    v2, i2 = first_max(jnp.where(lane_f == i1, ninf, e_vals))
    e2 = jnp.exp(v2 - v1)
    w1 = p_g / (1.0 + e2)
    w2 = p_g * e2 / (1.0 + e2)
    rt = jnp.where(lane == 0, i1 - N_GROUPS,
                   jnp.where(lane == 1, i2 - N_GROUPS,
                             jnp.where(lane == 2, w1, jnp.where(lane == 3, w2, 0.0))))
    rt_ref[...] = rt
    pick = (lax.broadcasted_iota(jnp.int32, (8, LANES), 0)
            == lax.broadcasted_iota(jnp.int32, (8, LANES), 1)).astype(BF16)
    ids_ref[...] = lax.dot_general(pick, rt.astype(BF16), (((1,), (1,)), ((), ())),
                                   preferred_element_type=F32)


def _mix(o_a, o_b, x, mod, g_out_a, g_out_b, g_ffn, w_out_bf16, w_router, b_router):
    b, s, d = x.shape
    half = o_a.shape[2]
    tm = 512
    nt = s // tm
    vec = lambda n: pl.BlockSpec((1, n), lambda bi, i: (0, 0))
    return pl.pallas_call(
        _mix_kernel,
        out_shape=(jax.ShapeDtypeStruct((b, s, d), F32),
                   jax.ShapeDtypeStruct((b, s, d), F32),
                   jax.ShapeDtypeStruct((b * s, LANES), F32),
                   jax.ShapeDtypeStruct((8, b * s), F32)),
        grid=(b, nt),
        in_specs=[
            pl.BlockSpec((1, tm, half), lambda bi, i: (bi, i, 0)),
            pl.BlockSpec((1, tm, half), lambda bi, i: (bi, i, 0)),
            pl.BlockSpec((1, tm, d), lambda bi, i: (bi, i, 0)),
            pl.BlockSpec((1, 6, d), lambda bi, i: (bi, 0, 0)),
            vec(half), vec(half), vec(d),
            pl.BlockSpec((d, d), lambda bi, i: (0, 0), pipeline_mode=pl.Buffered(1)),
            pl.BlockSpec((d, LANES), lambda bi, i: (0, 0)),
            vec(LANES),
        ],
        out_specs=(pl.BlockSpec((1, tm, d), lambda bi, i: (bi, i, 0)),
                   pl.BlockSpec((1, tm, d), lambda bi, i: (bi, i, 0)),
                   pl.BlockSpec((tm, LANES), lambda bi, i: (bi * nt + i, 0)),
                   pl.BlockSpec((8, tm), lambda bi, i: (0, bi * nt + i))),
        compiler_params=pltpu.CompilerParams(
            dimension_semantics=("arbitrary", "arbitrary"), vmem_limit_bytes=VMEM_LIMIT),
        name="mix_out",
    )(o_a, o_b, x, mod, g_out_a.reshape(1, half), g_out_b.reshape(1, half), g_ffn.reshape(1, d),
      w_out_bf16, w_router, b_router.reshape(1, LANES))


MOE_TM = 512


ROW_BUFS = 3
DISPATCH_TM = 512
DISPATCH_STEPS_MIN = ROW_BUFS


def _dispatch_kernel(slot_ref, zf_ref, h_ref, xs_hbm, buf, zbuf, sem, zsem):
    i = pl.program_id(0)
    n_steps = pl.num_programs(0)
    td = h_ref.shape[0]
    tm = zbuf.shape[0]
    n_tiles_max = xs_hbm.shape[0] // tm

    @pl.when(i == 0)
    def _():
        zbuf[...] = jnp.zeros(zbuf.shape, zbuf.dtype)

        def fill_copy(t):
            z0 = pl.multiple_of(zf_ref[t], 8)
            n = pl.multiple_of(tm - z0, 8)
            row0 = pl.multiple_of(t * tm + z0, 8)
            return pltpu.make_async_copy(zbuf.at[pl.ds(0, n)], xs_hbm.at[pl.ds(row0, n)], zsem.at[0])

        def fill(t, carry):
            @pl.when(zf_ref[t] < tm)
            def _():
                fill_copy(t).start()
            return carry
        lax.fori_loop(0, n_tiles_max, fill, 0)

        def drain(t, carry):
            @pl.when(zf_ref[t] < tm)
            def _():
                fill_copy(t).wait()
            return carry
        lax.fori_loop(0, n_tiles_max, drain, 0)

    def wait_rows(s):
        for _ in range(2):
            pltpu.make_async_copy(buf.at[s], xs_hbm.at[pl.ds(0, td)], sem.at[s]).wait()

    def step(s):
        @pl.when(i >= ROW_BUFS)
        def _():
            wait_rows(s)

        buf[s] = h_ref[...]
        for r in range(td):
            for k in range(2):
                dst = slot_ref[k * (n_steps * td) + i * td + r]
                pltpu.make_async_copy(buf.at[s, pl.ds(r, 1)], xs_hbm.at[pl.ds(dst, 1)],
                                      sem.at[s]).start(priority=k)

        @pl.when(i == n_steps - 1)
        def _():
            for t in range(min(ROW_BUFS, DISPATCH_STEPS_MIN)):
                wait_rows((s + ROW_BUFS - t) % ROW_BUFS)

    for s in range(ROW_BUFS):
        pl.when(i % ROW_BUFS == s)(functools.partial(step, s))


def _dispatch(h2, slot, zero_fill, n_pad):
    t, d = h2.shape
    tm = MOE_TM
    assert t // DISPATCH_TM >= DISPATCH_STEPS_MIN
    grid_spec = pltpu.PrefetchScalarGridSpec(
        num_scalar_prefetch=2,
        grid=(t // DISPATCH_TM,),
        in_specs=[pl.BlockSpec((DISPATCH_TM, d), lambda i, *_: (i, 0))],
        out_specs=pl.BlockSpec(memory_space=pl.ANY),
        scratch_shapes=[
            pltpu.VMEM((ROW_BUFS, DISPATCH_TM, d), F32),
            pltpu.VMEM((tm, d), F32),
            pltpu.SemaphoreType.DMA((ROW_BUFS,)),
            pltpu.SemaphoreType.DMA((1,)),
        ],
    )
    return pl.pallas_call(
        _dispatch_kernel,
        out_shape=jax.ShapeDtypeStruct((n_pad, d), F32),
        grid_spec=grid_spec,
        compiler_params=pltpu.CompilerParams(
            dimension_semantics=("arbitrary",), vmem_limit_bytes=VMEM_LIMIT),
        name="dispatch",
    )(slot, zero_fill, h2)


def _moe_kernel(te_ref, nxe_ref, nt_ref, x_ref, wg_hbm, wu_hbm, wd_hbm, o_ref,
                wg32, wu32, wd32, wgb, wub, wdb, wsem):
    i = pl.program_id(0)
    n_tiles = nt_ref[0]

    def weight_copies(e):
        return [pltpu.make_async_copy(src.at[e], dst, wsem.at[0])
                for src, dst in ((wg_hbm, wg32), (wu_hbm, wu32), (wd_hbm, wd32))]

    @pl.when(i == 0)
    def _():
        for cp in weight_copies(te_ref[0]):
            cp.start()

    @pl.when(i < n_tiles)
    def _():
        @pl.when(jnp.logical_or(i == 0, te_ref[i] != te_ref[jnp.maximum(i - 1, 0)]))
        def _():
            for cp in weight_copies(0):
                cp.wait()
            wgb[...] = wg32[...].astype(BF16)
            wub[...] = wu32[...].astype(BF16)
            wdb[...] = wd32[...].astype(BF16)

            @pl.when(nxe_ref[i] >= 0)
            def _():
                for cp in weight_copies(nxe_ref[i]):
                    cp.start()

        xb = x_ref[...].astype(BF16)
        g = jnp.dot(xb, wgb[...], preferred_element_type=F32)
        u = jnp.dot(xb, wub[...], preferred_element_type=F32)
        a = (g * jax.nn.sigmoid(g) * u).astype(BF16)
        o_ref[...] = jnp.dot(a, wdb[...], preferred_element_type=F32)

    @pl.when(i >= n_tiles)
    def _():
        o_ref[...] = jnp.zeros(o_ref.shape, o_ref.dtype)


def _moe(xs, w_gate, w_up, w_down, tile_expert, next_expert, n_tiles):
    n_pad, d = xs.shape
    tm = MOE_TM
    max_tiles = n_pad // tm
    de = w_gate.shape[2]
    grid_spec = pltpu.PrefetchScalarGridSpec(
        num_scalar_prefetch=3,
        grid=(max_tiles,),
        in_specs=[pl.BlockSpec((tm, d), lambda i, te, nxe, nt: (jnp.minimum(i, nt[0] - 1), 0))]
        + [pl.BlockSpec(memory_space=pl.ANY)] * 3,
        out_specs=pl.BlockSpec((tm, d), lambda i, *_: (i, 0)),
        scratch_shapes=[
            pltpu.VMEM((d, de), F32),
            pltpu.VMEM((d, de), F32),
            pltpu.VMEM((de, d), F32),
            pltpu.VMEM((d, de), BF16),
            pltpu.VMEM((d, de), BF16),
            pltpu.VMEM((de, d), BF16),
            pltpu.SemaphoreType.DMA((1,)),
        ],
    )
    return pl.pallas_call(
        _moe_kernel,
        out_shape=jax.ShapeDtypeStruct((n_pad, d), F32),
        grid_spec=grid_spec,
        compiler_params=pltpu.CompilerParams(
            dimension_semantics=("arbitrary",), vmem_limit_bytes=VMEM_LIMIT),
        name="moe",
    )(tile_expert, next_expert, n_tiles, xs, w_gate, w_up, w_down)


def _route_plan(ids, n_tok):
    tm = MOE_TM
    n_asg = 2 * n_tok
    n_pad = n_asg + N_EXPERTS * tm
    max_tiles = n_pad // tm
    e_flat = ids[:2].astype(jnp.int32).reshape(-1)
    onehot = e_flat[None, :] == jnp.arange(N_EXPERTS)[:, None]
    blk = LANES
    oh3 = onehot.reshape(N_EXPERTS, n_asg // blk, blk).astype(BF16)
    upper = (jnp.arange(blk)[:, None] <= jnp.arange(blk)[None, :]).astype(BF16)
    within = jnp.einsum("ebj,jk->ebk", oh3, upper, preferred_element_type=F32).astype(jnp.int32)
    blk_tot = within[:, :, -1]
    offs = jnp.cumsum(blk_tot, axis=1) - blk_tot
    csum = (within + offs[:, :, None]).reshape(N_EXPERTS, n_asg)
    counts = jnp.sum(blk_tot, axis=1)
    oh_i = onehot.astype(jnp.int32)
    rank = jnp.sum(csum * oh_i, axis=0) - 1
    tiles_e = (counts + tm - 1) // tm
    tile_end = jnp.cumsum(tiles_e)
    base = (tile_end - tiles_e) * tm
    slot = jnp.sum(oh_i * base[:, None], axis=0) + rank
    n_tiles = tile_end[-1]
    tile_id = jnp.arange(max_tiles)
    te = jnp.sum((tile_id[:, None] >= tile_end[None, :]).astype(jnp.int32), axis=1)
    te_last = jnp.sum(((n_tiles - 1) >= tile_end).astype(jnp.int32))
    tile_expert = jnp.where(tile_id < n_tiles, te, te_last).astype(jnp.int32)
    ex = jnp.arange(N_EXPERTS)
    later = (ex[None, :] > ex[:, None]) & (counts[None, :] > 0)
    nxt_e = jnp.min(jnp.where(later, ex[None, :], N_EXPERTS), axis=1)
    nxt_e = jnp.where(nxt_e < N_EXPERTS, nxt_e, -1)
    sel = (tile_expert[:, None] == ex[None, :]).astype(jnp.int32)
    next_expert = jnp.sum(sel * nxt_e[None, :], axis=1).astype(jnp.int32)
    last_tile = jnp.sum(sel * (tile_end - 1)[None, :], axis=1)
    used_last = jnp.sum(sel * (counts - (tiles_e - 1) * tm)[None, :], axis=1)
    zero_fill = jnp.where(tile_id >= n_tiles, 0,
                          jnp.where(tile_id == last_tile, (used_last // 8) * 8, tm)).astype(jnp.int32)
    return (tile_expert, next_expert, n_tiles.reshape(1).astype(jnp.int32), zero_fill,
            slot.astype(jnp.int32), n_pad)


FINAL_TM = 256


def _final_kernel(slot_ref, x1_ref, rt_ref, mod_ref, modf_ref, g_ref, y_hbm, o_ref, ybuf, gsem):
    i = pl.program_id(0)
    n_steps = pl.num_programs(0)
    tm = x1_ref.shape[0]

    def row_in(tile, r, k, s):
        idx = slot_ref[k * (n_steps * tm) + tile * tm + r]
        pltpu.make_async_copy(y_hbm.at[pl.ds(idx, 1)], ybuf.at[s, pl.ds(k * tm + r, 1)],
                              gsem.at[s]).start(priority=k)

    def wait_in(s):
        pltpu.make_async_copy(y_hbm.at[pl.ds(0, 2 * tm)], ybuf.at[s], gsem.at[s]).wait()

    @pl.when(i == 0)
    def _():
        for t in range(ROW_BUFS - 1):
            tile = jnp.minimum(t, n_steps - 1)

            def row(r, carry):
                row_in(tile, r, 0, t)
                row_in(tile, r, 1, t)
                return carry
            lax.fori_loop(0, tm, row, 0, unroll=8)

    def step(s):
        wait_in(s)
        rt = rt_ref[...]
        y = rt[:, 2:3] * ybuf[s, :tm, :] + rt[:, 3:4] * ybuf[s, tm:, :]
        nxt = jnp.minimum(i + ROW_BUFS - 1, n_steps - 1)
        for r in range(tm):
            row_in(nxt, r, 0, (s + ROW_BUFS - 1) % ROW_BUFS)
            row_in(nxt, r, 1, (s + ROW_BUFS - 1) % ROW_BUFS)
        x2 = x1_ref[...] + mod_ref[0, 5:6, :] * y
        o_ref[...] = _rms(x2, g_ref[...]) * (1.0 + modf_ref[0, 1:2, :]) + modf_ref[0, 0:1, :]

        @pl.when(i == n_steps - 1)
        def _():
            for t in range(1, ROW_BUFS):
                wait_in((s + t) % ROW_BUFS)

    for s in range(ROW_BUFS):
        pl.when(i % ROW_BUFS == s)(functools.partial(step, s))


def _final(x1, y_rows, slot, rt, mod, modf, g_final):
    b, s, d = x1.shape
    tm = FINAL_TM
    nt = s // tm
    grid_spec = pltpu.PrefetchScalarGridSpec(
        num_scalar_prefetch=1,
        grid=(b * nt,),
        in_specs=[
            pl.BlockSpec((tm, d), lambda i, *_: (i, 0)),
            pl.BlockSpec((tm, LANES), lambda i, *_: (i, 0)),
            pl.BlockSpec((1, 6, d), lambda i, *_: (i // nt, 0, 0)),
            pl.BlockSpec((1, 2, d), lambda i, *_: (i // nt, 0, 0)),
            pl.BlockSpec((1, d), lambda i, *_: (0, 0)),
            pl.BlockSpec(memory_space=pl.ANY),
        ],
        out_specs=pl.BlockSpec((tm, d), lambda i, *_: (i, 0)),
        scratch_shapes=[
            pltpu.VMEM((ROW_BUFS, 2 * tm, d), F32),
            pltpu.SemaphoreType.DMA((ROW_BUFS,)),
        ],
    )
    out = pl.pallas_call(
        _final_kernel,
        out_shape=jax.ShapeDtypeStruct((b * s, d), F32),
        grid_spec=grid_spec,
        compiler_params=pltpu.CompilerParams(
            dimension_semantics=("arbitrary",), vmem_limit_bytes=VMEM_LIMIT),
        name="final",
    )(slot, x1.reshape(b * s, d), rt, mod, modf, g_final.reshape(1, d), y_rows)
    return out.reshape(b, s, d)


def kernel(x, c, w_ada, b_ada, g_mix, w_in, sinks_a, rel_bias_b, g_out_a, g_out_b, w_out, g_ffn,
           w_router_group, b_router_group, w_router_expert, b_router_expert, w_gate, w_up, w_down,
           w_ada_final, b_ada_final, g_final):
    b, s, d = x.shape
    assert w_ada.shape[0] == 1, "one layer"
    n_tok = b * s

    c_act = jax.nn.silu(c)
    a_rep = jnp.broadcast_to(c_act[:, :, None], (b, d, LANES))
    mod = _ada(a_rep, w_ada[0], b_ada[0]).reshape(b, 6, d)
    modf = _ada(a_rep, w_ada_final, b_ada_final).reshape(b, 2, d)

    proj = _proj(x, mod, g_mix[0], w_in[0].astype(BF16))

    bias_a, nw_a, bias_b, nw_b = _band_tables(rel_bias_b[0])
    kv_a0 = DA_Q // LANES
    o_a = _attention(proj, bias_a, sinks_a[0].astype(F32), n_groups=N_KV_A, ncol=2, nw=nw_a,
                     q_col0=0, k_col0=kv_a0, v_col0=kv_a0 + DA_KV // LANES, kv_share=2, dup=True)
    qb0 = (DA_Q + 2 * DA_KV) // LANES
    o_b = _attention(proj, bias_b, None, n_groups=N_HEADS_B // 2, ncol=1, nw=nw_b,
                     q_col0=qb0, k_col0=qb0 + DB // LANES, v_col0=qb0 + 2 * DB // LANES,
                     kv_share=1, dup=False)

    n_r = N_GROUPS + N_EXPERTS
    assert n_r <= ROUTER_LO_LANE
    w_r = jnp.concatenate([w_router_group[0], w_router_expert[0]], axis=1)
    w_r_hi = w_r.astype(BF16)
    w_r_lo = (w_r - w_r_hi.astype(F32)).astype(BF16)
    w_router = (jnp.zeros((d, LANES), BF16).at[:, :n_r].set(w_r_hi)
                .at[:, ROUTER_LO_LANE:ROUTER_LO_LANE + n_r].set(w_r_lo))
    b_router = jnp.zeros((LANES,), F32).at[:n_r].set(
        jnp.concatenate([b_router_group[0], b_router_expert[0]]))
    x1, h2, rt, ids = _mix(o_a, o_b, x, mod, g_out_a[0], g_out_b[0], g_ffn[0], w_out[0].astype(BF16),
                      w_router, b_router)

    tile_expert, next_expert, n_tiles, zero_fill, slot, n_pad = _route_plan(ids, n_tok)
    xs = _dispatch(h2.reshape(n_tok, d), slot, zero_fill, n_pad)
    y = _moe(xs, w_gate[0], w_up[0], w_down[0], tile_expert, next_expert, n_tiles)

    return _final(x1, y, slot, rt, mod, modf, g_final)
```

```python
import functools

import jax
import jax.numpy as jnp
import numpy as np
from jax import lax
from jax.experimental import pallas as pl
from jax.experimental.pallas import tpu as pltpu

D_MODEL = 2048
CHUNK = 64
HEAD_DIM = 64
N_HEADS_A = 16
N_KV_A = 4
N_PREV_A = 2
N_HEADS_B = 16
N_PREV_B = 8
REL_CLIP = 128
DA_Q = N_HEADS_A * HEAD_DIM
DA_KV = N_KV_A * HEAD_DIM
DB = N_HEADS_B * HEAD_DIM
N_GROUPS = 4
EXPERTS_PER_GROUP = 8
N_EXPERTS = N_GROUPS * EXPERTS_PER_GROUP
EPS = 1e-6
NEG_INF = -1e30
LOG2E = 1.4426950408889634

LANES = 128
SUBLANES = 8
VMEM_LIMIT = 56 * 1024 * 1024

ADA_TN = 2048
ADA_JB = 8
PROJ_TM = 512
PROJ_N_CHUNK = 512
QBLK = 2 * CHUNK
KV_PREP_ROWS = 512
MIX_TM = 512
MOE_TM = 512
DISPATCH_TM = 512
FINAL_TM = 256
ROW_BUFS = 3
ROUTER_LO_LANE = 64

F32 = jnp.float32
BF16 = jnp.bfloat16


def _rms(x, g):
    return x * lax.rsqrt(jnp.mean(x * x, axis=-1, keepdims=True) + EPS) * g


def _ada_kernel(a_ref, w_ref, b_ref, o_ref):
    n_b, k, _ = a_ref.shape
    tn = w_ref.shape[1]
    sub = SUBLANES
    for jb in range(tn // (ADA_JB * LANES)):
        col0 = jb * ADA_JB * LANES

        def body(kc, accs):
            k0 = pl.multiple_of(kc * sub, sub)
            a_rows = [a_ref[b, pl.ds(k0, sub), :] for b in range(n_b)]
            out = []
            for j in range(ADA_JB):
                w = w_ref[pl.ds(k0, sub), col0 + j * LANES:col0 + (j + 1) * LANES]
                out.append([accs[j][b] + a_rows[b] * w for b in range(n_b)])
            return out

        zero = jnp.zeros((sub, LANES), F32)
        accs = lax.fori_loop(0, k // sub, body, [[zero] * n_b for _ in range(ADA_JB)], unroll=4)
        for j in range(ADA_JB):
            cols = slice(col0 + j * LANES, col0 + (j + 1) * LANES)
            for b in range(n_b):
                o_ref[b:b + 1, cols] = jnp.sum(accs[j][b], axis=0, keepdims=True) + b_ref[:, cols]


def _ada(a_rep, w, bias):
    n_b, k, _ = a_rep.shape
    n = w.shape[1]
    tn = ADA_TN
    return pl.pallas_call(
        _ada_kernel,
        out_shape=jax.ShapeDtypeStruct((n_b, n), F32),
        grid=(n // tn,),
        in_specs=[
            pl.BlockSpec((n_b, k, LANES), lambda j: (0, 0, 0)),
            pl.BlockSpec((k, tn), lambda j: (0, j)),
            pl.BlockSpec((1, tn), lambda j: (0, j)),
        ],
        out_specs=pl.BlockSpec((n_b, tn), lambda j: (0, j)),
        compiler_params=pltpu.CompilerParams(
            dimension_semantics=("arbitrary",), vmem_limit_bytes=VMEM_LIMIT),
        name="ada",
    )(a_rep, w, bias.reshape(1, n))


def _proj_kernel(x_ref, mod_ref, g_ref, w_ref, o_ref, *, n_chunk):
    h = _rms(x_ref[0], g_ref[...])
    h = h * (1.0 + mod_ref[0, 1:2, :]) + mod_ref[0, 0:1, :]
    hb = h.astype(BF16)
    for n0 in range(0, o_ref.shape[2], n_chunk):
        cols = slice(n0, n0 + n_chunk)
        o_ref[0, :, cols] = jnp.dot(hb, w_ref[:, cols], preferred_element_type=F32).astype(BF16)


def _proj(x, mod, g_mix, w_in_bf16):
    b, s, d = x.shape
    n = w_in_bf16.shape[1]
    tm = PROJ_TM
    return pl.pallas_call(
        functools.partial(_proj_kernel, n_chunk=PROJ_N_CHUNK),
        out_shape=jax.ShapeDtypeStruct((b, s, n), BF16),
        grid=(b, s // tm),
        in_specs=[
            pl.BlockSpec((1, tm, d), lambda bi, i: (bi, i, 0)),
            pl.BlockSpec((1, 6, d), lambda bi, i: (bi, 0, 0)),
            pl.BlockSpec((1, d), lambda bi, i: (0, 0)),
            pl.BlockSpec((d, n), lambda bi, i: (0, 0), pipeline_mode=pl.Buffered(1)),
        ],
        out_specs=pl.BlockSpec((1, tm, n), lambda bi, i: (bi, i, 0)),
        compiler_params=pltpu.CompilerParams(
            dimension_semantics=("arbitrary", "arbitrary"), vmem_limit_bytes=VMEM_LIMIT),
        name="proj",
    )(x, mod, g_mix.reshape(1, d), w_in_bf16)


def _attn_kernel(*refs, ncol, nw, dup, has_sink):
    if has_sink:
        sink_ref, q_ref, k_ref, v_ref, bias_ref, o_ref = refs[:6]
        scratch = refs[6:]
    else:
        q_ref, k_ref, v_ref, bias_ref, o_ref = refs[:5]
        scratch = refs[5:]
        sink_ref = None
    s_len = q_ref.shape[1]
    n_blk = s_len // QBLK
    n_stack = 2 * ncol
    grp = pl.program_id(1)

    rows = KV_PREP_ROWS
    lane = lax.broadcasted_iota(jnp.int32, (QBLK, LANES), 1)
    low = lane < HEAD_DIM
    if dup:
        kd_ref, vt_ref = scratch[:2]
        half = grp % 2
        keep = (lax.broadcasted_iota(jnp.int32, (rows, LANES), 1) // HEAD_DIM) == half

        def spread(t):
            return jnp.where(keep, t, pltpu.roll(t, HEAD_DIM, 1))

        def dup_body(c, carry):
            r0 = pl.multiple_of(c * rows, rows)
            kd_ref[pl.ds(r0, rows), :] = spread(k_ref[0, pl.ds(r0, rows), :].astype(F32)).astype(BF16)
            return carry

        lax.fori_loop(0, s_len // rows, dup_body, 0)
        k_src = kd_ref
    else:
        vt_ref = scratch[0]
        k_src = k_ref.at[0]

        def spread(t):
            return t

    for c in range(s_len // rows):
        t = spread(v_ref[0, c * rows:(c + 1) * rows, :].astype(F32))
        vt_ref[:, c * rows:(c + 1) * rows] = t.T.astype(BF16)

    top = lax.broadcasted_iota(jnp.int32, (LANES, QBLK), 0) < HEAD_DIM

    if has_sink:
        sink = jnp.concatenate(
            [jnp.full((1, QBLK), sink_ref[grp * n_stack + h] * LOG2E, F32) for h in range(n_stack)],
            axis=1)

    def rows_of(j):
        return j * QBLK if isinstance(j, int) else pl.multiple_of(j * QBLK, QBLK)

    def logits(j, nvb):
        qf = q_ref[0, pl.ds(rows_of(j), QBLK), :].astype(F32) * (HEAD_DIM ** -0.5 * LOG2E)
        parts = []
        for c in range(ncol):
            qc = qf[:, c * LANES:(c + 1) * LANES]
            parts.append(jnp.where(low, qc, 0.0))
            parts.append(jnp.where(low, 0.0, qc))
        lhs = jnp.concatenate(parts, axis=0).astype(BF16)
        kw = k_src[pl.ds(rows_of(j - (nvb - 1)), nvb * QBLK), :]
        s = lax.dot_general(kw, lhs, (((1,), (1,)), ((), ())), preferred_element_type=F32)
        return s + bias_ref[0, (nw - nvb) * QBLK:, :]

    def softmax(s):
        m = jnp.max(s, axis=0, keepdims=True)
        if has_sink:
            m = jnp.maximum(m, sink)
        p = jnp.exp2(s - m)
        denom = jnp.sum(p, axis=0, keepdims=True)
        if has_sink:
            denom = denom + jnp.exp2(sink - m)
        return p.astype(BF16), denom

    def emit(j, nvb, p, denom):
        vw = vt_ref[:, pl.ds(rows_of(j - (nvb - 1)), nvb * QBLK)]
        o = jnp.dot(vw, p, preferred_element_type=F32) / denom
        for c in range(ncol):
            o0 = o[:, (2 * c) * QBLK:(2 * c + 1) * QBLK]
            o1 = o[:, (2 * c + 1) * QBLK:(2 * c + 2) * QBLK]
            o_ref[0, pl.ds(rows_of(j), QBLK), c * LANES:(c + 1) * LANES] = (
                jnp.where(top, o0, o1).T.astype(BF16))

    s_scr, p_scr = scratch[-2:]
    last = n_blk - 1

    def n_valid(j):
        return min(j + 1, nw) if isinstance(j, int) else nw

    def put_logits(u, j):
        s_scr[u, :n_valid(j) * QBLK, :] = logits(j, n_valid(j))

    def put_probs(u, j):
        rows_n = n_valid(j) * QBLK
        p, den = softmax(s_scr[u, :rows_n, :])
        p_scr[u, :rows_n, :] = p
        return den

    def pair(j, d_even, clamp):
        emit(j, n_valid(j), p_scr[0, :n_valid(j) * QBLK, :], d_even)
        d_odd = put_probs(1, j + 1)
        put_logits(0, clamp(j + 2))
        emit(j + 1, n_valid(j + 1), p_scr[1, :n_valid(j + 1) * QBLK, :], d_odd)
        d_next = put_probs(0, clamp(j + 2))
        put_logits(1, clamp(j + 3))
        return d_next

    first = nw - 1 + (nw - 1) % 2
    assert n_blk % 2 == 0 and first + 4 <= n_blk
    put_logits(0, 0)
    den = put_probs(0, 0)
    put_logits(1, 1)
    for j in range(0, first, 2):
        den = pair(j, den, lambda k: k)

    lax.fori_loop(0, (n_blk - first) // 2,
                  lambda t, d: pair(first + 2 * t, d, lambda k: jnp.minimum(k, last)), den)


def _attention(proj, bias, sinks, *, n_groups, ncol, nw, q_col0, k_col0, v_col0, kv_share, dup):
    b, s, _ = proj.shape
    qw = ncol * LANES
    has_sink = sinks is not None
    kernel = functools.partial(_attn_kernel, ncol=ncol, nw=nw, dup=dup, has_sink=has_sink)
    _, n_keys, n_q = bias.shape
    in_specs = [
        pl.BlockSpec((1, s, qw), lambda bi, g, *_: (bi, 0, q_col0 // ncol + g)),
        pl.BlockSpec((1, s, LANES), lambda bi, g, *_: (bi, 0, k_col0 + g // kv_share)),
        pl.BlockSpec((1, s, LANES), lambda bi, g, *_: (bi, 0, v_col0 + g // kv_share)),
        pl.BlockSpec((1, n_keys, n_q), lambda bi, g, *_: (g, 0, 0)),
    ]
    out_spec = pl.BlockSpec((1, s, qw), lambda bi, g, *_: (bi, 0, g))
    scratch = [pltpu.VMEM((s, LANES), BF16)] if dup else []
    scratch += [pltpu.VMEM((LANES, s), BF16),
                pltpu.VMEM((2, n_keys, n_q), F32), pltpu.VMEM((2, n_keys, n_q), BF16)]
    grid_spec = pltpu.PrefetchScalarGridSpec(
        num_scalar_prefetch=1 if has_sink else 0,
        grid=(b, n_groups),
        in_specs=in_specs,
        out_specs=out_spec,
        scratch_shapes=scratch,
    )
    args = ((sinks,) if has_sink else ()) + (proj, proj, proj, bias)
    return pl.pallas_call(
        kernel,
        out_shape=jax.ShapeDtypeStruct((b, s, n_groups * qw), BF16),
        grid_spec=grid_spec,
        compiler_params=pltpu.CompilerParams(
            dimension_semantics=("arbitrary", "arbitrary"), vmem_limit_bytes=VMEM_LIMIT),
        name="attn_a" if dup else "attn_b",
    )(*args)


def _band_tables(rel_bias_b):
    qi = jnp.arange(QBLK)[:, None]
    r = qi // CHUNK

    def table(nw, n_prev, fn):
        kj = jnp.arange(nw * QBLK)[None, :]
        rel = (nw - 1) * QBLK + qi - kj
        inband = (kj >= r * CHUNK) & (kj < (r + n_prev + 1) * CHUNK)
        return jnp.where(inband[None], fn(rel), NEG_INF)

    slopes = jnp.exp2(-8.0 * jnp.arange(1, N_HEADS_A + 1, dtype=F32) / N_HEADS_A)
    nw_a = (N_PREV_A * CHUNK) // QBLK + 1
    nw_b = (N_PREV_B * CHUNK) // QBLK + 1
    bias_a = table(nw_a, N_PREV_A, lambda rel: -slopes[:, None, None] * jnp.abs(rel).astype(F32)[None])
    w_b = nw_b * QBLK
    row = -(-(QBLK + w_b - 2) // LANES) * LANES
    p = row + 1
    t = np.arange(p)
    m = np.where(t < w_b, t, t - p)
    dist = np.clip((nw_b - 1) * QBLK - m, -REL_CLIP, REL_CLIP) + REL_CLIP
    vec = rel_bias_b[:, dist].astype(F32)
    reps = -(-(QBLK * row) // p)
    rel_b = jnp.tile(vec, (1, reps))[:, :QBLK * row].reshape(N_HEADS_B, QBLK, row)[:, :, :w_b]
    bias_b = table(nw_b, N_PREV_B, lambda rel: rel_b)
    rep = N_HEADS_A // N_KV_A
    bias_a = bias_a.reshape(N_KV_A, rep * QBLK, nw_a * QBLK).transpose(0, 2, 1)
    bias_b = bias_b.reshape(N_HEADS_B // 2, 2 * QBLK, nw_b * QBLK).transpose(0, 2, 1)
    return bias_a * LOG2E, nw_a, bias_b * LOG2E, nw_b


def _mix_kernel(oa_ref, ob_ref, x_ref, mod_ref, ga_ref, gb_ref, gf_ref, wo_ref, wr_ref, br_ref,
                x1_ref, h2_ref, rt_ref, ids_ref):
    na = _rms(oa_ref[0].astype(F32), ga_ref[...]).astype(BF16)
    nb = _rms(ob_ref[0].astype(F32), gb_ref[...]).astype(BF16)
    acc = jnp.dot(jnp.concatenate([na, nb], axis=1), wo_ref[...], preferred_element_type=F32)
    x1 = x_ref[0] + mod_ref[0, 2:3, :] * acc
    x1_ref[0] = x1
    h2 = _rms(x1, gf_ref[...]) * (1.0 + mod_ref[0, 4:5, :]) + mod_ref[0, 3:4, :]
    h2_ref[0] = h2

    tm = h2.shape[0]
    h_hi = h2.astype(BF16)
    h_lo = (h2 - h_hi.astype(F32)).astype(BF16)
    r = jnp.dot(jnp.concatenate([h_hi, h_lo], axis=0), wr_ref[...], preferred_element_type=F32)
    r = r[:tm] + r[tm:]
    logits = r + pltpu.roll(r, LANES - ROUTER_LO_LANE, 1) + br_ref[...]

    lane = lax.broadcasted_iota(jnp.int32, (tm, LANES), 1)
    lane_f = lane.astype(F32)
    big = float(LANES)
    ninf = -jnp.inf

    def first_max(vals):
        top = jnp.max(vals, axis=-1, keepdims=True)
        idx = jnp.min(jnp.where(vals == top, lane_f, big), axis=-1, keepdims=True)
        return top, idx

    is_g = lane < N_GROUPS
    g_top, g_idx = first_max(jnp.where(is_g, logits, ninf))
    p_g = 1.0 / jnp.sum(jnp.where(is_g, jnp.exp(logits - g_top), 0.0), axis=-1, keepdims=True)
    lo = N_GROUPS + g_idx * EXPERTS_PER_GROUP
    e_vals = jnp.where((lane_f >= lo) & (lane_f < lo + EXPERTS_PER_GROUP), logits, ninf)
    v1, i1 = first_max(e_vals)
    v2, i2 = first_max(jnp.where(lane_f == i1, ninf, e_vals))
    e2 = jnp.exp(v2 - v1)
    w1 = p_g / (1.0 + e2)
    w2 = p_g * e2 / (1.0 + e2)
    rt = jnp.where(lane == 0, i1 - N_GROUPS,
                   jnp.where(lane == 1, i2 - N_GROUPS,
                             jnp.where(lane == 2, w1, jnp.where(lane == 3, w2, 0.0))))
    rt_ref[...] = rt
    pick = (lax.broadcasted_iota(jnp.int32, (SUBLANES, LANES), 0)
            == lax.broadcasted_iota(jnp.int32, (SUBLANES, LANES), 1)).astype(BF16)
    ids_ref[...] = lax.dot_general(pick, rt.astype(BF16), (((1,), (1,)), ((), ())),
                                   preferred_element_type=F32)


def _mix(o_a, o_b, x, mod, g_out_a, g_out_b, g_ffn, w_out_bf16, w_router, b_router):
    b, s, d = x.shape
    half = o_a.shape[2]
    tm = MIX_TM
    nt = s // tm
    vec = lambda n: pl.BlockSpec((1, n), lambda bi, i: (0, 0))
    return pl.pallas_call(
        _mix_kernel,
        out_shape=(jax.ShapeDtypeStruct((b, s, d), F32),
                   jax.ShapeDtypeStruct((b, s, d), F32),
                   jax.ShapeDtypeStruct((b * s, LANES), F32),
                   jax.ShapeDtypeStruct((SUBLANES, b * s), F32)),
        grid=(b, nt),
        in_specs=[
            pl.BlockSpec((1, tm, half), lambda bi, i: (bi, i, 0)),
            pl.BlockSpec((1, tm, half), lambda bi, i: (bi, i, 0)),
            pl.BlockSpec((1, tm, d), lambda bi, i: (bi, i, 0)),
            pl.BlockSpec((1, 6, d), lambda bi, i: (bi, 0, 0)),
            vec(half), vec(half), vec(d),
            pl.BlockSpec((d, d), lambda bi, i: (0, 0), pipeline_mode=pl.Buffered(1)),
            pl.BlockSpec((d, LANES), lambda bi, i: (0, 0)),
            vec(LANES),
        ],
        out_specs=(pl.BlockSpec((1, tm, d), lambda bi, i: (bi, i, 0)),
                   pl.BlockSpec((1, tm, d), lambda bi, i: (bi, i, 0)),
                   pl.BlockSpec((tm, LANES), lambda bi, i: (bi * nt + i, 0)),
                   pl.BlockSpec((SUBLANES, tm), lambda bi, i: (0, bi * nt + i))),
        compiler_params=pltpu.CompilerParams(
            dimension_semantics=("arbitrary", "arbitrary"), vmem_limit_bytes=VMEM_LIMIT),
        name="mix_out",
    )(o_a, o_b, x, mod, g_out_a.reshape(1, half), g_out_b.reshape(1, half), g_ffn.reshape(1, d),
      w_out_bf16, w_router, b_router.reshape(1, LANES))


DISPATCH_STEPS_MIN = ROW_BUFS


def _dispatch_kernel(slot_ref, zf_ref, h_ref, xs_hbm, buf, zbuf, sem, zsem):
    i = pl.program_id(0)
    n_steps = pl.num_programs(0)
    td = h_ref.shape[0]
    tm = zbuf.shape[0]
    n_tiles_max = xs_hbm.shape[0] // tm

    @pl.when(i == 0)
    def _():
        zbuf[...] = jnp.zeros(zbuf.shape, zbuf.dtype)

        def fill_copy(t):
            z0 = pl.multiple_of(zf_ref[t], SUBLANES)
            n = pl.multiple_of(tm - z0, SUBLANES)
            row0 = pl.multiple_of(t * tm + z0, SUBLANES)
            return pltpu.make_async_copy(zbuf.at[pl.ds(0, n)], xs_hbm.at[pl.ds(row0, n)], zsem.at[0])

        def fill(t, carry):
            @pl.when(zf_ref[t] < tm)
            def _():
                fill_copy(t).start()
            return carry
        lax.fori_loop(0, n_tiles_max, fill, 0)

        def drain(t, carry):
            @pl.when(zf_ref[t] < tm)
            def _():
                fill_copy(t).wait()
            return carry
        lax.fori_loop(0, n_tiles_max, drain, 0)

    def wait_rows(s):
        for _ in range(2):
            pltpu.make_async_copy(buf.at[s], xs_hbm.at[pl.ds(0, td)], sem.at[s]).wait()

    def step(s):
        @pl.when(i >= ROW_BUFS)
        def _():
            wait_rows(s)

        buf[s] = h_ref[...]
        for r in range(td):
            for k in range(2):
                dst = slot_ref[k * (n_steps * td) + i * td + r]
                pltpu.make_async_copy(buf.at[s, pl.ds(r, 1)], xs_hbm.at[pl.ds(dst, 1)],
                                      sem.at[s]).start(priority=k)

        @pl.when(i == n_steps - 1)
        def _():
            for t in range(min(ROW_BUFS, DISPATCH_STEPS_MIN)):
                wait_rows((s + ROW_BUFS - t) % ROW_BUFS)

    for s in range(ROW_BUFS):
        pl.when(i % ROW_BUFS == s)(functools.partial(step, s))


def _dispatch(h2, slot, zero_fill, n_pad):
    t, d = h2.shape
    tm = MOE_TM
    assert t // DISPATCH_TM >= DISPATCH_STEPS_MIN
    grid_spec = pltpu.PrefetchScalarGridSpec(
        num_scalar_prefetch=2,
        grid=(t // DISPATCH_TM,),
        in_specs=[pl.BlockSpec((DISPATCH_TM, d), lambda i, *_: (i, 0))],
        out_specs=pl.BlockSpec(memory_space=pl.ANY),
        scratch_shapes=[
            pltpu.VMEM((ROW_BUFS, DISPATCH_TM, d), F32),
            pltpu.VMEM((tm, d), F32),
            pltpu.SemaphoreType.DMA((ROW_BUFS,)),
            pltpu.SemaphoreType.DMA((1,)),
        ],
    )
    return pl.pallas_call(
        _dispatch_kernel,
        out_shape=jax.ShapeDtypeStruct((n_pad, d), F32),
        grid_spec=grid_spec,
        compiler_params=pltpu.CompilerParams(
            dimension_semantics=("arbitrary",), vmem_limit_bytes=VMEM_LIMIT),
        name="dispatch",
    )(slot, zero_fill, h2)


def _moe_kernel(te_ref, nxe_ref, nt_ref, x_ref, wg_hbm, wu_hbm, wd_hbm, o_ref,
                wg32, wu32, wd32, wgb, wub, wdb, wsem):
    i = pl.program_id(0)
    n_tiles = nt_ref[0]

    def weight_copies(e):
        return [pltpu.make_async_copy(src.at[e], dst, wsem.at[0])
                for src, dst in ((wg_hbm, wg32), (wu_hbm, wu32), (wd_hbm, wd32))]

    @pl.when(i == 0)
    def _():
        for cp in weight_copies(te_ref[0]):
            cp.start()

    @pl.when(i < n_tiles)
    def _():
        @pl.when(jnp.logical_or(i == 0, te_ref[i] != te_ref[jnp.maximum(i - 1, 0)]))
        def _():
            for cp in weight_copies(0):
                cp.wait()
            wgb[...] = wg32[...].astype(BF16)
            wub[...] = wu32[...].astype(BF16)
            wdb[...] = wd32[...].astype(BF16)

            @pl.when(nxe_ref[i] >= 0)
            def _():
                for cp in weight_copies(nxe_ref[i]):
                    cp.start()

        xb = x_ref[...].astype(BF16)
        g = jnp.dot(xb, wgb[...], preferred_element_type=F32)
        u = jnp.dot(xb, wub[...], preferred_element_type=F32)
        a = (g * jax.nn.sigmoid(g) * u).astype(BF16)
        o_ref[...] = jnp.dot(a, wdb[...], preferred_element_type=F32)

    @pl.when(i >= n_tiles)
    def _():
        o_ref[...] = jnp.zeros(o_ref.shape, o_ref.dtype)


def _moe(xs, w_gate, w_up, w_down, tile_expert, next_expert, n_tiles):
    n_pad, d = xs.shape
    tm = MOE_TM
    max_tiles = n_pad // tm
    de = w_gate.shape[2]
    grid_spec = pltpu.PrefetchScalarGridSpec(
        num_scalar_prefetch=3,
        grid=(max_tiles,),
        in_specs=[pl.BlockSpec((tm, d), lambda i, te, nxe, nt: (jnp.minimum(i, nt[0] - 1), 0))]
        + [pl.BlockSpec(memory_space=pl.ANY)] * 3,
        out_specs=pl.BlockSpec((tm, d), lambda i, *_: (i, 0)),
        scratch_shapes=[
            pltpu.VMEM((d, de), F32),
            pltpu.VMEM((d, de), F32),
            pltpu.VMEM((de, d), F32),
            pltpu.VMEM((d, de), BF16),
            pltpu.VMEM((d, de), BF16),
            pltpu.VMEM((de, d), BF16),
            pltpu.SemaphoreType.DMA((1,)),
        ],
    )
    return pl.pallas_call(
        _moe_kernel,
        out_shape=jax.ShapeDtypeStruct((n_pad, d), F32),
        grid_spec=grid_spec,
        compiler_params=pltpu.CompilerParams(
            dimension_semantics=("arbitrary",), vmem_limit_bytes=VMEM_LIMIT),
        name="moe",
    )(tile_expert, next_expert, n_tiles, xs, w_gate, w_up, w_down)


def _route_plan(ids, n_tok):
    tm = MOE_TM
    n_asg = 2 * n_tok
    n_pad = n_asg + N_EXPERTS * tm
    max_tiles = n_pad // tm
    e_flat = ids[:2].astype(jnp.int32).reshape(-1)
    onehot = e_flat[None, :] == jnp.arange(N_EXPERTS)[:, None]
    blk = LANES
    oh3 = onehot.reshape(N_EXPERTS, n_asg // blk, blk).astype(BF16)
    upper = (jnp.arange(blk)[:, None] <= jnp.arange(blk)[None, :]).astype(BF16)
    within = jnp.einsum("ebj,jk->ebk", oh3, upper, preferred_element_type=F32).astype(jnp.int32)
    blk_tot = within[:, :, -1]
    offs = jnp.cumsum(blk_tot, axis=1) - blk_tot
    csum = (within + offs[:, :, None]).reshape(N_EXPERTS, n_asg)
    counts = jnp.sum(blk_tot, axis=1)
    oh_i = onehot.astype(jnp.int32)
    rank = jnp.sum(csum * oh_i, axis=0) - 1
    tiles_e = (counts + tm - 1) // tm
    tile_end = jnp.cumsum(tiles_e)
    base = (tile_end - tiles_e) * tm
    slot = jnp.sum(oh_i * base[:, None], axis=0) + rank
    n_tiles = tile_end[-1]
    tile_id = jnp.arange(max_tiles)
    te = jnp.sum((tile_id[:, None] >= tile_end[None, :]).astype(jnp.int32), axis=1)
    te_last = jnp.sum(((n_tiles - 1) >= tile_end).astype(jnp.int32))
    tile_expert = jnp.where(tile_id < n_tiles, te, te_last).astype(jnp.int32)
    ex = jnp.arange(N_EXPERTS)
    later = (ex[None, :] > ex[:, None]) & (counts[None, :] > 0)
    nxt_e = jnp.min(jnp.where(later, ex[None, :], N_EXPERTS), axis=1)
    nxt_e = jnp.where(nxt_e < N_EXPERTS, nxt_e, -1)
    sel = (tile_expert[:, None] == ex[None, :]).astype(jnp.int32)
    next_expert = jnp.sum(sel * nxt_e[None, :], axis=1).astype(jnp.int32)
    last_tile = jnp.sum(sel * (tile_end - 1)[None, :], axis=1)
    used_last = jnp.sum(sel * (counts - (tiles_e - 1) * tm)[None, :], axis=1)
    zero_fill = jnp.where(tile_id >= n_tiles, 0,
                          jnp.where(tile_id == last_tile, (used_last // SUBLANES) * SUBLANES,
                                    tm)).astype(jnp.int32)
    return (tile_expert, next_expert, n_tiles.reshape(1).astype(jnp.int32), zero_fill,
            slot.astype(jnp.int32), n_pad)


def _final_kernel(slot_ref, x1_ref, rt_ref, mod_ref, modf_ref, g_ref, y_hbm, o_ref, ybuf, gsem):
    i = pl.program_id(0)
    n_steps = pl.num_programs(0)
    tm = x1_ref.shape[0]

    def row_in(tile, r, k, s):
        idx = slot_ref[k * (n_steps * tm) + tile * tm + r]
        pltpu.make_async_copy(y_hbm.at[pl.ds(idx, 1)], ybuf.at[s, pl.ds(k * tm + r, 1)],
                              gsem.at[s]).start(priority=k)

    def wait_in(s):
        pltpu.make_async_copy(y_hbm.at[pl.ds(0, 2 * tm)], ybuf.at[s], gsem.at[s]).wait()

    @pl.when(i == 0)
    def _():
        for t in range(ROW_BUFS - 1):
            tile = jnp.minimum(t, n_steps - 1)

            def row(r, carry):
                row_in(tile, r, 0, t)
                row_in(tile, r, 1, t)
                return carry
            lax.fori_loop(0, tm, row, 0, unroll=8)

    def step(s):
        wait_in(s)
        rt = rt_ref[...]
        y = rt[:, 2:3] * ybuf[s, :tm, :] + rt[:, 3:4] * ybuf[s, tm:, :]
        nxt = jnp.minimum(i + ROW_BUFS - 1, n_steps - 1)
        for r in range(tm):
            row_in(nxt, r, 0, (s + ROW_BUFS - 1) % ROW_BUFS)
            row_in(nxt, r, 1, (s + ROW_BUFS - 1) % ROW_BUFS)
        x2 = x1_ref[...] + mod_ref[0, 5:6, :] * y
        o_ref[...] = _rms(x2, g_ref[...]) * (1.0 + modf_ref[0, 1:2, :]) + modf_ref[0, 0:1, :]

        @pl.when(i == n_steps - 1)
        def _():
            for t in range(1, ROW_BUFS):
                wait_in((s + t) % ROW_BUFS)

    for s in range(ROW_BUFS):
        pl.when(i % ROW_BUFS == s)(functools.partial(step, s))


def _final(x1, y_rows, slot, rt, mod, modf, g_final):
    b, s, d = x1.shape
    tm = FINAL_TM
    nt = s // tm
    grid_spec = pltpu.PrefetchScalarGridSpec(
        num_scalar_prefetch=1,
        grid=(b * nt,),
        in_specs=[
            pl.BlockSpec((tm, d), lambda i, *_: (i, 0)),
            pl.BlockSpec((tm, LANES), lambda i, *_: (i, 0)),
            pl.BlockSpec((1, 6, d), lambda i, *_: (i // nt, 0, 0)),
            pl.BlockSpec((1, 2, d), lambda i, *_: (i // nt, 0, 0)),
            pl.BlockSpec((1, d), lambda i, *_: (0, 0)),
            pl.BlockSpec(memory_space=pl.ANY),
        ],
        out_specs=pl.BlockSpec((tm, d), lambda i, *_: (i, 0)),
        scratch_shapes=[
            pltpu.VMEM((ROW_BUFS, 2 * tm, d), F32),
            pltpu.SemaphoreType.DMA((ROW_BUFS,)),
        ],
    )
    out = pl.pallas_call(
        _final_kernel,
        out_shape=jax.ShapeDtypeStruct((b * s, d), F32),
        grid_spec=grid_spec,
        compiler_params=pltpu.CompilerParams(
            dimension_semantics=("arbitrary",), vmem_limit_bytes=VMEM_LIMIT),
        name="final",
    )(slot, x1.reshape(b * s, d), rt, mod, modf, g_final.reshape(1, d), y_rows)
    return out.reshape(b, s, d)


def kernel(x, c, w_ada, b_ada, g_mix, w_in, sinks_a, rel_bias_b, g_out_a, g_out_b, w_out, g_ffn,
           w_router_group, b_router_group, w_router_expert, b_router_expert, w_gate, w_up, w_down,
           w_ada_final, b_ada_final, g_final):
    b, s, d = x.shape
    assert w_ada.shape[0] == 1, "one layer"
    n_tok = b * s

    c_act = jax.nn.silu(c)
    a_rep = jnp.broadcast_to(c_act[:, :, None], (b, d, LANES))
    mod = _ada(a_rep, w_ada[0], b_ada[0]).reshape(b, 6, d)
    modf = _ada(a_rep, w_ada_final, b_ada_final).reshape(b, 2, d)

    proj = _proj(x, mod, g_mix[0], w_in[0].astype(BF16))

    bias_a, nw_a, bias_b, nw_b = _band_tables(rel_bias_b[0])
    kv_a0 = DA_Q // LANES
    o_a = _attention(proj, bias_a, sinks_a[0].astype(F32), n_groups=N_KV_A, ncol=2, nw=nw_a,
                     q_col0=0, k_col0=kv_a0, v_col0=kv_a0 + DA_KV // LANES, kv_share=2, dup=True)
    qb0 = (DA_Q + 2 * DA_KV) // LANES
    o_b = _attention(proj, bias_b, None, n_groups=N_HEADS_B // 2, ncol=1, nw=nw_b,
                     q_col0=qb0, k_col0=qb0 + DB // LANES, v_col0=qb0 + 2 * DB // LANES,
                     kv_share=1, dup=False)

    n_r = N_GROUPS + N_EXPERTS
    assert n_r <= ROUTER_LO_LANE
    w_r = jnp.concatenate([w_router_group[0], w_router_expert[0]], axis=1)
    w_r_hi = w_r.astype(BF16)
    w_r_lo = (w_r - w_r_hi.astype(F32)).astype(BF16)
    w_router = (jnp.zeros((d, LANES), BF16).at[:, :n_r].set(w_r_hi)
                .at[:, ROUTER_LO_LANE:ROUTER_LO_LANE + n_r].set(w_r_lo))
    b_router = jnp.zeros((LANES,), F32).at[:n_r].set(
        jnp.concatenate([b_router_group[0], b_router_expert[0]]))
    x1, h2, rt, ids = _mix(o_a, o_b, x, mod, g_out_a[0], g_out_b[0], g_ffn[0], w_out[0].astype(BF16),
                      w_router, b_router)

    tile_expert, next_expert, n_tiles, zero_fill, slot, n_pad = _route_plan(ids, n_tok)
    xs = _dispatch(h2.reshape(n_tok, d), slot, zero_fill, n_pad)
    y = _moe(xs, w_gate[0], w_up[0], w_down[0], tile_expert, next_expert, n_tiles)

    return _final(x1, y, slot, rt, mod, modf, g_final)
```

```python
import functools

import jax
import jax.numpy as jnp
import numpy as np
from jax import lax
from jax.experimental import pallas as pl
from jax.experimental.pallas import tpu as pltpu

D_MODEL = 2048
CHUNK = 64
HEAD_DIM = 64
N_HEADS_A = 16
N_KV_A = 4
N_PREV_A = 2
N_HEADS_B = 16
N_PREV_B = 8
REL_CLIP = 128
DA_Q = N_HEADS_A * HEAD_DIM
DA_KV = N_KV_A * HEAD_DIM
DB = N_HEADS_B * HEAD_DIM
N_GROUPS = 4
EXPERTS_PER_GROUP = 8
N_EXPERTS = N_GROUPS * EXPERTS_PER_GROUP
EPS = 1e-6
NEG_INF = -1e30
LOG2E = 1.4426950408889634

LANES = 128
SUBLANES = 8
VMEM_LIMIT = 56 * 1024 * 1024

ADA_TN = 2048
ADA_JB = 8
PROJ_TM = 512
PROJ_N_CHUNK = 512
QBLK = 2 * CHUNK
KV_PREP_ROWS = 512
MIX_TM = 512
MOE_TM = 512
DISPATCH_TM = 512
FINAL_TM = 256
ROW_BUFS = 3
ROUTER_LO_LANE = 64

F32 = jnp.float32
BF16 = jnp.bfloat16


def _rms(x, g):
    return x * lax.rsqrt(jnp.mean(x * x, axis=-1, keepdims=True) + EPS) * g


def _ada_kernel(a_ref, w_ref, b_ref, o_ref):
    n_b, k, _ = a_ref.shape
    tn = w_ref.shape[1]
    sub = SUBLANES
    for jb in range(tn // (ADA_JB * LANES)):
        col0 = jb * ADA_JB * LANES

        def body(kc, accs):
            k0 = pl.multiple_of(kc * sub, sub)
            a_rows = [a_ref[b, pl.ds(k0, sub), :] for b in range(n_b)]
            out = []
            for j in range(ADA_JB):
                w = w_ref[pl.ds(k0, sub), col0 + j * LANES:col0 + (j + 1) * LANES]
                out.append([accs[j][b] + a_rows[b] * w for b in range(n_b)])
            return out

        zero = jnp.zeros((sub, LANES), F32)
        accs = lax.fori_loop(0, k // sub, body, [[zero] * n_b for _ in range(ADA_JB)], unroll=4)
        for j in range(ADA_JB):
            cols = slice(col0 + j * LANES, col0 + (j + 1) * LANES)
            for b in range(n_b):
                o_ref[b:b + 1, cols] = jnp.sum(accs[j][b], axis=0, keepdims=True) + b_ref[:, cols]


def _ada(a_rep, w, bias):
    n_b, k, _ = a_rep.shape
    n = w.shape[1]
    tn = ADA_TN
    return pl.pallas_call(
        _ada_kernel,
        out_shape=jax.ShapeDtypeStruct((n_b, n), F32),
        grid=(n // tn,),
        in_specs=[
            pl.BlockSpec((n_b, k, LANES), lambda j: (0, 0, 0)),
            pl.BlockSpec((k, tn), lambda j: (0, j)),
            pl.BlockSpec((1, tn), lambda j: (0, j)),
        ],
        out_specs=pl.BlockSpec((n_b, tn), lambda j: (0, j)),
        compiler_params=pltpu.CompilerParams(
            dimension_semantics=("arbitrary",), vmem_limit_bytes=VMEM_LIMIT),
        name="ada",
    )(a_rep, w, bias.reshape(1, n))


def _proj_kernel(x_ref, mod_ref, g_ref, w_ref, o_ref, *, n_chunk):
    h = _rms(x_ref[0], g_ref[...])
    h = h * (1.0 + mod_ref[0, 1:2, :]) + mod_ref[0, 0:1, :]
    hb = h.astype(BF16)
    for n0 in range(0, o_ref.shape[2], n_chunk):
        cols = slice(n0, n0 + n_chunk)
        o_ref[0, :, cols] = jnp.dot(hb, w_ref[:, cols], preferred_element_type=F32).astype(BF16)


def _proj(x, mod, g_mix, w_in_bf16):
    b, s, d = x.shape
    n = w_in_bf16.shape[1]
    tm = PROJ_TM
    return pl.pallas_call(
        functools.partial(_proj_kernel, n_chunk=PROJ_N_CHUNK),
        out_shape=jax.ShapeDtypeStruct((b, s, n), BF16),
        grid=(b, s // tm),
        in_specs=[
            pl.BlockSpec((1, tm, d), lambda bi, i: (bi, i, 0)),
            pl.BlockSpec((1, 6, d), lambda bi, i: (bi, 0, 0)),
            pl.BlockSpec((1, d), lambda bi, i: (0, 0)),
            pl.BlockSpec((d, n), lambda bi, i: (0, 0), pipeline_mode=pl.Buffered(1)),
        ],
        out_specs=pl.BlockSpec((1, tm, n), lambda bi, i: (bi, i, 0)),
        compiler_params=pltpu.CompilerParams(
            dimension_semantics=("arbitrary", "arbitrary"), vmem_limit_bytes=VMEM_LIMIT),
        name="proj",
    )(x, mod, g_mix.reshape(1, d), w_in_bf16)


def _attn_kernel(*refs, ncol, nw, dup, has_sink):
    if has_sink:
        sink_ref, q_ref, k_ref, v_ref, bias_ref, o_ref = refs[:6]
        scratch = refs[6:]
    else:
        q_ref, k_ref, v_ref, bias_ref, o_ref = refs[:5]
        scratch = refs[5:]
        sink_ref = None
    s_len = q_ref.shape[1]
    n_blk = s_len // QBLK
    n_stack = 2 * ncol
    grp = pl.program_id(1)

    rows = KV_PREP_ROWS
    lane = lax.broadcasted_iota(jnp.int32, (QBLK, LANES), 1)
    low = lane < HEAD_DIM
    if dup:
        kd_ref, vt_ref = scratch[:2]
        half = grp % 2
        keep = (lax.broadcasted_iota(jnp.int32, (rows, LANES), 1) // HEAD_DIM) == half

        def spread(t):
            return jnp.where(keep, t, pltpu.roll(t, HEAD_DIM, 1))

        def dup_body(c, carry):
            r0 = pl.multiple_of(c * rows, rows)
            kd_ref[pl.ds(r0, rows), :] = spread(k_ref[0, pl.ds(r0, rows), :].astype(F32)).astype(BF16)
            return carry

        lax.fori_loop(0, s_len // rows, dup_body, 0)
        k_src = kd_ref
    else:
        vt_ref = scratch[0]
        k_src = k_ref.at[0]

        def spread(t):
            return t

    for c in range(s_len // rows):
        t = spread(v_ref[0, c * rows:(c + 1) * rows, :].astype(F32))
        vt_ref[:, c * rows:(c + 1) * rows] = t.T.astype(BF16)

    top = lax.broadcasted_iota(jnp.int32, (LANES, QBLK), 0) < HEAD_DIM

    if has_sink:
        sink = jnp.concatenate(
            [jnp.full((1, QBLK), sink_ref[grp * n_stack + h] * LOG2E, F32) for h in range(n_stack)],
            axis=1)

    def rows_of(j):
        return j * QBLK if isinstance(j, int) else pl.multiple_of(j * QBLK, QBLK)

    def logits(j, nvb):
        qf = q_ref[0, pl.ds(rows_of(j), QBLK), :].astype(F32) * (HEAD_DIM ** -0.5 * LOG2E)
        parts = []
        for c in range(ncol):
            qc = qf[:, c * LANES:(c + 1) * LANES]
            parts.append(jnp.where(low, qc, 0.0))
            parts.append(jnp.where(low, 0.0, qc))
        lhs = jnp.concatenate(parts, axis=0).astype(BF16)
        kw = k_src[pl.ds(rows_of(j - (nvb - 1)), nvb * QBLK), :]
        s = lax.dot_general(kw, lhs, (((1,), (1,)), ((), ())), preferred_element_type=F32)
        return s + bias_ref[0, (nw - nvb) * QBLK:, :]

    def softmax(s):
        m = jnp.max(s, axis=0, keepdims=True)
        if has_sink:
            m = jnp.maximum(m, sink)
        p = jnp.exp2(s - m)
        denom = jnp.sum(p, axis=0, keepdims=True)
        if has_sink:
            denom = denom + jnp.exp2(sink - m)
        return p.astype(BF16), denom

    def emit(j, nvb, p, denom):
        vw = vt_ref[:, pl.ds(rows_of(j - (nvb - 1)), nvb * QBLK)]
        o = jnp.dot(vw, p, preferred_element_type=F32) / denom
        for c in range(ncol):
            o0 = o[:, (2 * c) * QBLK:(2 * c + 1) * QBLK]
            o1 = o[:, (2 * c + 1) * QBLK:(2 * c + 2) * QBLK]
            o_ref[0, pl.ds(rows_of(j), QBLK), c * LANES:(c + 1) * LANES] = (
                jnp.where(top, o0, o1).T.astype(BF16))

    s_scr, p_scr = scratch[-2:]
    last = n_blk - 1

    def n_valid(j):
        return min(j + 1, nw) if isinstance(j, int) else nw

    def put_logits(u, j):
        s_scr[u, :n_valid(j) * QBLK, :] = logits(j, n_valid(j))

    def put_probs(u, j):
        rows_n = n_valid(j) * QBLK
        p, den = softmax(s_scr[u, :rows_n, :])
        p_scr[u, :rows_n, :] = p
        return den

    def pair(j, d_even, clamp):
        emit(j, n_valid(j), p_scr[0, :n_valid(j) * QBLK, :], d_even)
        d_odd = put_probs(1, j + 1)
        put_logits(0, clamp(j + 2))
        emit(j + 1, n_valid(j + 1), p_scr[1, :n_valid(j + 1) * QBLK, :], d_odd)
        d_next = put_probs(0, clamp(j + 2))
        put_logits(1, clamp(j + 3))
        return d_next

    first = nw - 1 + (nw - 1) % 2
    assert n_blk % 2 == 0 and first + 4 <= n_blk
    put_logits(0, 0)
    den = put_probs(0, 0)
    put_logits(1, 1)
    for j in range(0, first, 2):
        den = pair(j, den, lambda k: k)

    lax.fori_loop(0, (n_blk - first) // 2,
                  lambda t, d: pair(first + 2 * t, d, lambda k: jnp.minimum(k, last)), den)


def _attention(proj, bias, sinks, *, n_groups, ncol, nw, q_col0, k_col0, v_col0, kv_share, dup):
    b, s, _ = proj.shape
    qw = ncol * LANES
    has_sink = sinks is not None
    kernel = functools.partial(_attn_kernel, ncol=ncol, nw=nw, dup=dup, has_sink=has_sink)
    _, n_keys, n_q = bias.shape
    in_specs = [
        pl.BlockSpec((1, s, qw), lambda bi, g, *_: (bi, 0, q_col0 // ncol + g)),
        pl.BlockSpec((1, s, LANES), lambda bi, g, *_: (bi, 0, k_col0 + g // kv_share)),
        pl.BlockSpec((1, s, LANES), lambda bi, g, *_: (bi, 0, v_col0 + g // kv_share)),
        pl.BlockSpec((1, n_keys, n_q), lambda bi, g, *_: (g, 0, 0)),
    ]
    out_spec = pl.BlockSpec((1, s, qw), lambda bi, g, *_: (bi, 0, g))
    scratch = [pltpu.VMEM((s, LANES), BF16)] if dup else []
    scratch += [pltpu.VMEM((LANES, s), BF16),
                pltpu.VMEM((2, n_keys, n_q), F32), pltpu.VMEM((2, n_keys, n_q), BF16)]
    grid_spec = pltpu.PrefetchScalarGridSpec(
        num_scalar_prefetch=1 if has_sink else 0,
        grid=(b, n_groups),
        in_specs=in_specs,
        out_specs=out_spec,
        scratch_shapes=scratch,
    )
    args = ((sinks,) if has_sink else ()) + (proj, proj, proj, bias)
    return pl.pallas_call(
        kernel,
        out_shape=jax.ShapeDtypeStruct((b, s, n_groups * qw), BF16),
        grid_spec=grid_spec,
        compiler_params=pltpu.CompilerParams(
            dimension_semantics=("arbitrary", "arbitrary"), vmem_limit_bytes=VMEM_LIMIT),
        name="attn_a" if dup else "attn_b",
    )(*args)


def _band_tables(rel_bias_b):
    qi = jnp.arange(QBLK)[:, None]
    r = qi // CHUNK

    def table(nw, n_prev, fn):
        kj = jnp.arange(nw * QBLK)[None, :]
        rel = (nw - 1) * QBLK + qi - kj
        inband = (kj >= r * CHUNK) & (kj < (r + n_prev + 1) * CHUNK)
        return jnp.where(inband[None], fn(rel), NEG_INF)

    slopes = jnp.exp2(-8.0 * jnp.arange(1, N_HEADS_A + 1, dtype=F32) / N_HEADS_A)
    nw_a = (N_PREV_A * CHUNK) // QBLK + 1
    nw_b = (N_PREV_B * CHUNK) // QBLK + 1
    bias_a = table(nw_a, N_PREV_A, lambda rel: -slopes[:, None, None] * jnp.abs(rel).astype(F32)[None])
    w_b = nw_b * QBLK
    row = -(-(QBLK + w_b - 2) // LANES) * LANES
    p = row + 1
    t = np.arange(p)
    m = np.where(t < w_b, t, t - p)
    dist = np.clip((nw_b - 1) * QBLK - m, -REL_CLIP, REL_CLIP) + REL_CLIP
    vec = rel_bias_b[:, dist].astype(F32)
    reps = -(-(QBLK * row) // p)
    rel_b = jnp.tile(vec, (1, reps))[:, :QBLK * row].reshape(N_HEADS_B, QBLK, row)[:, :, :w_b]
    bias_b = table(nw_b, N_PREV_B, lambda rel: rel_b)
    rep = N_HEADS_A // N_KV_A
    bias_a = bias_a.reshape(N_KV_A, rep * QBLK, nw_a * QBLK).transpose(0, 2, 1)
    bias_b = bias_b.reshape(N_HEADS_B // 2, 2 * QBLK, nw_b * QBLK).transpose(0, 2, 1)
    return bias_a * LOG2E, nw_a, bias_b * LOG2E, nw_b


def _mix_kernel(oa_ref, ob_ref, x_ref, mod_ref, ga_ref, gb_ref, gf_ref, wo_ref, wr_ref, br_ref,
                x1_ref, h2_ref, rt_ref, ids_ref):
    na = _rms(oa_ref[0].astype(F32), ga_ref[...]).astype(BF16)
    nb = _rms(ob_ref[0].astype(F32), gb_ref[...]).astype(BF16)
    acc = jnp.dot(jnp.concatenate([na, nb], axis=1), wo_ref[...], preferred_element_type=F32)
    x1 = x_ref[0] + mod_ref[0, 2:3, :] * acc
    x1_ref[0] = x1
    h2 = _rms(x1, gf_ref[...]) * (1.0 + mod_ref[0, 4:5, :]) + mod_ref[0, 3:4, :]
    h2_ref[0] = h2

    tm = h2.shape[0]
    h_hi = h2.astype(BF16)
    h_lo = (h2 - h_hi.astype(F32)).astype(BF16)
    r = jnp.dot(jnp.concatenate([h_hi, h_lo], axis=0), wr_ref[...], preferred_element_type=F32)
    r = r[:tm] + r[tm:]
    logits = r + pltpu.roll(r, LANES - ROUTER_LO_LANE, 1) + br_ref[...]

    lane = lax.broadcasted_iota(jnp.int32, (tm, LANES), 1)
    lane_f = lane.astype(F32)
    big = float(LANES)
    ninf = -jnp.inf

    def first_max(vals):
        top = jnp.max(vals, axis=-1, keepdims=True)
        idx = jnp.min(jnp.where(vals == top, lane_f, big), axis=-1, keepdims=True)
        return top, idx

    is_g = lane < N_GROUPS
    g_top, g_idx = first_max(jnp.where(is_g, logits, ninf))
    p_g = 1.0 / jnp.sum(jnp.where(is_g, jnp.exp(logits - g_top), 0.0), axis=-1, keepdims=True)
    lo = N_GROUPS + g_idx * EXPERTS_PER_GROUP
    e_vals = jnp.where((lane_f >= lo) & (lane_f < lo + EXPERTS_PER_GROUP), logits, ninf)
    v1, i1 = first_max(e_vals)
    v2, i2 = first_max(jnp.where(lane_f == i1, ninf, e_vals))
    e2 = jnp.exp(v2 - v1)
    w1 = p_g / (1.0 + e2)
    w2 = p_g * e2 / (1.0 + e2)
    rt = jnp.where(lane == 0, i1 - N_GROUPS,
                   jnp.where(lane == 1, i2 - N_GROUPS,
                             jnp.where(lane == 2, w1, jnp.where(lane == 3, w2, 0.0))))
    rt_ref[...] = rt
    pick = (lax.broadcasted_iota(jnp.int32, (SUBLANES, LANES), 0)
            == lax.broadcasted_iota(jnp.int32, (SUBLANES, LANES), 1)).astype(BF16)
    ids_ref[...] = lax.dot_general(pick, rt.astype(BF16), (((1,), (1,)), ((), ())),
                                   preferred_element_type=F32)


def _mix(o_a, o_b, x, mod, g_out_a, g_out_b, g_ffn, w_out_bf16, w_router, b_router):
    b, s, d = x.shape
    half = o_a.shape[2]
    tm = MIX_TM
    nt = s // tm
    vec = lambda n: pl.BlockSpec((1, n), lambda bi, i: (0, 0))
    return pl.pallas_call(
        _mix_kernel,
        out_shape=(jax.ShapeDtypeStruct((b, s, d), F32),
                   jax.ShapeDtypeStruct((b, s, d), F32),
                   jax.ShapeDtypeStruct((b * s, LANES), F32),
                   jax.ShapeDtypeStruct((SUBLANES, b * s), F32)),
        grid=(b, nt),
        in_specs=[
            pl.BlockSpec((1, tm, half), lambda bi, i: (bi, i, 0)),
            pl.BlockSpec((1, tm, half), lambda bi, i: (bi, i, 0)),
            pl.BlockSpec((1, tm, d), lambda bi, i: (bi, i, 0)),
            pl.BlockSpec((1, 6, d), lambda bi, i: (bi, 0, 0)),
            vec(half), vec(half), vec(d),
            pl.BlockSpec((d, d), lambda bi, i: (0, 0), pipeline_mode=pl.Buffered(1)),
            pl.BlockSpec((d, LANES), lambda bi, i: (0, 0)),
            vec(LANES),
        ],
        out_specs=(pl.BlockSpec((1, tm, d), lambda bi, i: (bi, i, 0)),
                   pl.BlockSpec((1, tm, d), lambda bi, i: (bi, i, 0)),
                   pl.BlockSpec((tm, LANES), lambda bi, i: (bi * nt + i, 0)),
                   pl.BlockSpec((SUBLANES, tm), lambda bi, i: (0, bi * nt + i))),
        compiler_params=pltpu.CompilerParams(
            dimension_semantics=("arbitrary", "arbitrary"), vmem_limit_bytes=VMEM_LIMIT),
        name="mix_out",
    )(o_a, o_b, x, mod, g_out_a.reshape(1, half), g_out_b.reshape(1, half), g_ffn.reshape(1, d),
      w_out_bf16, w_router, b_router.reshape(1, LANES))


DISPATCH_STEPS_MIN = ROW_BUFS


def _dispatch_kernel(slot_ref, zf_ref, h_ref, xs_hbm, buf, zbuf, sem, zsem):
    i = pl.program_id(0)
    n_steps = pl.num_programs(0)
    td = h_ref.shape[0]
    tm = zbuf.shape[0]
    n_tiles_max = xs_hbm.shape[0] // tm

    @pl.when(i == 0)
    def _():
        zbuf[...] = jnp.zeros(zbuf.shape, zbuf.dtype)

        def fill_copy(t):
            z0 = pl.multiple_of(zf_ref[t], SUBLANES)
            n = pl.multiple_of(tm - z0, SUBLANES)
            row0 = pl.multiple_of(t * tm + z0, SUBLANES)
            return pltpu.make_async_copy(zbuf.at[pl.ds(0, n)], xs_hbm.at[pl.ds(row0, n)], zsem.at[0])

        def fill(t, carry):
            @pl.when(zf_ref[t] < tm)
            def _():
                fill_copy(t).start()
            return carry
        lax.fori_loop(0, n_tiles_max, fill, 0)

        def drain(t, carry):
            @pl.when(zf_ref[t] < tm)
            def _():
                fill_copy(t).wait()
            return carry
        lax.fori_loop(0, n_tiles_max, drain, 0)

    def wait_rows(s):
        for _ in range(2):
            pltpu.make_async_copy(buf.at[s], xs_hbm.at[pl.ds(0, td)], sem.at[s]).wait()

    def step(s):
        @pl.when(i >= ROW_BUFS)
        def _():
            wait_rows(s)

        buf[s] = h_ref[...]
        for r in range(td):
            for k in range(2):
                dst = slot_ref[k * (n_steps * td) + i * td + r]
                pltpu.make_async_copy(buf.at[s, pl.ds(r, 1)], xs_hbm.at[pl.ds(dst, 1)],
                                      sem.at[s]).start(priority=k)

        @pl.when(i == n_steps - 1)
        def _():
            for t in range(min(ROW_BUFS, DISPATCH_STEPS_MIN)):
                wait_rows((s + ROW_BUFS - t) % ROW_BUFS)

    for s in range(ROW_BUFS):
        pl.when(i % ROW_BUFS == s)(functools.partial(step, s))


def _dispatch(h2, slot, zero_fill, n_pad):
    t, d = h2.shape
    tm = MOE_TM
    assert t // DISPATCH_TM >= DISPATCH_STEPS_MIN
    grid_spec = pltpu.PrefetchScalarGridSpec(
        num_scalar_prefetch=2,
        grid=(t // DISPATCH_TM,),
        in_specs=[pl.BlockSpec((DISPATCH_TM, d), lambda i, *_: (i, 0))],
        out_specs=pl.BlockSpec(memory_space=pl.ANY),
        scratch_shapes=[
            pltpu.VMEM((ROW_BUFS, DISPATCH_TM, d), F32),
            pltpu.VMEM((tm, d), F32),
            pltpu.SemaphoreType.DMA((ROW_BUFS,)),
            pltpu.SemaphoreType.DMA((1,)),
        ],
    )
    return pl.pallas_call(
        _dispatch_kernel,
        out_shape=jax.ShapeDtypeStruct((n_pad, d), F32),
        grid_spec=grid_spec,
        compiler_params=pltpu.CompilerParams(
            dimension_semantics=("arbitrary",), vmem_limit_bytes=VMEM_LIMIT),
        name="dispatch",
    )(slot, zero_fill, h2)


def _moe_kernel(te_ref, nxe_ref, nt_ref, x_ref, wg_hbm, wu_hbm, wd_hbm, o_ref,
                wg32, wu32, wd32, wgb, wub, wdb, wsem):
    i = pl.program_id(0)
    n_tiles = nt_ref[0]

    def weight_copies(e):
        return [pltpu.make_async_copy(src.at[e], dst, wsem.at[0])
                for src, dst in ((wg_hbm, wg32), (wu_hbm, wu32), (wd_hbm, wd32))]

    @pl.when(i == 0)
    def _():
        for cp in weight_copies(te_ref[0]):
            cp.start(priority=1)

    @pl.when(i < n_tiles)
    def _():
        @pl.when(jnp.logical_or(i == 0, te_ref[i] != te_ref[jnp.maximum(i - 1, 0)]))
        def _():
            for cp in weight_copies(0):
                cp.wait()
            wgb[...] = wg32[...].astype(BF16)
            wub[...] = wu32[...].astype(BF16)
            wdb[...] = wd32[...].astype(BF16)

            @pl.when(nxe_ref[i] >= 0)
            def _():
                for cp in weight_copies(nxe_ref[i]):
                    cp.start(priority=1)

        xb = x_ref[...].astype(BF16)
        g = jnp.dot(xb, wgb[...], preferred_element_type=F32)
        u = jnp.dot(xb, wub[...], preferred_element_type=F32)
        a = (g * jax.nn.sigmoid(g) * u).astype(BF16)
        o_ref[...] = jnp.dot(a, wdb[...], preferred_element_type=F32)

    @pl.when(i >= n_tiles)
    def _():
        o_ref[...] = jnp.zeros(o_ref.shape, o_ref.dtype)


def _moe(xs, w_gate, w_up, w_down, tile_expert, next_expert, n_tiles):
    n_pad, d = xs.shape
    tm = MOE_TM
    max_tiles = n_pad // tm
    de = w_gate.shape[2]
    grid_spec = pltpu.PrefetchScalarGridSpec(
        num_scalar_prefetch=3,
        grid=(max_tiles,),
        in_specs=[pl.BlockSpec((tm, d), lambda i, te, nxe, nt: (jnp.minimum(i, nt[0] - 1), 0))]
        + [pl.BlockSpec(memory_space=pl.ANY)] * 3,
        out_specs=pl.BlockSpec((tm, d), lambda i, *_: (i, 0)),
        scratch_shapes=[
            pltpu.VMEM((d, de), F32),
            pltpu.VMEM((d, de), F32),
            pltpu.VMEM((de, d), F32),
            pltpu.VMEM((d, de), BF16),
            pltpu.VMEM((d, de), BF16),
            pltpu.VMEM((de, d), BF16),
            pltpu.SemaphoreType.DMA((1,)),
        ],
    )
    return pl.pallas_call(
        _moe_kernel,
        out_shape=jax.ShapeDtypeStruct((n_pad, d), F32),
        grid_spec=grid_spec,
        compiler_params=pltpu.CompilerParams(
            dimension_semantics=("arbitrary",), vmem_limit_bytes=VMEM_LIMIT),
        name="moe",
    )(tile_expert, next_expert, n_tiles, xs, w_gate, w_up, w_down)


def _route_plan(ids, n_tok):
    tm = MOE_TM
    n_asg = 2 * n_tok
    n_pad = n_asg + N_EXPERTS * tm
    max_tiles = n_pad // tm
    e_flat = ids[:2].astype(jnp.int32).reshape(-1)
    onehot = e_flat[None, :] == jnp.arange(N_EXPERTS)[:, None]
    blk = LANES
    oh3 = onehot.reshape(N_EXPERTS, n_asg // blk, blk).astype(BF16)
    upper = (jnp.arange(blk)[:, None] <= jnp.arange(blk)[None, :]).astype(BF16)
    within = jnp.einsum("ebj,jk->ebk", oh3, upper, preferred_element_type=F32).astype(jnp.int32)
    blk_tot = within[:, :, -1]
    offs = jnp.cumsum(blk_tot, axis=1) - blk_tot
    csum = (within + offs[:, :, None]).reshape(N_EXPERTS, n_asg)
    counts = jnp.sum(blk_tot, axis=1)
    oh_i = onehot.astype(jnp.int32)
    rank = jnp.sum(csum * oh_i, axis=0) - 1
    tiles_e = (counts + tm - 1) // tm
    tile_end = jnp.cumsum(tiles_e)
    base = (tile_end - tiles_e) * tm
    slot = jnp.sum(oh_i * base[:, None], axis=0) + rank
    n_tiles = tile_end[-1]
    tile_id = jnp.arange(max_tiles)
    te = jnp.sum((tile_id[:, None] >= tile_end[None, :]).astype(jnp.int32), axis=1)
    te_last = jnp.sum(((n_tiles - 1) >= tile_end).astype(jnp.int32))
    tile_expert = jnp.where(tile_id < n_tiles, te, te_last).astype(jnp.int32)
    ex = jnp.arange(N_EXPERTS)
    later = (ex[None, :] > ex[:, None]) & (counts[None, :] > 0)
    nxt_e = jnp.min(jnp.where(later, ex[None, :], N_EXPERTS), axis=1)
    nxt_e = jnp.where(nxt_e < N_EXPERTS, nxt_e, -1)
    sel = (tile_expert[:, None] == ex[None, :]).astype(jnp.int32)
    next_expert = jnp.sum(sel * nxt_e[None, :], axis=1).astype(jnp.int32)
    last_tile = jnp.sum(sel * (tile_end - 1)[None, :], axis=1)
    used_last = jnp.sum(sel * (counts - (tiles_e - 1) * tm)[None, :], axis=1)
    zero_fill = jnp.where(tile_id >= n_tiles, 0,
                          jnp.where(tile_id == last_tile, (used_last // SUBLANES) * SUBLANES,
                                    tm)).astype(jnp.int32)
    return (tile_expert, next_expert, n_tiles.reshape(1).astype(jnp.int32), zero_fill,
            slot.astype(jnp.int32), n_pad)


def _final_kernel(slot_ref, x1_ref, rt_ref, mod_ref, modf_ref, g_ref, y_hbm, o_ref, ybuf, gsem):
    i = pl.program_id(0)
    n_steps = pl.num_programs(0)
    tm = x1_ref.shape[0]

    def row_in(tile, r, k, s):
        idx = slot_ref[k * (n_steps * tm) + tile * tm + r]
        pltpu.make_async_copy(y_hbm.at[pl.ds(idx, 1)], ybuf.at[s, pl.ds(k * tm + r, 1)],
                              gsem.at[s]).start(priority=k)

    def wait_in(s):
        pltpu.make_async_copy(y_hbm.at[pl.ds(0, 2 * tm)], ybuf.at[s], gsem.at[s]).wait()

    @pl.when(i == 0)
    def _():
        for t in range(ROW_BUFS - 1):
            tile = jnp.minimum(t, n_steps - 1)

            def row(r, carry):
                row_in(tile, r, 0, t)
                row_in(tile, r, 1, t)
                return carry
            lax.fori_loop(0, tm, row, 0, unroll=8)

    def step(s):
        wait_in(s)
        rt = rt_ref[...]
        y = rt[:, 2:3] * ybuf[s, :tm, :] + rt[:, 3:4] * ybuf[s, tm:, :]
        nxt = jnp.minimum(i + ROW_BUFS - 1, n_steps - 1)
        for r in range(tm):
            row_in(nxt, r, 0, (s + ROW_BUFS - 1) % ROW_BUFS)
            row_in(nxt, r, 1, (s + ROW_BUFS - 1) % ROW_BUFS)
        x2 = x1_ref[...] + mod_ref[0, 5:6, :] * y
        o_ref[...] = _rms(x2, g_ref[...]) * (1.0 + modf_ref[0, 1:2, :]) + modf_ref[0, 0:1, :]

        @pl.when(i == n_steps - 1)
        def _():
            for t in range(1, ROW_BUFS):
                wait_in((s + t) % ROW_BUFS)

    for s in range(ROW_BUFS):
        pl.when(i % ROW_BUFS == s)(functools.partial(step, s))


def _final(x1, y_rows, slot, rt, mod, modf, g_final):
    b, s, d = x1.shape
    tm = FINAL_TM
    nt = s // tm
    grid_spec = pltpu.PrefetchScalarGridSpec(
        num_scalar_prefetch=1,
        grid=(b * nt,),
        in_specs=[
            pl.BlockSpec((tm, d), lambda i, *_: (i, 0)),
            pl.BlockSpec((tm, LANES), lambda i, *_: (i, 0)),
            pl.BlockSpec((1, 6, d), lambda i, *_: (i // nt, 0, 0)),
            pl.BlockSpec((1, 2, d), lambda i, *_: (i // nt, 0, 0)),
            pl.BlockSpec((1, d), lambda i, *_: (0, 0)),
            pl.BlockSpec(memory_space=pl.ANY),
        ],
        out_specs=pl.BlockSpec((tm, d), lambda i, *_: (i, 0)),
        scratch_shapes=[
            pltpu.VMEM((ROW_BUFS, 2 * tm, d), F32),
            pltpu.SemaphoreType.DMA((ROW_BUFS,)),
        ],
    )
    out = pl.pallas_call(
        _final_kernel,
        out_shape=jax.ShapeDtypeStruct((b * s, d), F32),
        grid_spec=grid_spec,
        compiler_params=pltpu.CompilerParams(
            dimension_semantics=("arbitrary",), vmem_limit_bytes=VMEM_LIMIT),
        name="final",
    )(slot, x1.reshape(b * s, d), rt, mod, modf, g_final.reshape(1, d), y_rows)
    return out.reshape(b, s, d)


def kernel(x, c, w_ada, b_ada, g_mix, w_in, sinks_a, rel_bias_b, g_out_a, g_out_b, w_out, g_ffn,
           w_router_group, b_router_group, w_router_expert, b_router_expert, w_gate, w_up, w_down,
           w_ada_final, b_ada_final, g_final):
    b, s, d = x.shape
    assert w_ada.shape[0] == 1, "one layer"
    n_tok = b * s

    c_act = jax.nn.silu(c)
    a_rep = jnp.broadcast_to(c_act[:, :, None], (b, d, LANES))
    mod = _ada(a_rep, w_ada[0], b_ada[0]).reshape(b, 6, d)
    modf = _ada(a_rep, w_ada_final, b_ada_final).reshape(b, 2, d)

    proj = _proj(x, mod, g_mix[0], w_in[0].astype(BF16))

    bias_a, nw_a, bias_b, nw_b = _band_tables(rel_bias_b[0])
    kv_a0 = DA_Q // LANES
    o_a = _attention(proj, bias_a, sinks_a[0].astype(F32), n_groups=N_KV_A, ncol=2, nw=nw_a,
                     q_col0=0, k_col0=kv_a0, v_col0=kv_a0 + DA_KV // LANES, kv_share=2, dup=True)
    qb0 = (DA_Q + 2 * DA_KV) // LANES
    o_b = _attention(proj, bias_b, None, n_groups=N_HEADS_B // 2, ncol=1, nw=nw_b,
                     q_col0=qb0, k_col0=qb0 + DB // LANES, v_col0=qb0 + 2 * DB // LANES,
                     kv_share=1, dup=False)

    n_r = N_GROUPS + N_EXPERTS
    assert n_r <= ROUTER_LO_LANE
    w_r = jnp.concatenate([w_router_group[0], w_router_expert[0]], axis=1)
    w_r_hi = w_r.astype(BF16)
    w_r_lo = (w_r - w_r_hi.astype(F32)).astype(BF16)
    w_router = (jnp.zeros((d, LANES), BF16).at[:, :n_r].set(w_r_hi)
                .at[:, ROUTER_LO_LANE:ROUTER_LO_LANE + n_r].set(w_r_lo))
    b_router = jnp.zeros((LANES,), F32).at[:n_r].set(
        jnp.concatenate([b_router_group[0], b_router_expert[0]]))
    x1, h2, rt, ids = _mix(o_a, o_b, x, mod, g_out_a[0], g_out_b[0], g_ffn[0], w_out[0].astype(BF16),
                      w_router, b_router)

    tile_expert, next_expert, n_tiles, zero_fill, slot, n_pad = _route_plan(ids, n_tok)
    xs = _dispatch(h2.reshape(n_tok, d), slot, zero_fill, n_pad)
    y = _moe(xs, w_gate[0], w_up[0], w_down[0], tile_expert, next_expert, n_tiles)

    return _final(x1, y, slot, rt, mod, modf, g_final)
```

```python
import functools

import jax
import jax.numpy as jnp
import numpy as np
from jax import lax
from jax.experimental import pallas as pl
from jax.experimental.pallas import tpu as pltpu

D_MODEL = 2048
CHUNK = 64
HEAD_DIM = 64
N_HEADS_A = 16
N_KV_A = 4
N_PREV_A = 2
N_HEADS_B = 16
N_PREV_B = 8
REL_CLIP = 128
DA_Q = N_HEADS_A * HEAD_DIM
DA_KV = N_KV_A * HEAD_DIM
DB = N_HEADS_B * HEAD_DIM
N_GROUPS = 4
EXPERTS_PER_GROUP = 8
N_EXPERTS = N_GROUPS * EXPERTS_PER_GROUP
EPS = 1e-6
NEG_INF = -1e30
LOG2E = 1.4426950408889634

LANES = 128
SUBLANES = 8
VMEM_LIMIT = 56 * 1024 * 1024

ADA_TN = 1024
ADA_JB = 8
PROJ_TM = 512
PROJ_N_CHUNK = 512
QBLK = 2 * CHUNK
KV_PREP_ROWS = 512
MIX_TM = 512
MOE_TM = 512
DISPATCH_TM = 512
FINAL_TM = 256
ROW_BUFS = 3
ROUTER_LO_LANE = 64

F32 = jnp.float32
BF16 = jnp.bfloat16


def _rms(x, g):
    return x * lax.rsqrt(jnp.mean(x * x, axis=-1, keepdims=True) + EPS) * g


def _ada_kernel(a_ref, w_ref, b_ref, o_ref):
    n_b, k, _ = a_ref.shape
    tn = w_ref.shape[1]
    sub = SUBLANES
    for jb in range(tn // (ADA_JB * LANES)):
        col0 = jb * ADA_JB * LANES

        def body(kc, accs):
            k0 = pl.multiple_of(kc * sub, sub)
            a_rows = [a_ref[b, pl.ds(k0, sub), :] for b in range(n_b)]
            out = []
            for j in range(ADA_JB):
                w = w_ref[pl.ds(k0, sub), col0 + j * LANES:col0 + (j + 1) * LANES]
                out.append([accs[j][b] + a_rows[b] * w for b in range(n_b)])
            return out

        zero = jnp.zeros((sub, LANES), F32)
        accs = lax.fori_loop(0, k // sub, body, [[zero] * n_b for _ in range(ADA_JB)], unroll=4)
        for j in range(ADA_JB):
            cols = slice(col0 + j * LANES, col0 + (j + 1) * LANES)
            for b in range(n_b):
                o_ref[b:b + 1, cols] = jnp.sum(accs[j][b], axis=0, keepdims=True) + b_ref[:, cols]


def _ada(a_rep, w, bias):
    n_b, k, _ = a_rep.shape
    n = w.shape[1]
    tn = ADA_TN
    return pl.pallas_call(
        _ada_kernel,
        out_shape=jax.ShapeDtypeStruct((n_b, n), F32),
        grid=(n // tn,),
        in_specs=[
            pl.BlockSpec((n_b, k, LANES), lambda j: (0, 0, 0)),
            pl.BlockSpec((k, tn), lambda j: (0, j)),
            pl.BlockSpec((1, tn), lambda j: (0, j)),
        ],
        out_specs=pl.BlockSpec((n_b, tn), lambda j: (0, j)),
        compiler_params=pltpu.CompilerParams(
            dimension_semantics=("arbitrary",), vmem_limit_bytes=VMEM_LIMIT),
        name="ada",
    )(a_rep, w, bias.reshape(1, n))


def _proj_kernel(x_ref, mod_ref, g_ref, w_ref, o_ref, *, n_chunk):
    h = _rms(x_ref[0], g_ref[...])
    h = h * (1.0 + mod_ref[0, 1:2, :]) + mod_ref[0, 0:1, :]
    hb = h.astype(BF16)
    for n0 in range(0, o_ref.shape[2], n_chunk):
        cols = slice(n0, n0 + n_chunk)
        o_ref[0, :, cols] = jnp.dot(hb, w_ref[:, cols], preferred_element_type=F32).astype(BF16)


def _proj(x, mod, g_mix, w_in_bf16):
    b, s, d = x.shape
    n = w_in_bf16.shape[1]
    tm = PROJ_TM
    return pl.pallas_call(
        functools.partial(_proj_kernel, n_chunk=PROJ_N_CHUNK),
        out_shape=jax.ShapeDtypeStruct((b, s, n), BF16),
        grid=(b, s // tm),
        in_specs=[
            pl.BlockSpec((1, tm, d), lambda bi, i: (bi, i, 0)),
            pl.BlockSpec((1, 6, d), lambda bi, i: (bi, 0, 0)),
            pl.BlockSpec((1, d), lambda bi, i: (0, 0)),
            pl.BlockSpec((d, n), lambda bi, i: (0, 0), pipeline_mode=pl.Buffered(1)),
        ],
        out_specs=pl.BlockSpec((1, tm, n), lambda bi, i: (bi, i, 0)),
        compiler_params=pltpu.CompilerParams(
            dimension_semantics=("arbitrary", "arbitrary"), vmem_limit_bytes=VMEM_LIMIT),
        name="proj",
    )(x, mod, g_mix.reshape(1, d), w_in_bf16)


def _attn_kernel(*refs, ncol, nw, dup, has_sink):
    if has_sink:
        sink_ref, q_ref, k_ref, v_ref, bias_ref, o_ref = refs[:6]
        scratch = refs[6:]
    else:
        q_ref, k_ref, v_ref, bias_ref, o_ref = refs[:5]
        scratch = refs[5:]
        sink_ref = None
    s_len = q_ref.shape[1]
    n_blk = s_len // QBLK
    n_stack = 2 * ncol
    grp = pl.program_id(1)

    rows = KV_PREP_ROWS
    lane = lax.broadcasted_iota(jnp.int32, (QBLK, LANES), 1)
    low = lane < HEAD_DIM
    if dup:
        kd_ref, vt_ref = scratch[:2]
        half = grp % 2
        keep = (lax.broadcasted_iota(jnp.int32, (rows, LANES), 1) // HEAD_DIM) == half

        def spread(t):
            return jnp.where(keep, t, pltpu.roll(t, HEAD_DIM, 1))

        def dup_body(c, carry):
            r0 = pl.multiple_of(c * rows, rows)
            kd_ref[pl.ds(r0, rows), :] = spread(k_ref[0, pl.ds(r0, rows), :].astype(F32)).astype(BF16)
            return carry

        lax.fori_loop(0, s_len // rows, dup_body, 0)
        k_src = kd_ref
    else:
        vt_ref = scratch[0]
        k_src = k_ref.at[0]

        def spread(t):
            return t

    for c in range(s_len // rows):
        t = spread(v_ref[0, c * rows:(c + 1) * rows, :].astype(F32))
        vt_ref[:, c * rows:(c + 1) * rows] = t.T.astype(BF16)

    top = lax.broadcasted_iota(jnp.int32, (LANES, QBLK), 0) < HEAD_DIM

    if has_sink:
        sink = jnp.concatenate(
            [jnp.full((1, QBLK), sink_ref[grp * n_stack + h] * LOG2E, F32) for h in range(n_stack)],
            axis=1)

    def rows_of(j):
        return j * QBLK if isinstance(j, int) else pl.multiple_of(j * QBLK, QBLK)

    def logits(j, nvb):
        qf = q_ref[0, pl.ds(rows_of(j), QBLK), :].astype(F32) * (HEAD_DIM ** -0.5 * LOG2E)
        parts = []
        for c in range(ncol):
            qc = qf[:, c * LANES:(c + 1) * LANES]
            parts.append(jnp.where(low, qc, 0.0))
            parts.append(jnp.where(low, 0.0, qc))
        lhs = jnp.concatenate(parts, axis=0).astype(BF16)
        kw = k_src[pl.ds(rows_of(j - (nvb - 1)), nvb * QBLK), :]
        s = lax.dot_general(kw, lhs, (((1,), (1,)), ((), ())), preferred_element_type=F32)
        return s + bias_ref[0, (nw - nvb) * QBLK:, :]

    def softmax(s):
        m = jnp.max(s, axis=0, keepdims=True)
        if has_sink:
            m = jnp.maximum(m, sink)
        p = jnp.exp2(s - m)
        denom = jnp.sum(p, axis=0, keepdims=True)
        if has_sink:
            denom = denom + jnp.exp2(sink - m)
        return p.astype(BF16), denom

    def emit(j, nvb, p, denom):
        vw = vt_ref[:, pl.ds(rows_of(j - (nvb - 1)), nvb * QBLK)]
        o = jnp.dot(vw, p, preferred_element_type=F32) / denom
        for c in range(ncol):
            o0 = o[:, (2 * c) * QBLK:(2 * c + 1) * QBLK]
            o1 = o[:, (2 * c + 1) * QBLK:(2 * c + 2) * QBLK]
            o_ref[0, pl.ds(rows_of(j), QBLK), c * LANES:(c + 1) * LANES] = (
                jnp.where(top, o0, o1).T.astype(BF16))

    s_scr, p_scr = scratch[-2:]
    last = n_blk - 1

    def n_valid(j):
        return min(j + 1, nw) if isinstance(j, int) else nw

    def put_logits(u, j):
        s_scr[u, :n_valid(j) * QBLK, :] = logits(j, n_valid(j))

    def put_probs(u, j):
        rows_n = n_valid(j) * QBLK
        p, den = softmax(s_scr[u, :rows_n, :])
        p_scr[u, :rows_n, :] = p
        return den

    def pair(j, d_even, clamp):
        emit(j, n_valid(j), p_scr[0, :n_valid(j) * QBLK, :], d_even)
        d_odd = put_probs(1, j + 1)
        put_logits(0, clamp(j + 2))
        emit(j + 1, n_valid(j + 1), p_scr[1, :n_valid(j + 1) * QBLK, :], d_odd)
        d_next = put_probs(0, clamp(j + 2))
        put_logits(1, clamp(j + 3))
        return d_next

    first = nw - 1 + (nw - 1) % 2
    assert n_blk % 2 == 0 and first + 4 <= n_blk
    put_logits(0, 0)
    den = put_probs(0, 0)
    put_logits(1, 1)
    for j in range(0, first, 2):
        den = pair(j, den, lambda k: k)

    lax.fori_loop(0, (n_blk - first) // 2,
                  lambda t, d: pair(first + 2 * t, d, lambda k: jnp.minimum(k, last)), den)


def _attention(proj, bias, sinks, *, n_groups, ncol, nw, q_col0, k_col0, v_col0, kv_share, dup):
    b, s, _ = proj.shape
    qw = ncol * LANES
    has_sink = sinks is not None
    kernel = functools.partial(_attn_kernel, ncol=ncol, nw=nw, dup=dup, has_sink=has_sink)
    _, n_keys, n_q = bias.shape
    in_specs = [
        pl.BlockSpec((1, s, qw), lambda bi, g, *_: (bi, 0, q_col0 // ncol + g)),
        pl.BlockSpec((1, s, LANES), lambda bi, g, *_: (bi, 0, k_col0 + g // kv_share)),
        pl.BlockSpec((1, s, LANES), lambda bi, g, *_: (bi, 0, v_col0 + g // kv_share)),
        pl.BlockSpec((1, n_keys, n_q), lambda bi, g, *_: (g, 0, 0)),
    ]
    out_spec = pl.BlockSpec((1, s, qw), lambda bi, g, *_: (bi, 0, g))
    scratch = [pltpu.VMEM((s, LANES), BF16)] if dup else []
    scratch += [pltpu.VMEM((LANES, s), BF16),
                pltpu.VMEM((2, n_keys, n_q), F32), pltpu.VMEM((2, n_keys, n_q), BF16)]
    grid_spec = pltpu.PrefetchScalarGridSpec(
        num_scalar_prefetch=1 if has_sink else 0,
        grid=(b, n_groups),
        in_specs=in_specs,
        out_specs=out_spec,
        scratch_shapes=scratch,
    )
    args = ((sinks,) if has_sink else ()) + (proj, proj, proj, bias)
    return pl.pallas_call(
        kernel,
        out_shape=jax.ShapeDtypeStruct((b, s, n_groups * qw), BF16),
        grid_spec=grid_spec,
        compiler_params=pltpu.CompilerParams(
            dimension_semantics=("arbitrary", "arbitrary"), vmem_limit_bytes=VMEM_LIMIT),
        name="attn_a" if dup else "attn_b",
    )(*args)


def _band_tables(rel_bias_b):
    qi = jnp.arange(QBLK)[:, None]
    r = qi // CHUNK

    def table(nw, n_prev, fn):
        kj = jnp.arange(nw * QBLK)[None, :]
        rel = (nw - 1) * QBLK + qi - kj
        inband = (kj >= r * CHUNK) & (kj < (r + n_prev + 1) * CHUNK)
        return jnp.where(inband[None], fn(rel), NEG_INF)

    slopes = jnp.exp2(-8.0 * jnp.arange(1, N_HEADS_A + 1, dtype=F32) / N_HEADS_A)
    nw_a = (N_PREV_A * CHUNK) // QBLK + 1
    nw_b = (N_PREV_B * CHUNK) // QBLK + 1
    bias_a = table(nw_a, N_PREV_A, lambda rel: -slopes[:, None, None] * jnp.abs(rel).astype(F32)[None])
    w_b = nw_b * QBLK
    row = -(-(QBLK + w_b - 2) // LANES) * LANES
    p = row + 1
    t = np.arange(p)
    m = np.where(t < w_b, t, t - p)
    dist = np.clip((nw_b - 1) * QBLK - m, -REL_CLIP, REL_CLIP) + REL_CLIP
    vec = rel_bias_b[:, dist].astype(F32)
    reps = -(-(QBLK * row) // p)
    rel_b = jnp.tile(vec, (1, reps))[:, :QBLK * row].reshape(N_HEADS_B, QBLK, row)[:, :, :w_b]
    bias_b = table(nw_b, N_PREV_B, lambda rel: rel_b)
    rep = N_HEADS_A // N_KV_A
    bias_a = bias_a.reshape(N_KV_A, rep * QBLK, nw_a * QBLK).transpose(0, 2, 1)
    bias_b = bias_b.reshape(N_HEADS_B // 2, 2 * QBLK, nw_b * QBLK).transpose(0, 2, 1)
    return bias_a * LOG2E, nw_a, bias_b * LOG2E, nw_b


def _mix_kernel(oa_ref, ob_ref, x_ref, mod_ref, ga_ref, gb_ref, gf_ref, wo_ref, wr_ref, br_ref,
                x1_ref, h2_ref, rt_ref, ids_ref):
    na = _rms(oa_ref[0].astype(F32), ga_ref[...]).astype(BF16)
    nb = _rms(ob_ref[0].astype(F32), gb_ref[...]).astype(BF16)
    acc = jnp.dot(jnp.concatenate([na, nb], axis=1), wo_ref[...], preferred_element_type=F32)
    x1 = x_ref[0] + mod_ref[0, 2:3, :] * acc
    x1_ref[0] = x1
    h2 = _rms(x1, gf_ref[...]) * (1.0 + mod_ref[0, 4:5, :]) + mod_ref[0, 3:4, :]
    h2_ref[0] = h2

    tm = h2.shape[0]
    h_hi = h2.astype(BF16)
    h_lo = (h2 - h_hi.astype(F32)).astype(BF16)
    r = jnp.dot(jnp.concatenate([h_hi, h_lo], axis=0), wr_ref[...], preferred_element_type=F32)
    r = r[:tm] + r[tm:]
    logits = r + pltpu.roll(r, LANES - ROUTER_LO_LANE, 1) + br_ref[...]

    lane = lax.broadcasted_iota(jnp.int32, (tm, LANES), 1)
    lane_f = lane.astype(F32)
    big = float(LANES)
    ninf = -jnp.inf

    def first_max(vals):
        top = jnp.max(vals, axis=-1, keepdims=True)
        idx = jnp.min(jnp.where(vals == top, lane_f, big), axis=-1, keepdims=True)
        return top, idx

    is_g = lane < N_GROUPS
    g_top, g_idx = first_max(jnp.where(is_g, logits, ninf))
    p_g = 1.0 / jnp.sum(jnp.where(is_g, jnp.exp(logits - g_top), 0.0), axis=-1, keepdims=True)
    lo = N_GROUPS + g_idx * EXPERTS_PER_GROUP
    e_vals = jnp.where((lane_f >= lo) & (lane_f < lo + EXPERTS_PER_GROUP), logits, ninf)
    v1, i1 = first_max(e_vals)
    v2, i2 = first_max(jnp.where(lane_f == i1, ninf, e_vals))
    e2 = jnp.exp(v2 - v1)
    w1 = p_g / (1.0 + e2)
    w2 = p_g * e2 / (1.0 + e2)
    rt = jnp.where(lane == 0, i1 - N_GROUPS,
                   jnp.where(lane == 1, i2 - N_GROUPS,
                             jnp.where(lane == 2, w1, jnp.where(lane == 3, w2, 0.0))))
    rt_ref[...] = rt
    pick = (lax.broadcasted_iota(jnp.int32, (SUBLANES, LANES), 0)
            == lax.broadcasted_iota(jnp.int32, (SUBLANES, LANES), 1)).astype(BF16)
    ids_ref[...] = lax.dot_general(pick, rt.astype(BF16), (((1,), (1,)), ((), ())),
                                   preferred_element_type=F32)


def _mix(o_a, o_b, x, mod, g_out_a, g_out_b, g_ffn, w_out_bf16, w_router, b_router):
    b, s, d = x.shape
    half = o_a.shape[2]
    tm = MIX_TM
    nt = s // tm
    vec = lambda n: pl.BlockSpec((1, n), lambda bi, i: (0, 0))
    return pl.pallas_call(
        _mix_kernel,
        out_shape=(jax.ShapeDtypeStruct((b, s, d), F32),
                   jax.ShapeDtypeStruct((b, s, d), F32),
                   jax.ShapeDtypeStruct((b * s, LANES), F32),
                   jax.ShapeDtypeStruct((SUBLANES, b * s), F32)),
        grid=(b, nt),
        in_specs=[
            pl.BlockSpec((1, tm, half), lambda bi, i: (bi, i, 0)),
            pl.BlockSpec((1, tm, half), lambda bi, i: (bi, i, 0)),
            pl.BlockSpec((1, tm, d), lambda bi, i: (bi, i, 0)),
            pl.BlockSpec((1, 6, d), lambda bi, i: (bi, 0, 0)),
            vec(half), vec(half), vec(d),
            pl.BlockSpec((d, d), lambda bi, i: (0, 0), pipeline_mode=pl.Buffered(1)),
            pl.BlockSpec((d, LANES), lambda bi, i: (0, 0)),
            vec(LANES),
        ],
        out_specs=(pl.BlockSpec((1, tm, d), lambda bi, i: (bi, i, 0)),
                   pl.BlockSpec((1, tm, d), lambda bi, i: (bi, i, 0)),
                   pl.BlockSpec((tm, LANES), lambda bi, i: (bi * nt + i, 0)),
                   pl.BlockSpec((SUBLANES, tm), lambda bi, i: (0, bi * nt + i))),
        compiler_params=pltpu.CompilerParams(
            dimension_semantics=("arbitrary", "arbitrary"), vmem_limit_bytes=VMEM_LIMIT),
        name="mix_out",
    )(o_a, o_b, x, mod, g_out_a.reshape(1, half), g_out_b.reshape(1, half), g_ffn.reshape(1, d),
      w_out_bf16, w_router, b_router.reshape(1, LANES))


DISPATCH_STEPS_MIN = ROW_BUFS


def _dispatch_kernel(slot_ref, zf_ref, h_ref, xs_hbm, buf, zbuf, sem, zsem):
    i = pl.program_id(0)
    n_steps = pl.num_programs(0)
    td = h_ref.shape[0]
    tm = zbuf.shape[0]
    n_tiles_max = xs_hbm.shape[0] // tm

    @pl.when(i == 0)
    def _():
        zbuf[...] = jnp.zeros(zbuf.shape, zbuf.dtype)

        def fill_copy(t):
            z0 = pl.multiple_of(zf_ref[t], SUBLANES)
            n = pl.multiple_of(tm - z0, SUBLANES)
            row0 = pl.multiple_of(t * tm + z0, SUBLANES)
            return pltpu.make_async_copy(zbuf.at[pl.ds(0, n)], xs_hbm.at[pl.ds(row0, n)], zsem.at[0])

        def fill(t, carry):
            @pl.when(zf_ref[t] < tm)
            def _():
                fill_copy(t).start()
            return carry
        lax.fori_loop(0, n_tiles_max, fill, 0)

        def drain(t, carry):
            @pl.when(zf_ref[t] < tm)
            def _():
                fill_copy(t).wait()
            return carry
        lax.fori_loop(0, n_tiles_max, drain, 0)

    def wait_rows(s):
        for _ in range(2):
            pltpu.make_async_copy(buf.at[s], xs_hbm.at[pl.ds(0, td)], sem.at[s]).wait()

    def step(s):
        @pl.when(i >= ROW_BUFS)
        def _():
            wait_rows(s)

        buf[s] = h_ref[...]
        for r in range(td):
            for k in range(2):
                dst = slot_ref[k * (n_steps * td) + i * td + r]
                pltpu.make_async_copy(buf.at[s, pl.ds(r, 1)], xs_hbm.at[pl.ds(dst, 1)],
                                      sem.at[s]).start(priority=k)

        @pl.when(i == n_steps - 1)
        def _():
            for t in range(min(ROW_BUFS, DISPATCH_STEPS_MIN)):
                wait_rows((s + ROW_BUFS - t) % ROW_BUFS)

    for s in range(ROW_BUFS):
        pl.when(i % ROW_BUFS == s)(functools.partial(step, s))


def _dispatch(h2, slot, zero_fill, n_pad):
    t, d = h2.shape
    tm = MOE_TM
    assert t // DISPATCH_TM >= DISPATCH_STEPS_MIN
    grid_spec = pltpu.PrefetchScalarGridSpec(
        num_scalar_prefetch=2,
        grid=(t // DISPATCH_TM,),
        in_specs=[pl.BlockSpec((DISPATCH_TM, d), lambda i, *_: (i, 0))],
        out_specs=pl.BlockSpec(memory_space=pl.ANY),
        scratch_shapes=[
            pltpu.VMEM((ROW_BUFS, DISPATCH_TM, d), F32),
            pltpu.VMEM((tm, d), F32),
            pltpu.SemaphoreType.DMA((ROW_BUFS,)),
            pltpu.SemaphoreType.DMA((1,)),
        ],
    )
    return pl.pallas_call(
        _dispatch_kernel,
        out_shape=jax.ShapeDtypeStruct((n_pad, d), F32),
        grid_spec=grid_spec,
        compiler_params=pltpu.CompilerParams(
            dimension_semantics=("arbitrary",), vmem_limit_bytes=VMEM_LIMIT),
        name="dispatch",
    )(slot, zero_fill, h2)


def _moe_kernel(te_ref, nxe_ref, nt_ref, x_ref, wg_hbm, wu_hbm, wd_hbm, o_ref,
                wg32, wu32, wd32, wgb, wub, wdb, wsem):
    i = pl.program_id(0)
    n_tiles = nt_ref[0]

    def weight_copies(e):
        return [pltpu.make_async_copy(src.at[e], dst, wsem.at[0])
                for src, dst in ((wg_hbm, wg32), (wu_hbm, wu32), (wd_hbm, wd32))]

    @pl.when(i == 0)
    def _():
        for cp in weight_copies(te_ref[0]):
            cp.start(priority=1)

    @pl.when(i < n_tiles)
    def _():
        @pl.when(jnp.logical_or(i == 0, te_ref[i] != te_ref[jnp.maximum(i - 1, 0)]))
        def _():
            for cp in weight_copies(0):
                cp.wait()
            wgb[...] = wg32[...].astype(BF16)
            wub[...] = wu32[...].astype(BF16)
            wdb[...] = wd32[...].astype(BF16)

            @pl.when(nxe_ref[i] >= 0)
            def _():
                for cp in weight_copies(nxe_ref[i]):
                    cp.start(priority=1)

        xb = x_ref[...].astype(BF16)
        g = jnp.dot(xb, wgb[...], preferred_element_type=F32)
        u = jnp.dot(xb, wub[...], preferred_element_type=F32)
        a = (g * jax.nn.sigmoid(g) * u).astype(BF16)
        o_ref[...] = jnp.dot(a, wdb[...], preferred_element_type=F32)

    @pl.when(i >= n_tiles)
    def _():
        o_ref[...] = jnp.zeros(o_ref.shape, o_ref.dtype)


def _moe(xs, w_gate, w_up, w_down, tile_expert, next_expert, n_tiles):
    n_pad, d = xs.shape
    tm = MOE_TM
    max_tiles = n_pad // tm
    de = w_gate.shape[2]
    grid_spec = pltpu.PrefetchScalarGridSpec(
        num_scalar_prefetch=3,
        grid=(max_tiles,),
        in_specs=[pl.BlockSpec((tm, d), lambda i, te, nxe, nt: (jnp.minimum(i, nt[0] - 1), 0))]
        + [pl.BlockSpec(memory_space=pl.ANY)] * 3,
        out_specs=pl.BlockSpec((tm, d), lambda i, *_: (i, 0)),
        scratch_shapes=[
            pltpu.VMEM((d, de), F32),
            pltpu.VMEM((d, de), F32),
            pltpu.VMEM((de, d), F32),
            pltpu.VMEM((d, de), BF16),
            pltpu.VMEM((d, de), BF16),
            pltpu.VMEM((de, d), BF16),
            pltpu.SemaphoreType.DMA((1,)),
        ],
    )
    return pl.pallas_call(
        _moe_kernel,
        out_shape=jax.ShapeDtypeStruct((n_pad, d), F32),
        grid_spec=grid_spec,
        compiler_params=pltpu.CompilerParams(
            dimension_semantics=("arbitrary",), vmem_limit_bytes=VMEM_LIMIT),
        name="moe",
    )(tile_expert, next_expert, n_tiles, xs, w_gate, w_up, w_down)


def _route_plan(ids, n_tok):
    tm = MOE_TM
    n_asg = 2 * n_tok
    n_pad = n_asg + N_EXPERTS * tm
    max_tiles = n_pad // tm
    e_flat = ids[:2].astype(jnp.int32).reshape(-1)
    onehot = e_flat[None, :] == jnp.arange(N_EXPERTS)[:, None]
    blk = LANES
    oh3 = onehot.reshape(N_EXPERTS, n_asg // blk, blk).astype(BF16)
    upper = (jnp.arange(blk)[:, None] <= jnp.arange(blk)[None, :]).astype(BF16)
    within = jnp.einsum("ebj,jk->ebk", oh3, upper, preferred_element_type=F32).astype(jnp.int32)
    blk_tot = within[:, :, -1]
    offs = jnp.cumsum(blk_tot, axis=1) - blk_tot
    csum = (within + offs[:, :, None]).reshape(N_EXPERTS, n_asg)
    counts = jnp.sum(blk_tot, axis=1)
    oh_i = onehot.astype(jnp.int32)
    rank = jnp.sum(csum * oh_i, axis=0) - 1
    tiles_e = (counts + tm - 1) // tm
    tile_end = jnp.cumsum(tiles_e)
    base = (tile_end - tiles_e) * tm
    slot = jnp.sum(oh_i * base[:, None], axis=0) + rank
    n_tiles = tile_end[-1]
    tile_id = jnp.arange(max_tiles)
    te = jnp.sum((tile_id[:, None] >= tile_end[None, :]).astype(jnp.int32), axis=1)
    te_last = jnp.sum(((n_tiles - 1) >= tile_end).astype(jnp.int32))
    tile_expert = jnp.where(tile_id < n_tiles, te, te_last).astype(jnp.int32)
    ex = jnp.arange(N_EXPERTS)
    later = (ex[None, :] > ex[:, None]) & (counts[None, :] > 0)
    nxt_e = jnp.min(jnp.where(later, ex[None, :], N_EXPERTS), axis=1)
    nxt_e = jnp.where(nxt_e < N_EXPERTS, nxt_e, -1)
    sel = (tile_expert[:, None] == ex[None, :]).astype(jnp.int32)
    next_expert = jnp.sum(sel * nxt_e[None, :], axis=1).astype(jnp.int32)
    last_tile = jnp.sum(sel * (tile_end - 1)[None, :], axis=1)
    used_last = jnp.sum(sel * (counts - (tiles_e - 1) * tm)[None, :], axis=1)
    zero_fill = jnp.where(tile_id >= n_tiles, 0,
                          jnp.where(tile_id == last_tile, (used_last // SUBLANES) * SUBLANES,
                                    tm)).astype(jnp.int32)
    return (tile_expert, next_expert, n_tiles.reshape(1).astype(jnp.int32), zero_fill,
            slot.astype(jnp.int32), n_pad)


def _final_kernel(slot_ref, x1_ref, rt_ref, mod_ref, modf_ref, g_ref, y_hbm, o_ref, ybuf, gsem):
    i = pl.program_id(0)
    n_steps = pl.num_programs(0)
    tm = x1_ref.shape[0]

    def row_in(tile, r, k, s):
        idx = slot_ref[k * (n_steps * tm) + tile * tm + r]
        pltpu.make_async_copy(y_hbm.at[pl.ds(idx, 1)], ybuf.at[s, pl.ds(k * tm + r, 1)],
                              gsem.at[s]).start(priority=k)

    def wait_in(s):
        pltpu.make_async_copy(y_hbm.at[pl.ds(0, 2 * tm)], ybuf.at[s], gsem.at[s]).wait()

    @pl.when(i == 0)
    def _():
        for t in range(ROW_BUFS - 1):
            tile = jnp.minimum(t, n_steps - 1)

            def row(r, carry):
                row_in(tile, r, 0, t)
                row_in(tile, r, 1, t)
                return carry
            lax.fori_loop(0, tm, row, 0, unroll=8)

    def step(s):
        wait_in(s)
        rt = rt_ref[...]
        y = rt[:, 2:3] * ybuf[s, :tm, :] + rt[:, 3:4] * ybuf[s, tm:, :]
        nxt = jnp.minimum(i + ROW_BUFS - 1, n_steps - 1)
        for r in range(tm):
            row_in(nxt, r, 0, (s + ROW_BUFS - 1) % ROW_BUFS)
            row_in(nxt, r, 1, (s + ROW_BUFS - 1) % ROW_BUFS)
        x2 = x1_ref[...] + mod_ref[0, 5:6, :] * y
        o_ref[...] = _rms(x2, g_ref[...]) * (1.0 + modf_ref[0, 1:2, :]) + modf_ref[0, 0:1, :]

        @pl.when(i == n_steps - 1)
        def _():
            for t in range(1, ROW_BUFS):
                wait_in((s + t) % ROW_BUFS)

    for s in range(ROW_BUFS):
        pl.when(i % ROW_BUFS == s)(functools.partial(step, s))


def _final(x1, y_rows, slot, rt, mod, modf, g_final):
    b, s, d = x1.shape
    tm = FINAL_TM
    nt = s // tm
    grid_spec = pltpu.PrefetchScalarGridSpec(
        num_scalar_prefetch=1,
        grid=(b * nt,),
        in_specs=[
            pl.BlockSpec((tm, d), lambda i, *_: (i, 0)),
            pl.BlockSpec((tm, LANES), lambda i, *_: (i, 0)),
            pl.BlockSpec((1, 6, d), lambda i, *_: (i // nt, 0, 0)),
            pl.BlockSpec((1, 2, d), lambda i, *_: (i // nt, 0, 0)),
            pl.BlockSpec((1, d), lambda i, *_: (0, 0)),
            pl.BlockSpec(memory_space=pl.ANY),
        ],
        out_specs=pl.BlockSpec((tm, d), lambda i, *_: (i, 0)),
        scratch_shapes=[
            pltpu.VMEM((ROW_BUFS, 2 * tm, d), F32),
            pltpu.SemaphoreType.DMA((ROW_BUFS,)),
        ],
    )
    out = pl.pallas_call(
        _final_kernel,
        out_shape=jax.ShapeDtypeStruct((b * s, d), F32),
        grid_spec=grid_spec,
        compiler_params=pltpu.CompilerParams(
            dimension_semantics=("arbitrary",), vmem_limit_bytes=VMEM_LIMIT),
        name="final",
    )(slot, x1.reshape(b * s, d), rt, mod, modf, g_final.reshape(1, d), y_rows)
    return out.reshape(b, s, d)


def kernel(x, c, w_ada, b_ada, g_mix, w_in, sinks_a, rel_bias_b, g_out_a, g_out_b, w_out, g_ffn,
           w_router_group, b_router_group, w_router_expert, b_router_expert, w_gate, w_up, w_down,
           w_ada_final, b_ada_final, g_final):
    b, s, d = x.shape
    assert w_ada.shape[0] == 1, "one layer"
    n_tok = b * s

    c_act = jax.nn.silu(c)
    a_rep = jnp.broadcast_to(c_act[:, :, None], (b, d, LANES))
    mod = _ada(a_rep, w_ada[0], b_ada[0]).reshape(b, 6, d)
    modf = _ada(a_rep, w_ada_final, b_ada_final).reshape(b, 2, d)

    proj = _proj(x, mod, g_mix[0], w_in[0].astype(BF16))

    bias_a, nw_a, bias_b, nw_b = _band_tables(rel_bias_b[0])
    kv_a0 = DA_Q // LANES
    o_a = _attention(proj, bias_a, sinks_a[0].astype(F32), n_groups=N_KV_A, ncol=2, nw=nw_a,
                     q_col0=0, k_col0=kv_a0, v_col0=kv_a0 + DA_KV // LANES, kv_share=2, dup=True)
    qb0 = (DA_Q + 2 * DA_KV) // LANES
    o_b = _attention(proj, bias_b, None, n_groups=N_HEADS_B // 2, ncol=1, nw=nw_b,
                     q_col0=qb0, k_col0=qb0 + DB // LANES, v_col0=qb0 + 2 * DB // LANES,
                     kv_share=1, dup=False)

    n_r = N_GROUPS + N_EXPERTS
    assert n_r <= ROUTER_LO_LANE
    w_r = jnp.concatenate([w_router_group[0], w_router_expert[0]], axis=1)
    w_r_hi = w_r.astype(BF16)
    w_r_lo = (w_r - w_r_hi.astype(F32)).astype(BF16)
    w_router = (jnp.zeros((d, LANES), BF16).at[:, :n_r].set(w_r_hi)
                .at[:, ROUTER_LO_LANE:ROUTER_LO_LANE + n_r].set(w_r_lo))
    b_router = jnp.zeros((LANES,), F32).at[:n_r].set(
        jnp.concatenate([b_router_group[0], b_router_expert[0]]))
    x1, h2, rt, ids = _mix(o_a, o_b, x, mod, g_out_a[0], g_out_b[0], g_ffn[0], w_out[0].astype(BF16),
                      w_router, b_router)

    tile_expert, next_expert, n_tiles, zero_fill, slot, n_pad = _route_plan(ids, n_tok)
    xs = _dispatch(h2.reshape(n_tok, d), slot, zero_fill, n_pad)
    y = _moe(xs, w_gate[0], w_up[0], w_down[0], tile_expert, next_expert, n_tiles)

    return _final(x1, y, slot, rt, mod, modf, g_final)
```

```python
import functools

import jax
import jax.numpy as jnp
import numpy as np
from jax import lax
from jax.experimental import pallas as pl
from jax.experimental.pallas import tpu as pltpu

D_MODEL = 2048
CHUNK = 64
HEAD_DIM = 64
N_HEADS_A = 16
N_KV_A = 4
N_PREV_A = 2
N_HEADS_B = 16
N_PREV_B = 8
REL_CLIP = 128
DA_Q = N_HEADS_A * HEAD_DIM
DA_KV = N_KV_A * HEAD_DIM
DB = N_HEADS_B * HEAD_DIM
N_GROUPS = 4
EXPERTS_PER_GROUP = 8
N_EXPERTS = N_GROUPS * EXPERTS_PER_GROUP
EPS = 1e-6
NEG_INF = -1e30
LOG2E = 1.4426950408889634

LANES = 128
SUBLANES = 8
VMEM_LIMIT = 56 * 1024 * 1024

ADA_TN = 1024
ADA_JB = 8
PROJ_TM = 512
PROJ_N_CHUNK = 512
QBLK = 2 * CHUNK
KV_PREP_ROWS = 512
MIX_TM = 512
MOE_TM = 512
DISPATCH_TM = 512
FINAL_TM = 256
ROW_BUFS = 3
ROUTER_LO_LANE = 64

F32 = jnp.float32
BF16 = jnp.bfloat16


def _rms(x, g):
    return x * lax.rsqrt(jnp.mean(x * x, axis=-1, keepdims=True) + EPS) * g


def _ada_kernel(a_ref, w_ref, b_ref, o_ref):
    n_b, k, _ = a_ref.shape
    tn = w_ref.shape[1]
    sub = SUBLANES
    for jb in range(tn // (ADA_JB * LANES)):
        col0 = jb * ADA_JB * LANES

        def body(kc, accs):
            k0 = pl.multiple_of(kc * sub, sub)
            a_rows = [a_ref[b, pl.ds(k0, sub), :] for b in range(n_b)]
            out = []
            for j in range(ADA_JB):
                w = w_ref[pl.ds(k0, sub), col0 + j * LANES:col0 + (j + 1) * LANES]
                out.append([accs[j][b] + a_rows[b] * w for b in range(n_b)])
            return out

        zero = jnp.zeros((sub, LANES), F32)
        accs = lax.fori_loop(0, k // sub, body, [[zero] * n_b for _ in range(ADA_JB)], unroll=4)
        for j in range(ADA_JB):
            cols = slice(col0 + j * LANES, col0 + (j + 1) * LANES)
            for b in range(n_b):
                o_ref[b:b + 1, cols] = jnp.sum(accs[j][b], axis=0, keepdims=True) + b_ref[:, cols]


def _ada(a_rep, w, bias):
    n_b, k, _ = a_rep.shape
    n = w.shape[1]
    tn = ADA_TN
    return pl.pallas_call(
        _ada_kernel,
        out_shape=jax.ShapeDtypeStruct((n_b, n), F32),
        grid=(n // tn,),
        in_specs=[
            pl.BlockSpec((n_b, k, LANES), lambda j: (0, 0, 0)),
            pl.BlockSpec((k, tn), lambda j: (0, j)),
            pl.BlockSpec((1, tn), lambda j: (0, j)),
        ],
        out_specs=pl.BlockSpec((n_b, tn), lambda j: (0, j)),
        compiler_params=pltpu.CompilerParams(
            dimension_semantics=("arbitrary",), vmem_limit_bytes=VMEM_LIMIT),
        name="ada",
    )(a_rep, w, bias.reshape(1, n))


def _proj_kernel(x_ref, mod_ref, g_ref, w_ref, o_ref, *, n_chunk):
    h = _rms(x_ref[0], g_ref[...])
    h = h * (1.0 + mod_ref[0, 1:2, :]) + mod_ref[0, 0:1, :]
    hb = h.astype(BF16)
    for n0 in range(0, o_ref.shape[2], n_chunk):
        cols = slice(n0, n0 + n_chunk)
        o_ref[0, :, cols] = jnp.dot(hb, w_ref[:, cols], preferred_element_type=F32).astype(BF16)


def _proj(x, mod, g_mix, w_in_bf16):
    b, s, d = x.shape
    n = w_in_bf16.shape[1]
    tm = PROJ_TM
    return pl.pallas_call(
        functools.partial(_proj_kernel, n_chunk=PROJ_N_CHUNK),
        out_shape=jax.ShapeDtypeStruct((b, s, n), BF16),
        grid=(b, s // tm),
        in_specs=[
            pl.BlockSpec((1, tm, d), lambda bi, i: (bi, i, 0)),
            pl.BlockSpec((1, 6, d), lambda bi, i: (bi, 0, 0)),
            pl.BlockSpec((1, d), lambda bi, i: (0, 0)),
            pl.BlockSpec((d, n), lambda bi, i: (0, 0), pipeline_mode=pl.Buffered(1)),
        ],
        out_specs=pl.BlockSpec((1, tm, n), lambda bi, i: (bi, i, 0)),
        compiler_params=pltpu.CompilerParams(
            dimension_semantics=("arbitrary", "arbitrary"), vmem_limit_bytes=VMEM_LIMIT),
        name="proj",
    )(x, mod, g_mix.reshape(1, d), w_in_bf16)


def _attn_kernel(*refs, ncol, nw, dup, has_sink):
    if has_sink:
        sink_ref, q_ref, k_ref, v_ref, bias_ref, o_ref = refs[:6]
        scratch = refs[6:]
    else:
        q_ref, k_ref, v_ref, bias_ref, o_ref = refs[:5]
        scratch = refs[5:]
        sink_ref = None
    s_len = q_ref.shape[1]
    n_blk = s_len // QBLK
    n_stack = 2 * ncol
    grp = pl.program_id(1)

    rows = KV_PREP_ROWS
    lane = lax.broadcasted_iota(jnp.int32, (QBLK, LANES), 1)
    low = lane < HEAD_DIM
    if dup:
        kd_ref, vt_ref = scratch[:2]
        half = grp % 2
        keep = (lax.broadcasted_iota(jnp.int32, (rows, LANES), 1) // HEAD_DIM) == half

        def spread(t):
            return jnp.where(keep, t, pltpu.roll(t, HEAD_DIM, 1))

        def dup_body(c, carry):
            r0 = pl.multiple_of(c * rows, rows)
            kd_ref[pl.ds(r0, rows), :] = spread(k_ref[0, pl.ds(r0, rows), :].astype(F32)).astype(BF16)
            return carry

        lax.fori_loop(0, s_len // rows, dup_body, 0)
        k_src = kd_ref
    else:
        vt_ref = scratch[0]
        k_src = k_ref.at[0]

        def spread(t):
            return t

    for c in range(s_len // rows):
        t = spread(v_ref[0, c * rows:(c + 1) * rows, :].astype(F32))
        vt_ref[:, c * rows:(c + 1) * rows] = t.T.astype(BF16)

    top = lax.broadcasted_iota(jnp.int32, (LANES, QBLK), 0) < HEAD_DIM

    if has_sink:
        sink = jnp.concatenate(
            [jnp.full((1, QBLK), sink_ref[grp * n_stack + h] * LOG2E, F32) for h in range(n_stack)],
            axis=1)

    def rows_of(j):
        return j * QBLK if isinstance(j, int) else pl.multiple_of(j * QBLK, QBLK)

    def logits(j, nvb):
        qf = q_ref[0, pl.ds(rows_of(j), QBLK), :].astype(F32) * (HEAD_DIM ** -0.5 * LOG2E)
        parts = []
        for c in range(ncol):
            qc = qf[:, c * LANES:(c + 1) * LANES]
            parts.append(jnp.where(low, qc, 0.0))
            parts.append(jnp.where(low, 0.0, qc))
        lhs = jnp.concatenate(parts, axis=0).astype(BF16)
        kw = k_src[pl.ds(rows_of(j - (nvb - 1)), nvb * QBLK), :]
        s = lax.dot_general(kw, lhs, (((1,), (1,)), ((), ())), preferred_element_type=F32)
        return s + bias_ref[0, (nw - nvb) * QBLK:, :]

    def softmax(s):
        m = jnp.max(s, axis=0, keepdims=True)
        if has_sink:
            m = jnp.maximum(m, sink)
        p = jnp.exp2(s - m)
        denom = jnp.sum(p, axis=0, keepdims=True)
        if has_sink:
            denom = denom + jnp.exp2(sink - m)
        return p.astype(BF16), denom

    def emit(j, nvb, p, denom):
        vw = vt_ref[:, pl.ds(rows_of(j - (nvb - 1)), nvb * QBLK)]
        o = jnp.dot(vw, p, preferred_element_type=F32) / denom
        for c in range(ncol):
            o0 = o[:, (2 * c) * QBLK:(2 * c + 1) * QBLK]
            o1 = o[:, (2 * c + 1) * QBLK:(2 * c + 2) * QBLK]
            o_ref[0, pl.ds(rows_of(j), QBLK), c * LANES:(c + 1) * LANES] = (
                jnp.where(top, o0, o1).T.astype(BF16))

    s_scr, p_scr = scratch[-2:]
    last = n_blk - 1

    def n_valid(j):
        return min(j + 1, nw) if isinstance(j, int) else nw

    def put_logits(u, j):
        s_scr[u, :n_valid(j) * QBLK, :] = logits(j, n_valid(j))

    def put_probs(u, j):
        rows_n = n_valid(j) * QBLK
        p, den = softmax(s_scr[u, :rows_n, :])
        p_scr[u, :rows_n, :] = p
        return den

    def pair(j, d_even, clamp):
        emit(j, n_valid(j), p_scr[0, :n_valid(j) * QBLK, :], d_even)
        d_odd = put_probs(1, j + 1)
        put_logits(0, clamp(j + 2))
        emit(j + 1, n_valid(j + 1), p_scr[1, :n_valid(j + 1) * QBLK, :], d_odd)
        d_next = put_probs(0, clamp(j + 2))
        put_logits(1, clamp(j + 3))
        return d_next

    first = nw - 1 + (nw - 1) % 2
    assert n_blk % 2 == 0 and first + 4 <= n_blk
    put_logits(0, 0)
    den = put_probs(0, 0)
    put_logits(1, 1)
    for j in range(0, first, 2):
        den = pair(j, den, lambda k: k)

    lax.fori_loop(0, (n_blk - first) // 2,
                  lambda t, d: pair(first + 2 * t, d, lambda k: jnp.minimum(k, last)), den)


def _attention(proj, bias, sinks, *, n_groups, ncol, nw, q_col0, k_col0, v_col0, kv_share, dup):
    b, s, _ = proj.shape
    qw = ncol * LANES
    has_sink = sinks is not None
    kernel = functools.partial(_attn_kernel, ncol=ncol, nw=nw, dup=dup, has_sink=has_sink)
    _, n_keys, n_q = bias.shape
    in_specs = [
        pl.BlockSpec((1, s, qw), lambda bi, g, *_: (bi, 0, q_col0 // ncol + g)),
        pl.BlockSpec((1, s, LANES), lambda bi, g, *_: (bi, 0, k_col0 + g // kv_share)),
        pl.BlockSpec((1, s, LANES), lambda bi, g, *_: (bi, 0, v_col0 + g // kv_share)),
        pl.BlockSpec((1, n_keys, n_q), lambda bi, g, *_: (g, 0, 0)),
    ]
    out_spec = pl.BlockSpec((1, s, qw), lambda bi, g, *_: (bi, 0, g))
    scratch = [pltpu.VMEM((s, LANES), BF16)] if dup else []
    scratch += [pltpu.VMEM((LANES, s), BF16),
                pltpu.VMEM((2, n_keys, n_q), F32), pltpu.VMEM((2, n_keys, n_q), BF16)]
    grid_spec = pltpu.PrefetchScalarGridSpec(
        num_scalar_prefetch=1 if has_sink else 0,
        grid=(b, n_groups),
        in_specs=in_specs,
        out_specs=out_spec,
        scratch_shapes=scratch,
    )
    args = ((sinks,) if has_sink else ()) + (proj, proj, proj, bias)
    return pl.pallas_call(
        kernel,
        out_shape=jax.ShapeDtypeStruct((b, s, n_groups * qw), BF16),
        grid_spec=grid_spec,
        compiler_params=pltpu.CompilerParams(
            dimension_semantics=("arbitrary", "arbitrary"), vmem_limit_bytes=VMEM_LIMIT),
        name="attn_a" if dup else "attn_b",
    )(*args)


def _band_tables(rel_bias_b):
    qi = jnp.arange(QBLK)[:, None]
    r = qi // CHUNK

    def table(nw, n_prev, fn):
        kj = jnp.arange(nw * QBLK)[None, :]
        rel = (nw - 1) * QBLK + qi - kj
        inband = (kj >= r * CHUNK) & (kj < (r + n_prev + 1) * CHUNK)
        return jnp.where(inband[None], fn(rel), NEG_INF)

    slopes = jnp.exp2(-8.0 * jnp.arange(1, N_HEADS_A + 1, dtype=F32) / N_HEADS_A)
    nw_a = (N_PREV_A * CHUNK) // QBLK + 1
    nw_b = (N_PREV_B * CHUNK) // QBLK + 1
    bias_a = table(nw_a, N_PREV_A, lambda rel: -slopes[:, None, None] * jnp.abs(rel).astype(F32)[None])
    w_b = nw_b * QBLK
    row = -(-(QBLK + w_b - 2) // LANES) * LANES
    p = row + 1
    t = np.arange(p)
    m = np.where(t < w_b, t, t - p)
    dist = np.clip((nw_b - 1) * QBLK - m, -REL_CLIP, REL_CLIP) + REL_CLIP
    vec = rel_bias_b[:, dist].astype(F32)
    reps = -(-(QBLK * row) // p)
    rel_b = jnp.tile(vec, (1, reps))[:, :QBLK * row].reshape(N_HEADS_B, QBLK, row)[:, :, :w_b]
    bias_b = table(nw_b, N_PREV_B, lambda rel: rel_b)
    rep = N_HEADS_A // N_KV_A
    bias_a = bias_a.reshape(N_KV_A, rep * QBLK, nw_a * QBLK).transpose(0, 2, 1)
    bias_b = bias_b.reshape(N_HEADS_B // 2, 2 * QBLK, nw_b * QBLK).transpose(0, 2, 1)
    return bias_a * LOG2E, nw_a, bias_b * LOG2E, nw_b


def _mix_kernel(oa_ref, ob_ref, x_ref, mod_ref, ga_ref, gb_ref, gf_ref, wo_ref, wr_ref, br_ref,
                x1_ref, h2_ref, rt_ref, ids_ref):
    na = _rms(oa_ref[0].astype(F32), ga_ref[...]).astype(BF16)
    nb = _rms(ob_ref[0].astype(F32), gb_ref[...]).astype(BF16)
    acc = jnp.dot(jnp.concatenate([na, nb], axis=1), wo_ref[...], preferred_element_type=F32)
    x1 = x_ref[0] + mod_ref[0, 2:3, :] * acc
    x1_ref[0] = x1
    h2 = _rms(x1, gf_ref[...]) * (1.0 + mod_ref[0, 4:5, :]) + mod_ref[0, 3:4, :]
    h2_ref[0] = h2

    tm = h2.shape[0]
    h_hi = h2.astype(BF16)
    h_lo = (h2 - h_hi.astype(F32)).astype(BF16)
    r = jnp.dot(jnp.concatenate([h_hi, h_lo], axis=0), wr_ref[...], preferred_element_type=F32)
    r = r[:tm] + r[tm:]
    logits = r + pltpu.roll(r, LANES - ROUTER_LO_LANE, 1) + br_ref[...]

    lane = lax.broadcasted_iota(jnp.int32, (tm, LANES), 1)
    lane_f = lane.astype(F32)
    big = float(LANES)
    ninf = -jnp.inf

    def first_max(vals):
        top = jnp.max(vals, axis=-1, keepdims=True)
        idx = jnp.min(jnp.where(vals == top, lane_f, big), axis=-1, keepdims=True)
        return top, idx

    is_g = lane < N_GROUPS
    g_top, g_idx = first_max(jnp.where(is_g, logits, ninf))
    p_g = 1.0 / jnp.sum(jnp.where(is_g, jnp.exp(logits - g_top), 0.0), axis=-1, keepdims=True)
    lo = N_GROUPS + g_idx * EXPERTS_PER_GROUP
    e_vals = jnp.where((lane_f >= lo) & (lane_f < lo + EXPERTS_PER_GROUP), logits, ninf)
    v1, i1 = first_max(e_vals)
    v2, i2 = first_max(jnp.where(lane_f == i1, ninf, e_vals))
    e2 = jnp.exp(v2 - v1)
    w1 = p_g / (1.0 + e2)
    w2 = p_g * e2 / (1.0 + e2)
    rt = jnp.where(lane == 0, i1 - N_GROUPS,
                   jnp.where(lane == 1, i2 - N_GROUPS,
                             jnp.where(lane == 2, w1, jnp.where(lane == 3, w2, 0.0))))
    rt_ref[...] = rt
    pick = (lax.broadcasted_iota(jnp.int32, (SUBLANES, LANES), 0)
            == lax.broadcasted_iota(jnp.int32, (SUBLANES, LANES), 1)).astype(BF16)
    ids_ref[...] = lax.dot_general(pick, rt.astype(BF16), (((1,), (1,)), ((), ())),
                                   preferred_element_type=F32)


def _mix(o_a, o_b, x, mod, g_out_a, g_out_b, g_ffn, w_out_bf16, w_router, b_router):
    b, s, d = x.shape
    half = o_a.shape[2]
    tm = MIX_TM
    nt = s // tm
    vec = lambda n: pl.BlockSpec((1, n), lambda bi, i: (0, 0))
    return pl.pallas_call(
        _mix_kernel,
        out_shape=(jax.ShapeDtypeStruct((b, s, d), F32),
                   jax.ShapeDtypeStruct((b, s, d), F32),
                   jax.ShapeDtypeStruct((b * s, LANES), F32),
                   jax.ShapeDtypeStruct((SUBLANES, b * s), F32)),
        grid=(b, nt),
        in_specs=[
            pl.BlockSpec((1, tm, half), lambda bi, i: (bi, i, 0)),
            pl.BlockSpec((1, tm, half), lambda bi, i: (bi, i, 0)),
            pl.BlockSpec((1, tm, d), lambda bi, i: (bi, i, 0)),
            pl.BlockSpec((1, 6, d), lambda bi, i: (bi, 0, 0)),
            vec(half), vec(half), vec(d),
            pl.BlockSpec((d, d), lambda bi, i: (0, 0), pipeline_mode=pl.Buffered(1)),
            pl.BlockSpec((d, LANES), lambda bi, i: (0, 0)),
            vec(LANES),
        ],
        out_specs=(pl.BlockSpec((1, tm, d), lambda bi, i: (bi, i, 0)),
                   pl.BlockSpec((1, tm, d), lambda bi, i: (bi, i, 0)),
                   pl.BlockSpec((tm, LANES), lambda bi, i: (bi * nt + i, 0)),
                   pl.BlockSpec((SUBLANES, tm), lambda bi, i: (0, bi * nt + i))),
        compiler_params=pltpu.CompilerParams(
            dimension_semantics=("arbitrary", "arbitrary"), vmem_limit_bytes=VMEM_LIMIT),
        name="mix_out",
    )(o_a, o_b, x, mod, g_out_a.reshape(1, half), g_out_b.reshape(1, half), g_ffn.reshape(1, d),
      w_out_bf16, w_router, b_router.reshape(1, LANES))


DISPATCH_STEPS_MIN = ROW_BUFS


def _dispatch_kernel(slot_ref, zf_ref, h_ref, a_ref, wf_ref, bf_ref, xs_hbm, modf_ref,
                     buf, zbuf, sem, zsem):
    i = pl.program_id(0)
    n_steps = pl.num_programs(0)
    td = h_ref.shape[0]
    tm = zbuf.shape[0]
    n_tiles_max = xs_hbm.shape[0] // tm

    @pl.when(i == 0)
    def _():
        zbuf[...] = jnp.zeros(zbuf.shape, zbuf.dtype)

        def fill_copy(t):
            z0 = pl.multiple_of(zf_ref[t], SUBLANES)
            n = pl.multiple_of(tm - z0, SUBLANES)
            row0 = pl.multiple_of(t * tm + z0, SUBLANES)
            return pltpu.make_async_copy(zbuf.at[pl.ds(0, n)], xs_hbm.at[pl.ds(row0, n)], zsem.at[0])

        def fill(t, carry):
            @pl.when(zf_ref[t] < tm)
            def _():
                fill_copy(t).start()
            return carry
        lax.fori_loop(0, n_tiles_max, fill, 0)

        def drain(t, carry):
            @pl.when(zf_ref[t] < tm)
            def _():
                fill_copy(t).wait()
            return carry
        lax.fori_loop(0, n_tiles_max, drain, 0)

    def wait_rows(s):
        for _ in range(2):
            pltpu.make_async_copy(buf.at[s], xs_hbm.at[pl.ds(0, td)], sem.at[s]).wait()

    def step(s):
        @pl.when(i >= ROW_BUFS)
        def _():
            wait_rows(s)

        buf[s] = h_ref[...]
        for r in range(td):
            for k in range(2):
                dst = slot_ref[k * (n_steps * td) + i * td + r]
                pltpu.make_async_copy(buf.at[s, pl.ds(r, 1)], xs_hbm.at[pl.ds(dst, 1)],
                                      sem.at[s]).start(priority=k)

        for j in range(wf_ref.shape[1] // LANES):
            cols = slice(j * LANES, (j + 1) * LANES)
            w = wf_ref[:, cols]
            for b in range(a_ref.shape[0]):
                modf_ref[b:b + 1, cols] = jnp.sum(a_ref[b] * w, axis=0, keepdims=True) + bf_ref[:, cols]

        @pl.when(i == n_steps - 1)
        def _():
            for t in range(min(ROW_BUFS, DISPATCH_STEPS_MIN)):
                wait_rows((s + ROW_BUFS - t) % ROW_BUFS)

    for s in range(ROW_BUFS):
        pl.when(i % ROW_BUFS == s)(functools.partial(step, s))


def _dispatch(h2, slot, zero_fill, n_pad, a_rep, w_final, b_final):
    t, d = h2.shape
    tm = MOE_TM
    n_steps = t // DISPATCH_TM
    assert n_steps >= DISPATCH_STEPS_MIN
    n_b, k, _ = a_rep.shape
    n_f = w_final.shape[1]
    tn_f = n_f // n_steps
    assert tn_f * n_steps == n_f and tn_f % LANES == 0
    grid_spec = pltpu.PrefetchScalarGridSpec(
        num_scalar_prefetch=2,
        grid=(n_steps,),
        in_specs=[
            pl.BlockSpec((DISPATCH_TM, d), lambda i, *_: (i, 0)),
            pl.BlockSpec((n_b, k, LANES), lambda i, *_: (0, 0, 0)),
            pl.BlockSpec((k, tn_f), lambda i, *_: (0, i)),
            pl.BlockSpec((1, tn_f), lambda i, *_: (0, i)),
        ],
        out_specs=(pl.BlockSpec(memory_space=pl.ANY),
                   pl.BlockSpec((n_b, tn_f), lambda i, *_: (0, i))),
        scratch_shapes=[
            pltpu.VMEM((ROW_BUFS, DISPATCH_TM, d), F32),
            pltpu.VMEM((tm, d), F32),
            pltpu.SemaphoreType.DMA((ROW_BUFS,)),
            pltpu.SemaphoreType.DMA((1,)),
        ],
    )
    return pl.pallas_call(
        _dispatch_kernel,
        out_shape=(jax.ShapeDtypeStruct((n_pad, d), F32), jax.ShapeDtypeStruct((n_b, n_f), F32)),
        grid_spec=grid_spec,
        compiler_params=pltpu.CompilerParams(
            dimension_semantics=("arbitrary",), vmem_limit_bytes=VMEM_LIMIT),
        name="dispatch",
    )(slot, zero_fill, h2, a_rep, w_final, b_final.reshape(1, n_f))


def _moe_kernel(te_ref, nxe_ref, nt_ref, x_ref, wg_hbm, wu_hbm, wd_hbm, o_ref,
                wg32, wu32, wd32, wgb, wub, wdb, wsem):
    i = pl.program_id(0)
    n_tiles = nt_ref[0]

    def weight_copies(e):
        return [pltpu.make_async_copy(src.at[e], dst, wsem.at[0])
                for src, dst in ((wg_hbm, wg32), (wu_hbm, wu32), (wd_hbm, wd32))]

    @pl.when(i == 0)
    def _():
        for cp in weight_copies(te_ref[0]):
            cp.start(priority=1)

    @pl.when(i < n_tiles)
    def _():
        @pl.when(jnp.logical_or(i == 0, te_ref[i] != te_ref[jnp.maximum(i - 1, 0)]))
        def _():
            for cp in weight_copies(0):
                cp.wait()
            wgb[...] = wg32[...].astype(BF16)
            wub[...] = wu32[...].astype(BF16)
            wdb[...] = wd32[...].astype(BF16)

            @pl.when(nxe_ref[i] >= 0)
            def _():
                for cp in weight_copies(nxe_ref[i]):
                    cp.start(priority=1)

        xb = x_ref[...].astype(BF16)
        g = jnp.dot(xb, wgb[...], preferred_element_type=F32)
        u = jnp.dot(xb, wub[...], preferred_element_type=F32)
        a = (g * jax.nn.sigmoid(g) * u).astype(BF16)
        o_ref[...] = jnp.dot(a, wdb[...], preferred_element_type=F32)

    @pl.when(i >= n_tiles)
    def _():
        o_ref[...] = jnp.zeros(o_ref.shape, o_ref.dtype)


def _moe(xs, w_gate, w_up, w_down, tile_expert, next_expert, n_tiles):
    n_pad, d = xs.shape
    tm = MOE_TM
    max_tiles = n_pad // tm
    de = w_gate.shape[2]
    grid_spec = pltpu.PrefetchScalarGridSpec(
        num_scalar_prefetch=3,
        grid=(max_tiles,),
        in_specs=[pl.BlockSpec((tm, d), lambda i, te, nxe, nt: (jnp.minimum(i, nt[0] - 1), 0))]
        + [pl.BlockSpec(memory_space=pl.ANY)] * 3,
        out_specs=pl.BlockSpec((tm, d), lambda i, *_: (i, 0)),
        scratch_shapes=[
            pltpu.VMEM((d, de), F32),
            pltpu.VMEM((d, de), F32),
            pltpu.VMEM((de, d), F32),
            pltpu.VMEM((d, de), BF16),
            pltpu.VMEM((d, de), BF16),
            pltpu.VMEM((de, d), BF16),
            pltpu.SemaphoreType.DMA((1,)),
        ],
    )
    return pl.pallas_call(
        _moe_kernel,
        out_shape=jax.ShapeDtypeStruct((n_pad, d), F32),
        grid_spec=grid_spec,
        compiler_params=pltpu.CompilerParams(
            dimension_semantics=("arbitrary",), vmem_limit_bytes=VMEM_LIMIT),
        name="moe",
    )(tile_expert, next_expert, n_tiles, xs, w_gate, w_up, w_down)


def _route_plan(ids, n_tok):
    tm = MOE_TM
    n_asg = 2 * n_tok
    n_pad = n_asg + N_EXPERTS * tm
    max_tiles = n_pad // tm
    e_flat = ids[:2].astype(jnp.int32).reshape(-1)
    onehot = e_flat[None, :] == jnp.arange(N_EXPERTS)[:, None]
    blk = LANES
    oh3 = onehot.reshape(N_EXPERTS, n_asg // blk, blk).astype(BF16)
    upper = (jnp.arange(blk)[:, None] <= jnp.arange(blk)[None, :]).astype(BF16)
    within = jnp.einsum("ebj,jk->ebk", oh3, upper, preferred_element_type=F32).astype(jnp.int32)
    blk_tot = within[:, :, -1]
    offs = jnp.cumsum(blk_tot, axis=1) - blk_tot
    csum = (within + offs[:, :, None]).reshape(N_EXPERTS, n_asg)
    counts = jnp.sum(blk_tot, axis=1)
    oh_i = onehot.astype(jnp.int32)
    rank = jnp.sum(csum * oh_i, axis=0) - 1
    tiles_e = (counts + tm - 1) // tm
    tile_end = jnp.cumsum(tiles_e)
    base = (tile_end - tiles_e) * tm
    slot = jnp.sum(oh_i * base[:, None], axis=0) + rank
    n_tiles = tile_end[-1]
    tile_id = jnp.arange(max_tiles)
    te = jnp.sum((tile_id[:, None] >= tile_end[None, :]).astype(jnp.int32), axis=1)
    te_last = jnp.sum(((n_tiles - 1) >= tile_end).astype(jnp.int32))
    tile_expert = jnp.where(tile_id < n_tiles, te, te_last).astype(jnp.int32)
    ex = jnp.arange(N_EXPERTS)
    later = (ex[None, :] > ex[:, None]) & (counts[None, :] > 0)
    nxt_e = jnp.min(jnp.where(later, ex[None, :], N_EXPERTS), axis=1)
    nxt_e = jnp.where(nxt_e < N_EXPERTS, nxt_e, -1)
    sel = (tile_expert[:, None] == ex[None, :]).astype(jnp.int32)
    next_expert = jnp.sum(sel * nxt_e[None, :], axis=1).astype(jnp.int32)
    last_tile = jnp.sum(sel * (tile_end - 1)[None, :], axis=1)
    used_last = jnp.sum(sel * (counts - (tiles_e - 1) * tm)[None, :], axis=1)
    zero_fill = jnp.where(tile_id >= n_tiles, 0,
                          jnp.where(tile_id == last_tile, (used_last // SUBLANES) * SUBLANES,
                                    tm)).astype(jnp.int32)
    return (tile_expert, next_expert, n_tiles.reshape(1).astype(jnp.int32), zero_fill,
            slot.astype(jnp.int32), n_pad)


def _final_kernel(slot_ref, x1_ref, rt_ref, mod_ref, modf_ref, g_ref, y_hbm, o_ref, ybuf, gsem):
    i = pl.program_id(0)
    n_steps = pl.num_programs(0)
    tm = x1_ref.shape[0]

    def row_in(tile, r, k, s):
        idx = slot_ref[k * (n_steps * tm) + tile * tm + r]
        pltpu.make_async_copy(y_hbm.at[pl.ds(idx, 1)], ybuf.at[s, pl.ds(k * tm + r, 1)],
                              gsem.at[s]).start(priority=k)

    def wait_in(s):
        pltpu.make_async_copy(y_hbm.at[pl.ds(0, 2 * tm)], ybuf.at[s], gsem.at[s]).wait()

    @pl.when(i == 0)
    def _():
        for t in range(ROW_BUFS - 1):
            tile = jnp.minimum(t, n_steps - 1)

            def row(r, carry):
                row_in(tile, r, 0, t)
                row_in(tile, r, 1, t)
                return carry
            lax.fori_loop(0, tm, row, 0, unroll=8)

    def step(s):
        wait_in(s)
        rt = rt_ref[...]
        y = rt[:, 2:3] * ybuf[s, :tm, :] + rt[:, 3:4] * ybuf[s, tm:, :]
        nxt = jnp.minimum(i + ROW_BUFS - 1, n_steps - 1)
        for r in range(tm):
            row_in(nxt, r, 0, (s + ROW_BUFS - 1) % ROW_BUFS)
            row_in(nxt, r, 1, (s + ROW_BUFS - 1) % ROW_BUFS)
        x2 = x1_ref[...] + mod_ref[0, 5:6, :] * y
        o_ref[...] = _rms(x2, g_ref[...]) * (1.0 + modf_ref[0, 1:2, :]) + modf_ref[0, 0:1, :]

        @pl.when(i == n_steps - 1)
        def _():
            for t in range(1, ROW_BUFS):
                wait_in((s + t) % ROW_BUFS)

    for s in range(ROW_BUFS):
        pl.when(i % ROW_BUFS == s)(functools.partial(step, s))


def _final(x1, y_rows, slot, rt, mod, modf, g_final):
    b, s, d = x1.shape
    tm = FINAL_TM
    nt = s // tm
    grid_spec = pltpu.PrefetchScalarGridSpec(
        num_scalar_prefetch=1,
        grid=(b * nt,),
        in_specs=[
            pl.BlockSpec((tm, d), lambda i, *_: (i, 0)),
            pl.BlockSpec((tm, LANES), lambda i, *_: (i, 0)),
            pl.BlockSpec((1, 6, d), lambda i, *_: (i // nt, 0, 0)),
            pl.BlockSpec((1, 2, d), lambda i, *_: (i // nt, 0, 0)),
            pl.BlockSpec((1, d), lambda i, *_: (0, 0)),
            pl.BlockSpec(memory_space=pl.ANY),
        ],
        out_specs=pl.BlockSpec((tm, d), lambda i, *_: (i, 0)),
        scratch_shapes=[
            pltpu.VMEM((ROW_BUFS, 2 * tm, d), F32),
            pltpu.SemaphoreType.DMA((ROW_BUFS,)),
        ],
    )
    out = pl.pallas_call(
        _final_kernel,
        out_shape=jax.ShapeDtypeStruct((b * s, d), F32),
        grid_spec=grid_spec,
        compiler_params=pltpu.CompilerParams(
            dimension_semantics=("arbitrary",), vmem_limit_bytes=VMEM_LIMIT),
        name="final",
    )(slot, x1.reshape(b * s, d), rt, mod, modf, g_final.reshape(1, d), y_rows)
    return out.reshape(b, s, d)


def kernel(x, c, w_ada, b_ada, g_mix, w_in, sinks_a, rel_bias_b, g_out_a, g_out_b, w_out, g_ffn,
           w_router_group, b_router_group, w_router_expert, b_router_expert, w_gate, w_up, w_down,
           w_ada_final, b_ada_final, g_final):
    b, s, d = x.shape
    assert w_ada.shape[0] == 1, "one layer"
    n_tok = b * s

    c_act = jax.nn.silu(c)
    a_rep = jnp.broadcast_to(c_act[:, :, None], (b, d, LANES))
    mod = _ada(a_rep, w_ada[0], b_ada[0]).reshape(b, 6, d)

    proj = _proj(x, mod, g_mix[0], w_in[0].astype(BF16))

    bias_a, nw_a, bias_b, nw_b = _band_tables(rel_bias_b[0])
    kv_a0 = DA_Q // LANES
    o_a = _attention(proj, bias_a, sinks_a[0].astype(F32), n_groups=N_KV_A, ncol=2, nw=nw_a,
                     q_col0=0, k_col0=kv_a0, v_col0=kv_a0 + DA_KV // LANES, kv_share=2, dup=True)
    qb0 = (DA_Q + 2 * DA_KV) // LANES
    o_b = _attention(proj, bias_b, None, n_groups=N_HEADS_B // 2, ncol=1, nw=nw_b,
                     q_col0=qb0, k_col0=qb0 + DB // LANES, v_col0=qb0 + 2 * DB // LANES,
                     kv_share=1, dup=False)

    n_r = N_GROUPS + N_EXPERTS
    assert n_r <= ROUTER_LO_LANE
    w_r = jnp.concatenate([w_router_group[0], w_router_expert[0]], axis=1)
    w_r_hi = w_r.astype(BF16)
    w_r_lo = (w_r - w_r_hi.astype(F32)).astype(BF16)
    w_router = (jnp.zeros((d, LANES), BF16).at[:, :n_r].set(w_r_hi)
                .at[:, ROUTER_LO_LANE:ROUTER_LO_LANE + n_r].set(w_r_lo))
    b_router = jnp.zeros((LANES,), F32).at[:n_r].set(
        jnp.concatenate([b_router_group[0], b_router_expert[0]]))
    x1, h2, rt, ids = _mix(o_a, o_b, x, mod, g_out_a[0], g_out_b[0], g_ffn[0], w_out[0].astype(BF16),
                      w_router, b_router)

    tile_expert, next_expert, n_tiles, zero_fill, slot, n_pad = _route_plan(ids, n_tok)
    xs, modf = _dispatch(h2.reshape(n_tok, d), slot, zero_fill, n_pad, a_rep, w_ada_final, b_ada_final)
    modf = modf.reshape(b, 2, d)
    y = _moe(xs, w_gate[0], w_up[0], w_down[0], tile_expert, next_expert, n_tiles)

    return _final(x1, y, slot, rt, mod, modf, g_final)
```
